```python
import jax, jax.numpy as jnp
from jax import lax
import numpy as np

D_MODEL = 1024
BATCH = 4
SEQ = 4096
DEPTH = 2

ROPE_THETA = 500000.0
NORM_EPS = 1e-6
Q_BLOCK = 128
D_FF = 4 * D_MODEL
N_EVEN = (DEPTH + 1) // 2
N_ODD = DEPTH // 2
A_HEADS = 8
A_HEAD_DIM = 64
A_WIDTH = A_HEADS * A_HEAD_DIM
A_ROT_DIM = A_HEAD_DIM // 4
IDX_HEADS = 16
IDX_DIM = 64
TOPK_MAX = 256
B_GROUPS = 8
B_GROUP_DIM = 64
B_WIDTH = B_GROUPS * B_GROUP_DIM
CONV_WIDTH = 3
EVEN_COLS = (A_WIDTH, A_WIDTH, A_WIDTH,
             IDX_HEADS * IDX_DIM, IDX_DIM, IDX_HEADS,
             B_WIDTH, B_WIDTH, B_WIDTH)
EVEN_IN_WIDTH = sum(EVEN_COLS)
EVEN_OUT_WIDTH = A_WIDTH + B_WIDTH
C_HEADS = 16
C_NOPE_DIM = 64
C_ROPE_DIM = 32
C_V_DIM = 64
C_Q_RANK = 384
C_KV_RANK = 256

kernel_name = 'hybrid_dsa_shortconv_mla_sandwich'


def _rms_norm(x, gain):
    xf = x.astype(jnp.float32)
    xf = xf * lax.rsqrt(jnp.mean(xf * xf, axis=-1, keepdims=True) + NORM_EPS)
    return xf.astype(x.dtype) * gain


def _rope_cos_sin(positions, rot_dim):
    inv_freq = ROPE_THETA ** (-jnp.arange(0, rot_dim, 2, dtype=jnp.float32) / rot_dim)
    ang = positions.astype(jnp.float32)[..., None] * inv_freq
    return jnp.cos(ang)[:, :, None, :], jnp.sin(ang)[:, :, None, :]


def _rotate(x, cos, sin):
    xf = x.astype(jnp.float32)
    x1, x2 = jnp.split(xf, 2, axis=-1)
    return jnp.concatenate([x1 * cos - x2 * sin, x2 * cos + x1 * sin], axis=-1).astype(x.dtype)


def _partial_rope(x, cos, sin):
    rot = 2 * cos.shape[-1]
    return jnp.concatenate([_rotate(x[..., :rot], cos, sin), x[..., rot:]], axis=-1)


def _to_blocks(a):
    b, s = a.shape[:2]
    return jnp.moveaxis(a.reshape(b, s // Q_BLOCK, Q_BLOCK, *a.shape[2:]), 1, 0)


def _from_blocks(o):
    nb, b, qb = o.shape[:3]
    return jnp.moveaxis(o, 0, 1).reshape(b, nb * qb, *o.shape[3:])


def _dsa_attention(q, k, v, q_idx, k_idx, w_idx):
    bsz, s, h, dh = q.shape
    n_blocks = s // Q_BLOCK
    top_k = min(TOPK_MAX, s // 4)
    key_pos = jnp.arange(s, dtype=jnp.int32)
    gather = jax.vmap(lambda kb, ib: kb[ib])

    def block(args):
        qb, qib, wib, start = args
        q_pos = start + jnp.arange(Q_BLOCK, dtype=jnp.int32)
        causal = key_pos[None, :] <= q_pos[:, None]
        rel = jax.nn.relu(jnp.einsum('bqhd,bsd->bqhs', qib, k_idx).astype(jnp.float32))
        score = jnp.einsum('bqhs,bqh->bqs', rel, wib)
        score = jnp.where(causal[None], score, -jnp.inf)
        _, sel = lax.top_k(score, top_k)
        valid = sel <= q_pos[None, :, None]
        k_sel = gather(k, sel)
        v_sel = gather(v, sel)
        logits = jnp.einsum('bqhd,bqkhd->bhqk', qb, k_sel).astype(jnp.float32) * (dh ** -0.5)
        logits = jnp.where(valid[:, None], logits, -jnp.inf)
        p = jax.nn.softmax(logits, axis=-1).astype(v.dtype)
        return jnp.einsum('bhqk,bqkhd->bqhd', p, v_sel).reshape(bsz, Q_BLOCK, h * dh)

    starts = jnp.arange(n_blocks, dtype=jnp.int32) * Q_BLOCK
    out = lax.map(block, (_to_blocks(q), _to_blocks(q_idx), _to_blocks(w_idx), starts))
    return _from_blocks(out)


def _short_conv(x_in, gate_b, gate_c, conv_w):
    u = gate_c * x_in
    y = lax.conv_general_dilated(u, conv_w[:, None, :].astype(u.dtype), window_strides=(1,),
                                 padding=[(CONV_WIDTH - 1, 0)],
                                 dimension_numbers=('NWC', 'WIO', 'NWC'),
                                 feature_group_count=u.shape[-1])
    return gate_b * y


def _sparse_attn_conv_mixer(h, rope_quarter, w_in, conv_w, w_out):
    cos, sin = rope_quarter
    bsz, s, _ = h.shape
    proj = h @ w_in
    offsets = np.cumsum(EVEN_COLS)[:-1].tolist()
    q, k, v, q_idx, k_idx, w_idx, gate_b, gate_c, x_in = jnp.split(proj, offsets, axis=-1)
    q = _partial_rope(q.reshape(bsz, s, A_HEADS, A_HEAD_DIM), cos, sin)
    k = _partial_rope(k.reshape(bsz, s, A_HEADS, A_HEAD_DIM), cos, sin)
    v = v.reshape(bsz, s, A_HEADS, A_HEAD_DIM)
    q_idx = _partial_rope(q_idx.reshape(bsz, s, IDX_HEADS, IDX_DIM), cos, sin)
    k_idx = _partial_rope(k_idx[:, :, None, :], cos, sin)[:, :, 0, :]
    w_idx = w_idx.astype(jnp.float32) * (IDX_HEADS ** -0.5 * IDX_DIM ** -0.5)
    attn = _dsa_attention(q, k, v, q_idx, k_idx, w_idx)
    conv = _short_conv(x_in, gate_b, gate_c, conv_w)
    return jnp.concatenate([attn, conv], axis=-1) @ w_out


def _mla_mixer(h, rope_mla, w_dq, q_norm, w_uq, w_dkv, kv_norm, w_ukv, w_o):
    cos, sin = rope_mla
    bsz, s, _ = h.shape
    q = (_rms_norm(h @ w_dq, q_norm) @ w_uq).reshape(bsz, s, C_HEADS, C_NOPE_DIM + C_ROPE_DIM)
    q_nope = q[..., :C_NOPE_DIM]
    q_rope = _rotate(q[..., C_NOPE_DIM:], cos, sin)
    kv_a = h @ w_dkv
    c_kv = _rms_norm(kv_a[..., :C_KV_RANK], kv_norm)
    k_rope = _rotate(kv_a[..., C_KV_RANK:][:, :, None, :], cos, sin)[:, :, 0, :]
    kv = (c_kv @ w_ukv).reshape(bsz, s, C_HEADS, C_NOPE_DIM + C_V_DIM)
    k_nope, v = kv[..., :C_NOPE_DIM], kv[..., C_NOPE_DIM:]
    scale = (C_NOPE_DIM + C_ROPE_DIM) ** -0.5
    key_pos = jnp.arange(s, dtype=jnp.int32)

    def block(args):
        qnb, qrb, start = args
        q_pos = start + jnp.arange(Q_BLOCK, dtype=jnp.int32)
        causal = key_pos[None, :] <= q_pos[:, None]
        logits = (jnp.einsum('bqhd,bshd->bhqs', qnb, k_nope)
                  + jnp.einsum('bqhr,bsr->bhqs', qrb, k_rope)).astype(jnp.float32) * scale
        logits = jnp.where(causal[None, None], logits, -jnp.inf)
        p = jax.nn.softmax(logits, axis=-1).astype(v.dtype)
        return jnp.einsum('bhqs,bshd->bqhd', p, v).reshape(bsz, Q_BLOCK, C_HEADS * C_V_DIM)

    starts = jnp.arange(s // Q_BLOCK, dtype=jnp.int32) * Q_BLOCK
    out = lax.map(block, (_to_blocks(q_nope), _to_blocks(q_rope), starts))
    return _from_blocks(out) @ w_o


def _squared_relu_mlp(h, w1, w2):
    return jnp.square(jax.nn.relu(h @ w1)) @ w2


def setup_inputs(seed: int = 0) -> dict:
    key = jax.random.key(seed)
    ks = jax.random.split(key, 20)

    def dense(k, shape, fan_in):
        return jax.random.normal(k, shape, jnp.float32) * (fan_in ** -0.5)

    def gain(k, shape):
        return 1.0 + 0.02 * jax.random.normal(k, shape, jnp.float32)

    return {
        'x': jax.random.normal(ks[0], (BATCH, SEQ, D_MODEL), jnp.float32),
        'positions': jnp.broadcast_to(jnp.arange(SEQ, dtype=jnp.int32), (BATCH, SEQ)),
        'norm_mix_pre': gain(ks[1], (DEPTH, D_MODEL)),
        'norm_mix_post': gain(ks[2], (DEPTH, D_MODEL)),
        'norm_ffn_pre': gain(ks[3], (DEPTH, D_MODEL)),
        'norm_ffn_post': gain(ks[4], (DEPTH, D_MODEL)),
        'even_w_in': dense(ks[5], (N_EVEN, D_MODEL, EVEN_IN_WIDTH), D_MODEL),
        'even_conv_w': dense(ks[6], (N_EVEN, CONV_WIDTH, B_WIDTH), CONV_WIDTH),
        'even_w_out': dense(ks[7], (N_EVEN, EVEN_OUT_WIDTH, D_MODEL), EVEN_OUT_WIDTH),
        'odd_w_dq': dense(ks[8], (N_ODD, D_MODEL, C_Q_RANK), D_MODEL),
        'odd_q_norm': gain(ks[9], (N_ODD, C_Q_RANK)),
        'odd_w_uq': dense(ks[10], (N_ODD, C_Q_RANK, C_HEADS * (C_NOPE_DIM + C_ROPE_DIM)), C_Q_RANK),
        'odd_w_dkv': dense(ks[11], (N_ODD, D_MODEL, C_KV_RANK + C_ROPE_DIM), D_MODEL),
        'odd_kv_norm': gain(ks[12], (N_ODD, C_KV_RANK)),
        'odd_w_ukv': dense(ks[13], (N_ODD, C_KV_RANK, C_HEADS * (C_NOPE_DIM + C_V_DIM)), C_KV_RANK),
        'odd_w_o': dense(ks[14], (N_ODD, C_HEADS * C_V_DIM, D_MODEL), C_HEADS * C_V_DIM),
        'mlp_w1': dense(ks[15], (DEPTH, D_MODEL, D_FF), D_MODEL),
        'mlp_w2': dense(ks[16], (DEPTH, D_FF, D_MODEL), D_FF),
    }


def reference(x, positions, norm_mix_pre, norm_mix_post, norm_ffn_pre, norm_ffn_post,
              even_w_in, even_conv_w, even_w_out,
              odd_w_dq, odd_q_norm, odd_w_uq, odd_w_dkv, odd_kv_norm, odd_w_ukv, odd_w_o,
              mlp_w1, mlp_w2):
    rope_quarter = _rope_cos_sin(positions, A_ROT_DIM)
    rope_mla = _rope_cos_sin(positions, C_ROPE_DIM)
    for layer in range(DEPTH):
        j = layer // 2
        hn = _rms_norm(x, norm_mix_pre[layer])
        if layer % 2 == 0:
            mix = _sparse_attn_conv_mixer(hn, rope_quarter, even_w_in[j], even_conv_w[j], even_w_out[j])
        else:
            mix = _mla_mixer(hn, rope_mla, odd_w_dq[j], odd_q_norm[j], odd_w_uq[j],
                             odd_w_dkv[j], odd_kv_norm[j], odd_w_ukv[j], odd_w_o[j])
        x = x + _rms_norm(mix, norm_mix_post[layer])
        hn = _rms_norm(x, norm_ffn_pre[layer])
        x = x + _rms_norm(_squared_relu_mlp(hn, mlp_w1[layer], mlp_w2[layer]), norm_ffn_post[layer])
    return x
```

```python
import functools

import jax
import jax.numpy as jnp
from jax import lax
from jax.experimental import pallas as pl
from jax.experimental.pallas import tpu as pltpu

ROPE_THETA = 500000.0
NORM_EPS = 1e-6
LANES = 128
NEG_BIG = -1e30

A_HEADS = 8
A_HEAD_DIM = 64
A_WIDTH = A_HEADS * A_HEAD_DIM
A_ROT_DIM = A_HEAD_DIM // 4
IDX_HEADS = 16
IDX_DIM = 64
TOPK_MAX = 256
B_WIDTH = 512
CONV_WIDTH = 3
C_HEADS = 16
C_NOPE_DIM = 64
C_ROPE_DIM = 32
C_V_DIM = 64
C_Q_RANK = 384
C_KV_RANK = 256

KEY_LOWEST_FINITE = -2139095040

ROW_TILE = 256
MLP_ROW_TILE = 512
MLP_FF_TILE = 1024
DSA_Q_TILE = 128
DSA_KEY_CHUNK = 512
MLA_Q_TILE = 256
MLA_KEY_CHUNK = 512
MLA_HEAD_GROUP = 4
VMEM_LIMIT = 56 * 1024 * 1024


def _dot(a, b):
    return jnp.dot(a, b, preferred_element_type=jnp.float32)


def _dot_nt(a, b):
    return lax.dot_general(a, b, (((1,), (1,)), ((), ())), preferred_element_type=jnp.float32)


def _rms(x, gain):
    ms = jnp.mean(x * x, axis=-1, keepdims=True)
    return x * lax.rsqrt(ms + NORM_EPS) * gain


def _rope_tile(x, c, s_next, s_prev, half):
    return (x * c + pltpu.roll(x, LANES - half, axis=1) * s_next
            + pltpu.roll(x, half, axis=1) * s_prev)


def _params(*semantics):
    return pltpu.CompilerParams(dimension_semantics=semantics, vmem_limit_bytes=VMEM_LIMIT)


_EQ, _EK, _EV, _EQI, _EKI, _EWI, _EGB, _EGC, _EXI, _EEND = (
    0, 512, 1024, 1536, 2560, 2688, 2816, 3328, 3840, 4352)


def _even_in_kernel(seq_tiles, x_ref, g_ref, w_ref, c_ref, sn_ref, sp_ref, cw_ref,
                    q_ref, k_ref, v_ref, qi_ref, ki_ref, wi_ref, conv_ref, carry_ref):
    i = pl.program_id(0)
    tm = x_ref.shape[0]
    hn = _rms(x_ref[...], g_ref[...]).astype(jnp.bfloat16)
    c, sn, sp = c_ref[...], sn_ref[...], sp_ref[...]
    half = A_ROT_DIM // 2

    def rope_section(start, width, out_ref, scale=None):
        for t in range(width // LANES):
            y = _dot(hn, w_ref[:, start + t * LANES:start + (t + 1) * LANES])
            y = _rope_tile(y, c, sn, sp, half)
            if scale is not None:
                y = y * scale
            out_ref[:, t * LANES:(t + 1) * LANES] = y.astype(out_ref.dtype)

    rope_section(_EQ, A_WIDTH, q_ref, A_HEAD_DIM ** -0.5)
    rope_section(_EK, A_WIDTH, k_ref)
    v_ref[...] = _dot(hn, w_ref[:, _EV:_EQI]).astype(v_ref.dtype)
    rope_section(_EQI, IDX_HEADS * IDX_DIM, qi_ref)
    rope_section(_EKI, LANES, ki_ref)
    wi_ref[...] = _dot(hn, w_ref[:, _EWI:_EGB]) * (IDX_HEADS ** -0.5 * IDX_DIM ** -0.5)

    gate_b = _dot(hn, w_ref[:, _EGB:_EGC])
    u = _dot(hn, w_ref[:, _EGC:_EXI]) * _dot(hn, w_ref[:, _EXI:_EEND])
    seq_start = (i % seq_tiles) == 0
    prev = jnp.where(seq_start, 0.0, carry_ref[...])
    carry_ref[...] = u[tm - 8:, :]
    row = lax.broadcasted_iota(jnp.int32, u.shape, 0)
    u1 = jnp.where(row == 0, prev[7:8, :], pltpu.roll(u, 1, axis=0))
    u2 = pltpu.roll(u, 2, axis=0)
    u2 = jnp.where(row == 0, prev[6:7, :], jnp.where(row == 1, prev[7:8, :], u2))
    y = cw_ref[0:1, :] * u2 + cw_ref[1:2, :] * u1 + cw_ref[2:3, :] * u
    conv_ref[...] = (gate_b * y).astype(conv_ref.dtype)


def _even_in_proj(x, gain, w_packed, tabs, conv_w, seq):
    n, d = x.shape
    tm = ROW_TILE
    row = lambda width: pl.BlockSpec((tm, width), lambda i: (i, 0))
    full = lambda a: pl.BlockSpec(a.shape, lambda i: (0,) * a.ndim)
    bf = jnp.bfloat16
    out_shape = (
        jax.ShapeDtypeStruct((n, A_WIDTH), bf), jax.ShapeDtypeStruct((n, A_WIDTH), bf),
        jax.ShapeDtypeStruct((n, A_WIDTH), bf), jax.ShapeDtypeStruct((n, IDX_HEADS * IDX_DIM), bf),
        jax.ShapeDtypeStruct((n, LANES), bf), jax.ShapeDtypeStruct((n, LANES), jnp.float32),
        jax.ShapeDtypeStruct((n, B_WIDTH), bf))
    return pl.pallas_call(
        functools.partial(_even_in_kernel, seq // tm),
        grid=(n // tm,),
        in_specs=[row(d), full(gain), full(w_packed), row(LANES), row(LANES), row(LANES),
                  full(conv_w)],
        out_specs=(row(A_WIDTH), row(A_WIDTH), row(A_WIDTH), row(IDX_HEADS * IDX_DIM),
                   row(LANES), row(LANES), row(B_WIDTH)),
        out_shape=out_shape,
        scratch_shapes=[pltpu.VMEM((8, B_WIDTH), jnp.float32)],
        compiler_params=_params("arbitrary"),
        name="even_in_proj",
    )(x, gain, w_packed, *tabs, conv_w)


def _key_to_float(key):
    bits = key ^ ((key >> 31) & jnp.int32(0x7FFFFFFF))
    return lax.bitcast_convert_type(bits, jnp.float32)


def _softmax_step(s, vp, m_ref, l_ref, acc_ref, h):
    kc = s.shape[1]
    m_prev = m_ref[h]
    m_new = jnp.maximum(m_prev, jnp.max(s, axis=1, keepdims=True))
    alpha = jnp.exp(m_prev - m_new)
    p = jnp.concatenate(
        [jnp.exp(s[:, t * LANES:(t + 1) * LANES] - m_new) for t in range(kc // LANES)], axis=1)
    l_ref[h] = alpha * l_ref[h] + jnp.sum(p, axis=1, keepdims=True)
    acc_ref[h] = alpha * acc_ref[h] + _dot(p.astype(jnp.bfloat16), vp)
    m_ref[h] = m_new


def _dsa_kernel(top_k, q_ref, qi_ref, wi_ref, ki_ref, k_ref, v_ref, o_ref,
                sc_ref, qm_ref, qp_ref, m_ref, l_ref, acc_ref):
    i = pl.program_id(1)
    tq, kc = DSA_Q_TILE, DSA_KEY_CHUNK
    n_chunks = (i * tq) // kc + 1
    lane = lax.broadcasted_iota(jnp.int32, (tq, LANES), 1)
    low = lane < IDX_DIM

    for j in range(IDX_HEADS // 2):
        t = qi_ref[:, j * LANES:(j + 1) * LANES].astype(jnp.float32)
        qm_ref[j, 0:tq, :] = jnp.where(low, t, 0.0).astype(jnp.bfloat16)
        qm_ref[j, tq:2 * tq, :] = jnp.where(low, 0.0, t).astype(jnp.bfloat16)
    for j in range(A_HEADS // 2):
        t = q_ref[:, j * LANES:(j + 1) * LANES].astype(jnp.float32)
        qp_ref[2 * j] = jnp.where(low, t, 0.0).astype(jnp.bfloat16)
        qp_ref[2 * j + 1] = jnp.where(low, 0.0, t).astype(jnp.bfloat16)
    w_t = wi_ref[...].T

    q_pos = i * tq + lax.broadcasted_iota(jnp.int32, (kc, tq), 1)
    k_off = lax.broadcasted_iota(jnp.int32, (kc, tq), 0)

    def score_chunk(c, carry):
        base = pl.multiple_of(c * kc, kc)
        kch = ki_ref[pl.ds(base, kc), :]
        acc = jnp.zeros((kc, tq), jnp.float32)
        for j in range(IDX_HEADS // 2):
            r = _dot_nt(kch, qm_ref[j])
            acc = acc + jnp.maximum(r[:, :tq], 0.0) * w_t[2 * j:2 * j + 1, :]
            acc = acc + jnp.maximum(r[:, tq:], 0.0) * w_t[2 * j + 1:2 * j + 2, :]
        sc_ref[pl.ds(base, kc), :] = jnp.where(base + k_off <= q_pos, acc, -jnp.inf)
        return carry

    lax.fori_loop(0, n_chunks, score_chunk, 0)

    def bisect(step, lo):
        cand = lo + jnp.left_shift(jnp.int32(1), 31 - step)
        cand_f = _key_to_float(cand)

        def count_chunk(c, cnt):
            base = pl.multiple_of(c * kc, kc)
            hit = jnp.where(sc_ref[pl.ds(base, kc), :] >= cand_f, 1.0, 0.0)
            return cnt + jnp.sum(hit.reshape(kc // 8, 8, tq), axis=0)

        cnt = lax.fori_loop(0, n_chunks, count_chunk, jnp.zeros((8, tq), jnp.float32))
        total = jnp.sum(cnt, axis=0, keepdims=True)
        return jnp.where(total >= top_k, cand, lo)

    lo = lax.fori_loop(0, 32, bisect, jnp.full((1, tq), jnp.iinfo(jnp.int32).min, jnp.int32))
    thr = _key_to_float(jnp.maximum(lo, KEY_LOWEST_FINITE))

    m_ref[...] = jnp.full(m_ref.shape, NEG_BIG, jnp.float32)
    l_ref[...] = jnp.zeros(l_ref.shape, jnp.float32)
    acc_ref[...] = jnp.zeros(acc_ref.shape, jnp.float32)

    def attn_chunk(c, carry):
        base = pl.multiple_of(c * kc, kc)
        bias = jnp.where(sc_ref[pl.ds(base, kc), :] >= thr, 0.0, NEG_BIG).T
        for h in range(A_HEADS):
            cols = slice((h // 2) * LANES, (h // 2 + 1) * LANES)
            s = _dot_nt(qp_ref[h], k_ref[pl.ds(base, kc), cols]) + bias
            _softmax_step(s, v_ref[pl.ds(base, kc), cols], m_ref, l_ref, acc_ref, h)
        return carry

    lax.fori_loop(0, n_chunks, attn_chunk, 0)

    for j in range(A_HEADS // 2):
        even = acc_ref[2 * j] / l_ref[2 * j]
        odd = acc_ref[2 * j + 1] / l_ref[2 * j + 1]
        o_ref[:, j * LANES:(j + 1) * LANES] = jnp.where(low, even, odd).astype(o_ref.dtype)


def _dsa_attention(q, qi, wi, ki, k, v, batch, seq):
    n = q.shape[0]
    tq = DSA_Q_TILE
    nq = seq // tq
    top_k = min(TOPK_MAX, seq // 4)
    qrow = lambda width: pl.BlockSpec((tq, width), lambda b, i: (b * nq + i, 0))
    per_batch = lambda width: pl.BlockSpec((seq, width), lambda b, i: (b, 0))
    return pl.pallas_call(
        functools.partial(_dsa_kernel, top_k),
        grid=(batch, nq),
        in_specs=[qrow(A_WIDTH), qrow(IDX_HEADS * IDX_DIM), qrow(LANES),
                  per_batch(LANES), per_batch(A_WIDTH), per_batch(A_WIDTH)],
        out_specs=qrow(A_WIDTH),
        out_shape=jax.ShapeDtypeStruct((n, A_WIDTH), jnp.bfloat16),
        scratch_shapes=[
            pltpu.VMEM((seq, tq), jnp.float32),
            pltpu.VMEM((IDX_HEADS // 2, 2 * tq, LANES), jnp.bfloat16),
            pltpu.VMEM((A_HEADS, tq, LANES), jnp.bfloat16),
            pltpu.VMEM((A_HEADS, tq, LANES), jnp.float32),
            pltpu.VMEM((A_HEADS, tq, LANES), jnp.float32),
            pltpu.VMEM((A_HEADS, tq, LANES), jnp.float32),
        ],
        compiler_params=_params("arbitrary", "arbitrary"),
        name="dsa_attention",
    )(q, qi, wi, ki, k, v)


def _out_proj_kernel(n_in, *refs):
    a_refs = refs[:n_in]
    w_refs = refs[n_in:2 * n_in]
    x_ref, g_ref, o_ref = refs[2 * n_in:]
    y = _dot(a_refs[0][...], w_refs[0][...])
    for a_ref, w_ref in zip(a_refs[1:], w_refs[1:]):
        y = y + _dot(a_ref[...], w_ref[...])
    o_ref[...] = x_ref[...] + _rms(y, g_ref[...])


def _out_proj(acts, weights, x, gain):
    n, d = x.shape
    tm = ROW_TILE
    row = lambda a: pl.BlockSpec((tm, a.shape[1]), lambda i: (i, 0))
    full = lambda a: pl.BlockSpec(a.shape, lambda i: (0,) * a.ndim)
    return pl.pallas_call(
        functools.partial(_out_proj_kernel, len(acts)),
        grid=(n // tm,),
        in_specs=[row(a) for a in acts] + [full(w) for w in weights] + [row(x), full(gain)],
        out_specs=row(x),
        out_shape=jax.ShapeDtypeStruct((n, d), jnp.float32),
        compiler_params=_params("arbitrary"),
        name="out_proj",
    )(*acts, *weights, x, gain)


def _mlp_kernel(x_ref, gpre_ref, gpost_ref, w1_ref, w2_ref, o_ref, hn_ref, acc_ref):
    f = pl.program_id(1)

    @pl.when(f == 0)
    def _():
        hn_ref[...] = _rms(x_ref[...], gpre_ref[...]).astype(jnp.bfloat16)
        acc_ref[...] = jnp.zeros(acc_ref.shape, jnp.float32)

    h = jnp.maximum(_dot(hn_ref[...], w1_ref[...]), 0.0)
    acc_ref[...] += _dot((h * h).astype(jnp.bfloat16), w2_ref[...])

    @pl.when(f == pl.num_programs(1) - 1)
    def _():
        o_ref[...] = x_ref[...] + _rms(acc_ref[...], gpost_ref[...])


def _mlp(x, gpre, gpost, w1, w2):
    n, d = x.shape
    ff = w1.shape[1]
    tm, tf = MLP_ROW_TILE, MLP_FF_TILE
    return pl.pallas_call(
        _mlp_kernel,
        grid=(n // tm, ff // tf),
        in_specs=[pl.BlockSpec((tm, d), lambda i, f: (i, 0)),
                  pl.BlockSpec((1, d), lambda i, f: (0, 0)),
                  pl.BlockSpec((1, d), lambda i, f: (0, 0)),
                  pl.BlockSpec((d, tf), lambda i, f: (0, f)),
                  pl.BlockSpec((tf, d), lambda i, f: (f, 0))],
        out_specs=pl.BlockSpec((tm, d), lambda i, f: (i, 0)),
        out_shape=jax.ShapeDtypeStruct((n, d), jnp.float32),
        scratch_shapes=[pltpu.VMEM((tm, d), jnp.bfloat16), pltpu.VMEM((tm, d), jnp.float32)],
        compiler_params=_params("arbitrary", "arbitrary"),
        name="mlp",
    )(x, gpre, gpost, w1, w2)


_MLA_HEAD_PAD = LANES
_DQ, _DKV, _DKR, _DEND = 0, C_Q_RANK, C_Q_RANK + C_KV_RANK, C_Q_RANK + C_KV_RANK + LANES


def _mla_in_kernel(x_ref, g_ref, wd_ref, qn_ref, kvn_ref, wuq_ref, wuk_ref, wuv_ref,
                   c_ref, sn_ref, sp_ref, q_ref, k_ref, v_ref):
    hn = _rms(x_ref[...], g_ref[...]).astype(jnp.bfloat16)
    c, sn, sp = c_ref[...], sn_ref[...], sp_ref[...]
    half = C_ROPE_DIM // 2
    scale = (C_NOPE_DIM + C_ROPE_DIM) ** -0.5

    qa = _rms(_dot(hn, wd_ref[:, _DQ:_DKV]), qn_ref[...]).astype(jnp.bfloat16)
    ckv = _rms(_dot(hn, wd_ref[:, _DKV:_DKR]), kvn_ref[...]).astype(jnp.bfloat16)
    kr = pltpu.roll(_dot(hn, wd_ref[:, _DKR:_DEND]), C_NOPE_DIM, axis=1)
    kr = _rope_tile(kr, c, sn, sp, half)
    for h in range(C_HEADS):
        cols = slice(h * LANES, (h + 1) * LANES)
        qh = _rope_tile(_dot(qa, wuq_ref[:, cols]), c, sn, sp, half) * scale
        q_ref[:, cols] = qh.astype(q_ref.dtype)
        k_ref[:, cols] = (_dot(ckv, wuk_ref[:, cols]) + kr).astype(k_ref.dtype)
    v_ref[...] = _dot(ckv, wuv_ref[...]).astype(v_ref.dtype)


def _mla_in_proj(x, gain, w_down, q_norm, kv_norm, w_uq, w_uk, w_uv, tabs):
    n, d = x.shape
    tm = ROW_TILE
    row = lambda width: pl.BlockSpec((tm, width), lambda i: (i, 0))
    full = lambda a: pl.BlockSpec(a.shape, lambda i: (0,) * a.ndim)
    bf = jnp.bfloat16
    return pl.pallas_call(
        _mla_in_kernel,
        grid=(n // tm,),
        in_specs=[row(d), full(gain), full(w_down), full(q_norm), full(kv_norm),
                  full(w_uq), full(w_uk), full(w_uv), row(LANES), row(LANES), row(LANES)],
        out_specs=(row(C_HEADS * LANES), row(C_HEADS * LANES), row(C_HEADS * C_V_DIM)),
        out_shape=(jax.ShapeDtypeStruct((n, C_HEADS * LANES), bf),
                   jax.ShapeDtypeStruct((n, C_HEADS * LANES), bf),
                   jax.ShapeDtypeStruct((n, C_HEADS * C_V_DIM), bf)),
        compiler_params=_params("arbitrary"),
        name="mla_in_proj",
    )(x, gain, w_down, q_norm, kv_norm, w_uq, w_uk, w_uv, *tabs)


def _mla_attn_kernel(q_ref, k_ref, v_ref, o_ref, m_ref, l_ref, acc_ref):
    i = pl.program_id(2)
    tq, kc, hg = MLA_Q_TILE, MLA_KEY_CHUNK, MLA_HEAD_GROUP
    n_full = (i * tq) // kc

    m_ref[...] = jnp.full(m_ref.shape, NEG_BIG, jnp.float32)
    l_ref[...] = jnp.zeros(l_ref.shape, jnp.float32)
    acc_ref[...] = jnp.zeros(acc_ref.shape, jnp.float32)

    def chunk(c, masked):
        base = pl.multiple_of(c * kc, kc)
        if masked:
            q_pos = i * tq + lax.broadcasted_iota(jnp.int32, (tq, kc), 0)
            k_pos = base + lax.broadcasted_iota(jnp.int32, (tq, kc), 1)
            bias = jnp.where(k_pos <= q_pos, 0.0, NEG_BIG)
        for h in range(hg):
            s = _dot_nt(q_ref[:, h * LANES:(h + 1) * LANES],
                        k_ref[pl.ds(base, kc), h * LANES:(h + 1) * LANES])
            if masked:
                s = s + bias
            vp = v_ref[pl.ds(base, kc), (h // 2) * LANES:(h // 2 + 1) * LANES]
            _softmax_step(s, vp, m_ref, l_ref, acc_ref, h)

    def full_chunk(c, carry):
        chunk(c, False)
        return carry

    lax.fori_loop(0, n_full, full_chunk, 0)
    chunk(n_full, True)

    low = lax.broadcasted_iota(jnp.int32, (tq, LANES), 1) < C_V_DIM
    for j in range(hg // 2):
        even = acc_ref[2 * j] / l_ref[2 * j]
        odd = acc_ref[2 * j + 1] / l_ref[2 * j + 1]
        o_ref[:, j * LANES:(j + 1) * LANES] = jnp.where(low, even, odd).astype(o_ref.dtype)


def _mla_attention(q, k, v, batch, seq):
    n = q.shape[0]
    tq, hg = MLA_Q_TILE, MLA_HEAD_GROUP
    nq = seq // tq
    return pl.pallas_call(
        _mla_attn_kernel,
        grid=(batch, C_HEADS // hg, nq),
        in_specs=[pl.BlockSpec((tq, hg * LANES), lambda b, g, i: (b * nq + i, g)),
                  pl.BlockSpec((seq, hg * LANES), lambda b, g, i: (b, g)),
                  pl.BlockSpec((seq, hg * C_V_DIM), lambda b, g, i: (b, g))],
        out_specs=pl.BlockSpec((tq, hg * C_V_DIM), lambda b, g, i: (b * nq + i, g)),
        out_shape=jax.ShapeDtypeStruct((n, C_HEADS * C_V_DIM), jnp.bfloat16),
        scratch_shapes=[pltpu.VMEM((hg, tq, LANES), jnp.float32),
                        pltpu.VMEM((hg, tq, LANES), jnp.float32),
                        pltpu.VMEM((hg, tq, LANES), jnp.float32)],
        compiler_params=_params("arbitrary", "arbitrary", "arbitrary"),
        name="mla_attention",
    )(q, k, v)


def _rope_tables(positions, rot_dim, lead, period):
    inv_freq = ROPE_THETA ** (-jnp.arange(0, rot_dim, 2, dtype=jnp.float32) / rot_dim)
    ang = positions.astype(jnp.float32).reshape(-1)[:, None] * inv_freq
    cos, sin = jnp.cos(ang), jnp.sin(ang)
    n, half = cos.shape
    ones = lambda w: jnp.ones((n, w), jnp.float32)
    zeros = lambda w: jnp.zeros((n, w), jnp.float32)
    tail = period - lead - rot_dim
    c = jnp.concatenate([ones(lead), cos, cos, ones(tail)], axis=1)
    s_next = jnp.concatenate([zeros(lead), -sin, zeros(half + tail)], axis=1)
    s_prev = jnp.concatenate([zeros(lead + half), sin, zeros(tail)], axis=1)
    reps = LANES // period
    return tuple(jnp.tile(t, (1, reps)) for t in (c, s_next, s_prev))


def _pack_even_w_in(w):
    d = w.shape[0]
    q, k, v, qi, ki, wi, gb, gc, xi = jnp.split(
        w, [512, 1024, 1536, 2560, 2624, 2640, 3152, 3664], axis=1)
    pad = jnp.zeros((d, LANES - IDX_HEADS), w.dtype)
    return jnp.concatenate([q, k, v, qi, ki, ki, wi, pad, gb, gc, xi], axis=1).astype(jnp.bfloat16)


def _pad_heads(w, heads, width):
    r = w.shape[0]
    w = w.reshape(r, heads, width)
    return jnp.pad(w, ((0, 0), (0, 0), (0, LANES - width))).reshape(r, heads * LANES)


def kernel(x, positions, norm_mix_pre, norm_mix_post, norm_ffn_pre, norm_ffn_post, even_w_in, even_conv_w, even_w_out, odd_w_dq, odd_q_norm, odd_w_uq, odd_w_dkv, odd_kv_norm, odd_w_ukv, odd_w_o, mlp_w1, mlp_w2):
    batch, seq, d = x.shape
    depth = norm_mix_pre.shape[0]
    bf = jnp.bfloat16
    tabs_a = _rope_tables(positions, A_ROT_DIM, 0, A_HEAD_DIM)
    tabs_c = _rope_tables(positions, C_ROPE_DIM, C_NOPE_DIM, LANES)
    h = x.reshape(batch * seq, d)
    row = lambda a: a.reshape(1, -1)
    for layer in range(depth):
        j = layer // 2
        g_pre, g_post = row(norm_mix_pre[layer]), row(norm_mix_post[layer])
        if layer % 2 == 0:
            q, k, v, qi, ki, wi, conv = _even_in_proj(
                h, g_pre, _pack_even_w_in(even_w_in[j]), tabs_a, even_conv_w[j], seq)
            attn = _dsa_attention(q, qi, wi, ki, k, v, batch, seq)
            w_out = even_w_out[j].astype(bf)
            h = _out_proj([attn, conv], [w_out[:A_WIDTH], w_out[A_WIDTH:]], h, g_post)
        else:
            w_dkv = odd_w_dkv[j]
            w_down = jnp.concatenate(
                [odd_w_dq[j], w_dkv, jnp.zeros((d, LANES - C_ROPE_DIM), w_dkv.dtype)], axis=1)
            w_ukv = odd_w_ukv[j].reshape(C_KV_RANK, C_HEADS, C_NOPE_DIM + C_V_DIM)
            w_uk = _pad_heads(w_ukv[:, :, :C_NOPE_DIM].reshape(C_KV_RANK, -1), C_HEADS, C_NOPE_DIM)
            w_uv = w_ukv[:, :, C_NOPE_DIM:].reshape(C_KV_RANK, -1)
            w_uq = _pad_heads(odd_w_uq[j], C_HEADS, C_NOPE_DIM + C_ROPE_DIM)
            q, k, v = _mla_in_proj(h, g_pre, w_down.astype(bf), row(odd_q_norm[j]),
                                   row(odd_kv_norm[j]), w_uq.astype(bf), w_uk.astype(bf),
                                   w_uv.astype(bf), tabs_c)
            attn = _mla_attention(q, k, v, batch, seq)
            h = _out_proj([attn], [odd_w_o[j].astype(bf)], h, g_post)
        h = _mlp(h, row(norm_ffn_pre[layer]), row(norm_ffn_post[layer]),
                 mlp_w1[layer].astype(bf), mlp_w2[layer].astype(bf))
    return h.reshape(batch, seq, d)
```

```python
import functools

import jax
import jax.numpy as jnp
from jax import lax
from jax.experimental import pallas as pl
from jax.experimental.pallas import tpu as pltpu

ROPE_THETA = 500000.0
NORM_EPS = 1e-6
LANES = 128
NEG_BIG = -1e30

A_HEADS = 8
A_HEAD_DIM = 64
A_WIDTH = A_HEADS * A_HEAD_DIM
A_ROT_DIM = A_HEAD_DIM // 4
IDX_HEADS = 16
IDX_DIM = 64
TOPK_MAX = 256
B_WIDTH = 512
CONV_WIDTH = 3
C_HEADS = 16
C_NOPE_DIM = 64
C_ROPE_DIM = 32
C_V_DIM = 64
C_Q_RANK = 384
C_KV_RANK = 256

KEY_LOWEST_FINITE = -2139095040

ROW_TILE = 256
MLP_ROW_TILE = 512
MLP_FF_TILE = 1024
DSA_Q_TILE = 256
DSA_KEY_CHUNK = 512
DSA_COUNT_ROWS = 64
MLA_Q_TILE = 256
MLA_KEY_CHUNK = 512
MLA_HEAD_GROUP = 8
VMEM_LIMIT = 56 * 1024 * 1024


def _dot(a, b):
    return jnp.dot(a, b, preferred_element_type=jnp.float32)


def _dot_nt(a, b):
    return lax.dot_general(a, b, (((1,), (1,)), ((), ())), preferred_element_type=jnp.float32)


def _rms(x, gain):
    ms = jnp.mean(x * x, axis=-1, keepdims=True)
    return x * lax.rsqrt(ms + NORM_EPS) * gain


def _rope_tile(x, c, s_next, s_prev, half):
    return (x * c + pltpu.roll(x, LANES - half, axis=1) * s_next
            + pltpu.roll(x, half, axis=1) * s_prev)


def _params(*semantics):
    return pltpu.CompilerParams(dimension_semantics=semantics, vmem_limit_bytes=VMEM_LIMIT)


_EQ, _EK, _EQI, _EKI, _EWI, _EGB, _EGC, _EXI, _EEND = (
    0, 512, 1024, 2048, 2176, 2304, 2816, 3328, 3840)


def _even_in_kernel(seq_tiles, x_ref, g_ref, w_ref, wvt_ref, c_ref, sn_ref, sp_ref, cw_ref,
                    q_ref, k_ref, vt_ref, qi_ref, ki_ref, wi_ref, conv_ref, carry_ref):
    i = pl.program_id(0)
    tm = x_ref.shape[0]
    hn = _rms(x_ref[...], g_ref[...]).astype(jnp.bfloat16)
    c, sn, sp = c_ref[...], sn_ref[...], sp_ref[...]
    half = A_ROT_DIM // 2

    def rope_section(start, width, out_ref, scale=None):
        for t in range(width // LANES):
            y = _dot(hn, w_ref[:, start + t * LANES:start + (t + 1) * LANES])
            y = _rope_tile(y, c, sn, sp, half)
            if scale is not None:
                y = y * scale
            out_ref[:, t * LANES:(t + 1) * LANES] = y.astype(out_ref.dtype)

    rope_section(_EQ, A_WIDTH, q_ref, A_HEAD_DIM ** -0.5)
    rope_section(_EK, A_WIDTH, k_ref)
    vt_ref[...] = _dot_nt(wvt_ref[...], hn).astype(vt_ref.dtype)
    rope_section(_EQI, IDX_HEADS * IDX_DIM, qi_ref)
    rope_section(_EKI, LANES, ki_ref)
    wi_ref[...] = _dot(hn, w_ref[:, _EWI:_EGB]) * (IDX_HEADS ** -0.5 * IDX_DIM ** -0.5)

    gate_b = _dot(hn, w_ref[:, _EGB:_EGC])
    u = _dot(hn, w_ref[:, _EGC:_EXI]) * _dot(hn, w_ref[:, _EXI:_EEND])
    seq_start = (i % seq_tiles) == 0
    prev = jnp.where(seq_start, 0.0, carry_ref[...])
    carry_ref[...] = u[tm - 8:, :]
    row = lax.broadcasted_iota(jnp.int32, u.shape, 0)
    u1 = jnp.where(row == 0, prev[7:8, :], pltpu.roll(u, 1, axis=0))
    u2 = pltpu.roll(u, 2, axis=0)
    u2 = jnp.where(row == 0, prev[6:7, :], jnp.where(row == 1, prev[7:8, :], u2))
    y = cw_ref[0:1, :] * u2 + cw_ref[1:2, :] * u1 + cw_ref[2:3, :] * u
    conv_ref[...] = (gate_b * y).astype(conv_ref.dtype)


def _even_in_proj(x, gain, w_packed, w_vt, tabs, conv_w, seq):
    n, d = x.shape
    tm = ROW_TILE
    row = lambda width: pl.BlockSpec((tm, width), lambda i: (i, 0))
    full = lambda a: pl.BlockSpec(a.shape, lambda i: (0,) * a.ndim)
    bf = jnp.bfloat16
    out_shape = (
        jax.ShapeDtypeStruct((n, A_WIDTH), bf), jax.ShapeDtypeStruct((n, A_WIDTH), bf),
        jax.ShapeDtypeStruct((A_WIDTH, n), bf), jax.ShapeDtypeStruct((n, IDX_HEADS * IDX_DIM), bf),
        jax.ShapeDtypeStruct((n, LANES), bf), jax.ShapeDtypeStruct((n, LANES), jnp.float32),
        jax.ShapeDtypeStruct((n, B_WIDTH), bf))
    return pl.pallas_call(
        functools.partial(_even_in_kernel, seq // tm),
        grid=(n // tm,),
        in_specs=[row(d), full(gain), full(w_packed), full(w_vt), row(LANES), row(LANES),
                  row(LANES), full(conv_w)],
        out_specs=(row(A_WIDTH), row(A_WIDTH), pl.BlockSpec((A_WIDTH, tm), lambda i: (0, i)),
                   row(IDX_HEADS * IDX_DIM), row(LANES), row(LANES), row(B_WIDTH)),
        out_shape=out_shape,
        scratch_shapes=[pltpu.VMEM((8, B_WIDTH), jnp.float32)],
        compiler_params=_params("arbitrary"),
        name="even_in_proj",
    )(x, gain, w_packed, w_vt, *tabs, conv_w)


def _key_to_float(key):
    bits = key ^ ((key >> 31) & jnp.int32(0x7FFFFFFF))
    return lax.bitcast_convert_type(bits, jnp.float32)


def _softmax_step_t(s_t, v_t, m_ref, l_ref, acc_ref, h):
    m_prev = m_ref[h]
    m_new = jnp.maximum(m_prev, jnp.max(s_t, axis=0, keepdims=True))
    alpha = jnp.exp(m_prev - m_new)
    p_t = jnp.exp(s_t - m_new)
    l_ref[h] = alpha * l_ref[h] + jnp.sum(p_t, axis=0, keepdims=True)
    acc_ref[h] = alpha * acc_ref[h] + _dot(v_t, p_t.astype(jnp.bfloat16))
    m_ref[h] = m_new


def _softmax_init(m_ref, l_ref, acc_ref):
    m_ref[...] = jnp.full(m_ref.shape, NEG_BIG, jnp.float32)
    l_ref[...] = jnp.zeros(l_ref.shape, jnp.float32)
    acc_ref[...] = jnp.zeros(acc_ref.shape, jnp.float32)


def _softmax_finish(l_ref, acc_ref, o_ref):
    heads = acc_ref.shape[0]
    o_t = jnp.concatenate([acc_ref[h] / l_ref[h] for h in range(heads)], axis=0)
    o_ref[...] = o_t.T.astype(o_ref.dtype)


def _attention_pipeline(n_last, issue_qk, consume, s_refs):
    s0, s1 = s_refs
    issue_qk(0, 0, s0)

    def body(c, carry):
        issue_qk(c, 1, s1)
        consume(c, 0, s0, False)
        issue_qk(c + 1, 0, s0)
        consume(c, 1, s1, False)
        return carry

    lax.fori_loop(0, n_last, body, 0)
    issue_qk(n_last, 1, s1)
    consume(n_last, 0, s0, True)
    consume(n_last, 1, s1, True)


def _dsa_kernel(top_k, q_ref, qi_ref, wi_ref, ki_ref, k_ref, vt_ref, o_ref,
                sc_ref, qm_ref, qp_ref, m_ref, l_ref, acc_ref, s0_ref, s1_ref, bias_ref):
    i = pl.program_id(1)
    tq, kc = DSA_Q_TILE, DSA_KEY_CHUNK
    n_chunks = (i * tq) // kc + 1
    low = lax.broadcasted_iota(jnp.int32, (tq, LANES), 1) < IDX_DIM

    for j in range(IDX_HEADS // 2):
        t = qi_ref[:, j * LANES:(j + 1) * LANES].astype(jnp.float32)
        qm_ref[j, 0:tq, :] = jnp.where(low, t, 0.0).astype(jnp.bfloat16)
        qm_ref[j, tq:2 * tq, :] = jnp.where(low, 0.0, t).astype(jnp.bfloat16)
    for j in range(A_HEADS // 2):
        t = q_ref[:, j * LANES:(j + 1) * LANES].astype(jnp.float32)
        qp_ref[2 * j] = jnp.where(low, t, 0.0).astype(jnp.bfloat16)
        qp_ref[2 * j + 1] = jnp.where(low, 0.0, t).astype(jnp.bfloat16)
    w_t = wi_ref[...].T

    q_pos = i * tq + lax.broadcasted_iota(jnp.int32, (kc, tq), 1)
    k_off = lax.broadcasted_iota(jnp.int32, (kc, tq), 0)

    def score_chunk(c, carry):
        base = pl.multiple_of(c * kc, kc)
        kch = ki_ref[pl.ds(base, kc), :]
        acc = jnp.zeros((kc, tq), jnp.float32)
        for j in range(IDX_HEADS // 2):
            r = _dot_nt(kch, qm_ref[j])
            acc = acc + jnp.maximum(r[:, :tq], 0.0) * w_t[2 * j:2 * j + 1, :]
            acc = acc + jnp.maximum(r[:, tq:], 0.0) * w_t[2 * j + 1:2 * j + 2, :]
        sc_ref[pl.ds(base, kc), :] = jnp.where(base + k_off <= q_pos, acc, -jnp.inf)
        return carry

    lax.fori_loop(0, n_chunks, score_chunk, 0)

    part = DSA_COUNT_ROWS

    def bisect(step, lo):
        cand = lo + jnp.left_shift(jnp.int32(1), 31 - step)
        cand_f = _key_to_float(cand)

        def count_chunk(c, cnt):
            base = pl.multiple_of(c * kc, kc)
            hit = jnp.where(sc_ref[pl.ds(base, kc), :] >= cand_f, 1.0, 0.0)
            return cnt + jnp.sum(hit.reshape(kc // part, part, tq), axis=0)

        cnt = lax.fori_loop(0, n_chunks, count_chunk, jnp.zeros((part, tq), jnp.float32))
        total = jnp.sum(cnt, axis=0, keepdims=True)
        return jnp.where(total >= top_k, cand, lo)

    lo = lax.fori_loop(0, 32, bisect, jnp.full((1, tq), jnp.iinfo(jnp.int32).min, jnp.int32))
    thr = _key_to_float(jnp.maximum(lo, KEY_LOWEST_FINITE))

    _softmax_init(m_ref, l_ref, acc_ref)

    group = A_HEADS // 2

    def issue_qk(c, g, s_ref):
        base = pl.multiple_of(c * kc, kc)
        for u in range(group):
            h = g * group + u
            cols = slice((h // 2) * LANES, (h // 2 + 1) * LANES)
            s_ref[u] = _dot_nt(k_ref[pl.ds(base, kc), cols], qp_ref[h])

    def consume(c, g, s_ref, last):
        base = pl.multiple_of(c * kc, kc)
        if g == 0:
            bias_ref[...] = jnp.where(sc_ref[pl.ds(base, kc), :] >= thr, 0.0, NEG_BIG)
        for u in range(group):
            h = g * group + u
            v_t = vt_ref[h * A_HEAD_DIM:(h + 1) * A_HEAD_DIM, pl.ds(base, kc)]
            _softmax_step_t(s_ref[u] + bias_ref[...], v_t, m_ref, l_ref, acc_ref, h)

    _attention_pipeline(n_chunks - 1, issue_qk, consume, (s0_ref, s1_ref))
    _softmax_finish(l_ref, acc_ref, o_ref)


def _dsa_attention(q, qi, wi, ki, k, vt, batch, seq):
    n = q.shape[0]
    tq = DSA_Q_TILE
    nq = seq // tq
    top_k = min(TOPK_MAX, seq // 4)
    qrow = lambda width: pl.BlockSpec((tq, width), lambda b, i: (b * nq + i, 0))
    per_batch = lambda width: pl.BlockSpec((seq, width), lambda b, i: (b, 0))
    return pl.pallas_call(
        functools.partial(_dsa_kernel, top_k),
        grid=(batch, nq),
        in_specs=[qrow(A_WIDTH), qrow(IDX_HEADS * IDX_DIM), qrow(LANES),
                  per_batch(LANES), per_batch(A_WIDTH),
                  pl.BlockSpec((A_WIDTH, seq), lambda b, i: (0, b))],
        out_specs=qrow(A_WIDTH),
        out_shape=jax.ShapeDtypeStruct((n, A_WIDTH), jnp.bfloat16),
        scratch_shapes=[
            pltpu.VMEM((seq, tq), jnp.float32),
            pltpu.VMEM((IDX_HEADS // 2, 2 * tq, LANES), jnp.bfloat16),
            pltpu.VMEM((A_HEADS, tq, LANES), jnp.bfloat16),
            pltpu.VMEM((A_HEADS, 1, tq), jnp.float32),
            pltpu.VMEM((A_HEADS, 1, tq), jnp.float32),
            pltpu.VMEM((A_HEADS, A_HEAD_DIM, tq), jnp.float32),
            pltpu.VMEM((A_HEADS // 2, DSA_KEY_CHUNK, tq), jnp.float32),
            pltpu.VMEM((A_HEADS // 2, DSA_KEY_CHUNK, tq), jnp.float32),
            pltpu.VMEM((DSA_KEY_CHUNK, tq), jnp.float32),
        ],
        compiler_params=_params("arbitrary", "arbitrary"),
        name="dsa_attention",
    )(q, qi, wi, ki, k, vt)


def _out_proj_kernel(n_in, *refs):
    a_refs = refs[:n_in]
    w_refs = refs[n_in:2 * n_in]
    x_ref, g_ref, o_ref = refs[2 * n_in:]
    y = _dot(a_refs[0][...], w_refs[0][...])
    for a_ref, w_ref in zip(a_refs[1:], w_refs[1:]):
        y = y + _dot(a_ref[...], w_ref[...])
    o_ref[...] = x_ref[...] + _rms(y, g_ref[...])


def _out_proj(acts, weights, x, gain):
    n, d = x.shape
    tm = ROW_TILE
    row = lambda a: pl.BlockSpec((tm, a.shape[1]), lambda i: (i, 0))
    full = lambda a: pl.BlockSpec(a.shape, lambda i: (0,) * a.ndim)
    return pl.pallas_call(
        functools.partial(_out_proj_kernel, len(acts)),
        grid=(n // tm,),
        in_specs=[row(a) for a in acts] + [full(w) for w in weights] + [row(x), full(gain)],
        out_specs=row(x),
        out_shape=jax.ShapeDtypeStruct((n, d), jnp.float32),
        compiler_params=_params("arbitrary"),
        name="out_proj",
    )(*acts, *weights, x, gain)


def _mlp_kernel(x_ref, gpre_ref, gpost_ref, w1_ref, w2_ref, o_ref, hn_ref, acc_ref):
    f = pl.program_id(1)

    @pl.when(f == 0)
    def _():
        hn_ref[...] = _rms(x_ref[...], gpre_ref[...]).astype(jnp.bfloat16)
        acc_ref[...] = jnp.zeros(acc_ref.shape, jnp.float32)

    h = jnp.maximum(_dot(hn_ref[...], w1_ref[...]), 0.0)
    acc_ref[...] += _dot((h * h).astype(jnp.bfloat16), w2_ref[...])

    @pl.when(f == pl.num_programs(1) - 1)
    def _():
        o_ref[...] = x_ref[...] + _rms(acc_ref[...], gpost_ref[...])


def _mlp(x, gpre, gpost, w1, w2):
    n, d = x.shape
    ff = w1.shape[1]
    tm, tf = MLP_ROW_TILE, MLP_FF_TILE
    return pl.pallas_call(
        _mlp_kernel,
        grid=(n // tm, ff // tf),
        in_specs=[pl.BlockSpec((tm, d), lambda i, f: (i, 0)),
                  pl.BlockSpec((1, d), lambda i, f: (0, 0)),
                  pl.BlockSpec((1, d), lambda i, f: (0, 0)),
                  pl.BlockSpec((d, tf), lambda i, f: (0, f)),
                  pl.BlockSpec((tf, d), lambda i, f: (f, 0))],
        out_specs=pl.BlockSpec((tm, d), lambda i, f: (i, 0)),
        out_shape=jax.ShapeDtypeStruct((n, d), jnp.float32),
        scratch_shapes=[pltpu.VMEM((tm, d), jnp.bfloat16), pltpu.VMEM((tm, d), jnp.float32)],
        compiler_params=_params("arbitrary", "arbitrary"),
        name="mlp",
    )(x, gpre, gpost, w1, w2)


_MLA_HEAD_PAD = LANES
_DQ, _DKV, _DKR, _DEND = 0, C_Q_RANK, C_Q_RANK + C_KV_RANK, C_Q_RANK + C_KV_RANK + LANES


def _mla_in_kernel(x_ref, g_ref, wd_ref, qn_ref, kvn_ref, wuq_ref, wuk_ref, wuvt_ref,
                   c_ref, sn_ref, sp_ref, q_ref, k_ref, vt_ref):
    hn = _rms(x_ref[...], g_ref[...]).astype(jnp.bfloat16)
    c, sn, sp = c_ref[...], sn_ref[...], sp_ref[...]
    half = C_ROPE_DIM // 2
    scale = (C_NOPE_DIM + C_ROPE_DIM) ** -0.5

    qa = _rms(_dot(hn, wd_ref[:, _DQ:_DKV]), qn_ref[...]).astype(jnp.bfloat16)
    ckv = _rms(_dot(hn, wd_ref[:, _DKV:_DKR]), kvn_ref[...]).astype(jnp.bfloat16)
    kr = pltpu.roll(_dot(hn, wd_ref[:, _DKR:_DEND]), C_NOPE_DIM, axis=1)
    kr = _rope_tile(kr, c, sn, sp, half)
    for h in range(C_HEADS):
        cols = slice(h * LANES, (h + 1) * LANES)
        qh = _rope_tile(_dot(qa, wuq_ref[:, cols]), c, sn, sp, half) * scale
        q_ref[:, cols] = qh.astype(q_ref.dtype)
        k_ref[:, cols] = (_dot(ckv, wuk_ref[:, cols]) + kr).astype(k_ref.dtype)
    vt_ref[...] = _dot_nt(wuvt_ref[...], ckv).astype(vt_ref.dtype)


def _mla_in_proj(x, gain, w_down, q_norm, kv_norm, w_uq, w_uk, w_uvt, tabs):
    n, d = x.shape
    tm = ROW_TILE
    row = lambda width: pl.BlockSpec((tm, width), lambda i: (i, 0))
    full = lambda a: pl.BlockSpec(a.shape, lambda i: (0,) * a.ndim)
    bf = jnp.bfloat16
    return pl.pallas_call(
        _mla_in_kernel,
        grid=(n // tm,),
        in_specs=[row(d), full(gain), full(w_down), full(q_norm), full(kv_norm),
                  full(w_uq), full(w_uk), full(w_uvt), row(LANES), row(LANES), row(LANES)],
        out_specs=(row(C_HEADS * LANES), row(C_HEADS * LANES),
                   pl.BlockSpec((C_HEADS * C_V_DIM, tm), lambda i: (0, i))),
        out_shape=(jax.ShapeDtypeStruct((n, C_HEADS * LANES), bf),
                   jax.ShapeDtypeStruct((n, C_HEADS * LANES), bf),
                   jax.ShapeDtypeStruct((C_HEADS * C_V_DIM, n), bf)),
        compiler_params=_params("arbitrary"),
        name="mla_in_proj",
    )(x, gain, w_down, q_norm, kv_norm, w_uq, w_uk, w_uvt, *tabs)


def _mla_attn_kernel(q_ref, k_ref, vt_ref, o_ref, m_ref, l_ref, acc_ref, s0_ref, s1_ref):
    i = pl.program_id(2)
    tq, kc, group = MLA_Q_TILE, MLA_KEY_CHUNK, MLA_HEAD_GROUP // 2
    n_full = (i * tq) // kc
    _softmax_init(m_ref, l_ref, acc_ref)

    def issue_qk(c, g, s_ref):
        base = pl.multiple_of(c * kc, kc)
        for u in range(group):
            cols = slice((g * group + u) * LANES, (g * group + u + 1) * LANES)
            s_ref[u] = _dot_nt(k_ref[pl.ds(base, kc), cols], q_ref[:, cols])

    def consume(c, g, s_ref, last):
        base = pl.multiple_of(c * kc, kc)
        if last:
            q_pos = i * tq + lax.broadcasted_iota(jnp.int32, (kc, tq), 1)
            k_pos = base + lax.broadcasted_iota(jnp.int32, (kc, tq), 0)
            bias_t = jnp.where(k_pos <= q_pos, 0.0, NEG_BIG)
        for u in range(group):
            h = g * group + u
            s_t = s_ref[u] + bias_t if last else s_ref[u]
            v_t = vt_ref[h * C_V_DIM:(h + 1) * C_V_DIM, pl.ds(base, kc)]
            _softmax_step_t(s_t, v_t, m_ref, l_ref, acc_ref, h)

    _attention_pipeline(n_full, issue_qk, consume, (s0_ref, s1_ref))
    _softmax_finish(l_ref, acc_ref, o_ref)


def _mla_attention(q, k, vt, batch, seq):
    n = q.shape[0]
    tq, hg = MLA_Q_TILE, MLA_HEAD_GROUP
    nq = seq // tq
    return pl.pallas_call(
        _mla_attn_kernel,
        grid=(batch, C_HEADS // hg, nq),
        in_specs=[pl.BlockSpec((tq, hg * LANES), lambda b, g, i: (b * nq + i, g)),
                  pl.BlockSpec((seq, hg * LANES), lambda b, g, i: (b, g)),
                  pl.BlockSpec((hg * C_V_DIM, seq), lambda b, g, i: (g, b))],
        out_specs=pl.BlockSpec((tq, hg * C_V_DIM), lambda b, g, i: (b * nq + i, g)),
        out_shape=jax.ShapeDtypeStruct((n, C_HEADS * C_V_DIM), jnp.bfloat16),
        scratch_shapes=[pltpu.VMEM((hg, 1, tq), jnp.float32),
                        pltpu.VMEM((hg, 1, tq), jnp.float32),
                        pltpu.VMEM((hg, C_V_DIM, tq), jnp.float32),
                        pltpu.VMEM((hg // 2, MLA_KEY_CHUNK, tq), jnp.float32),
                        pltpu.VMEM((hg // 2, MLA_KEY_CHUNK, tq), jnp.float32)],
        compiler_params=_params("arbitrary", "arbitrary", "arbitrary"),
        name="mla_attention",
    )(q, k, vt)


def _rope_tables(positions, rot_dim, lead, period):
    inv_freq = ROPE_THETA ** (-jnp.arange(0, rot_dim, 2, dtype=jnp.float32) / rot_dim)
    ang = positions.astype(jnp.float32).reshape(-1)[:, None] * inv_freq
    cos, sin = jnp.cos(ang), jnp.sin(ang)
    n, half = cos.shape
    ones = lambda w: jnp.ones((n, w), jnp.float32)
    zeros = lambda w: jnp.zeros((n, w), jnp.float32)
    tail = period - lead - rot_dim
    c = jnp.concatenate([ones(lead), cos, cos, ones(tail)], axis=1)
    s_next = jnp.concatenate([zeros(lead), -sin, zeros(half + tail)], axis=1)
    s_prev = jnp.concatenate([zeros(lead + half), sin, zeros(tail)], axis=1)
    reps = LANES // period
    return tuple(jnp.tile(t, (1, reps)) for t in (c, s_next, s_prev))


def _pack_even_w_in(w):
    d = w.shape[0]
    q, k, v, qi, ki, wi, gb, gc, xi = jnp.split(
        w, [512, 1024, 1536, 2560, 2624, 2640, 3152, 3664], axis=1)
    pad = jnp.zeros((d, LANES - IDX_HEADS), w.dtype)
    packed = jnp.concatenate([q, k, qi, ki, ki, wi, pad, gb, gc, xi], axis=1)
    return packed.astype(jnp.bfloat16), v.T.astype(jnp.bfloat16)


def _pad_heads(w, heads, width):
    r = w.shape[0]
    w = w.reshape(r, heads, width)
    return jnp.pad(w, ((0, 0), (0, 0), (0, LANES - width))).reshape(r, heads * LANES)


def kernel(x, positions, norm_mix_pre, norm_mix_post, norm_ffn_pre, norm_ffn_post, even_w_in, even_conv_w, even_w_out, odd_w_dq, odd_q_norm, odd_w_uq, odd_w_dkv, odd_kv_norm, odd_w_ukv, odd_w_o, mlp_w1, mlp_w2):
    batch, seq, d = x.shape
    depth = norm_mix_pre.shape[0]
    bf = jnp.bfloat16
    tabs_a = _rope_tables(positions, A_ROT_DIM, 0, A_HEAD_DIM)
    tabs_c = _rope_tables(positions, C_ROPE_DIM, C_NOPE_DIM, LANES)
    h = x.reshape(batch * seq, d)
    row = lambda a: a.reshape(1, -1)
    for layer in range(depth):
        j = layer // 2
        g_pre, g_post = row(norm_mix_pre[layer]), row(norm_mix_post[layer])
        if layer % 2 == 0:
            w_packed, w_vt = _pack_even_w_in(even_w_in[j])
            q, k, vt, qi, ki, wi, conv = _even_in_proj(
                h, g_pre, w_packed, w_vt, tabs_a, even_conv_w[j], seq)
            attn = _dsa_attention(q, qi, wi, ki, k, vt, batch, seq)
            w_out = even_w_out[j].astype(bf)
            h = _out_proj([attn, conv], [w_out[:A_WIDTH], w_out[A_WIDTH:]], h, g_post)
        else:
            w_dkv = odd_w_dkv[j]
            w_down = jnp.concatenate(
                [odd_w_dq[j], w_dkv, jnp.zeros((d, LANES - C_ROPE_DIM), w_dkv.dtype)], axis=1)
            w_ukv = odd_w_ukv[j].reshape(C_KV_RANK, C_HEADS, C_NOPE_DIM + C_V_DIM)
            w_uk = _pad_heads(w_ukv[:, :, :C_NOPE_DIM].reshape(C_KV_RANK, -1), C_HEADS, C_NOPE_DIM)
            w_uv = w_ukv[:, :, C_NOPE_DIM:].reshape(C_KV_RANK, -1)
            w_uq = _pad_heads(odd_w_uq[j], C_HEADS, C_NOPE_DIM + C_ROPE_DIM)
            q, k, vt = _mla_in_proj(h, g_pre, w_down.astype(bf), row(odd_q_norm[j]),
                                    row(odd_kv_norm[j]), w_uq.astype(bf), w_uk.astype(bf),
                                    w_uv.T.astype(bf), tabs_c)
            attn = _mla_attention(q, k, vt, batch, seq)
            h = _out_proj([attn], [odd_w_o[j].astype(bf)], h, g_post)
        h = _mlp(h, row(norm_ffn_pre[layer]), row(norm_ffn_post[layer]),
                 mlp_w1[layer].astype(bf), mlp_w2[layer].astype(bf))
    return h.reshape(batch, seq, d)
```

```python
import functools

import jax
import jax.numpy as jnp
from jax import lax
from jax.experimental import pallas as pl
from jax.experimental.pallas import tpu as pltpu

ROPE_THETA = 500000.0
NORM_EPS = 1e-6
LANES = 128
NEG_BIG = -1e30
LOG2E = 1.4426950408889634
SUM_ROWS = 16

A_HEADS = 8
A_HEAD_DIM = 64
A_WIDTH = A_HEADS * A_HEAD_DIM
A_ROT_DIM = A_HEAD_DIM // 4
IDX_HEADS = 16
IDX_DIM = 64
TOPK_MAX = 256
B_WIDTH = 512
CONV_WIDTH = 3
C_HEADS = 16
C_NOPE_DIM = 64
C_ROPE_DIM = 32
C_V_DIM = 64
C_Q_RANK = 384
C_KV_RANK = 256

KEY_LOWEST_FINITE = -2139095040

ROW_TILE = 256
MLP_ROW_TILE = 512
MLP_FF_TILE = 1024
DSA_Q_TILE = 256
DSA_KEY_CHUNK = 512
DSA_COUNT_CHUNK = 256
DSA_COUNT_ROWS = 64
MLA_Q_TILE = 256
MLA_KEY_CHUNK = 512
MLA_HEAD_GROUP = 8
VMEM_LIMIT = 56 * 1024 * 1024


def _dot(a, b):
    return jnp.dot(a, b, preferred_element_type=jnp.float32)


def _dot_nt(a, b):
    return lax.dot_general(a, b, (((1,), (1,)), ((), ())), preferred_element_type=jnp.float32)


def _rms(x, gain):
    ms = jnp.mean(x * x, axis=-1, keepdims=True)
    return x * lax.rsqrt(ms + NORM_EPS) * gain


def _rope_tile(x, c, s_next, s_prev, half):
    return (x * c + pltpu.roll(x, LANES - half, axis=1) * s_next
            + pltpu.roll(x, half, axis=1) * s_prev)


def _params(*semantics):
    return pltpu.CompilerParams(dimension_semantics=semantics, vmem_limit_bytes=VMEM_LIMIT)


_EQ, _EK, _EQI, _EKI, _EWI, _EGB, _EGC, _EXI, _EEND = (
    0, 512, 1024, 2048, 2176, 2304, 2816, 3328, 3840)


def _even_in_kernel(seq_tiles, x_ref, g_ref, w_ref, wvt_ref, c_ref, sn_ref, sp_ref, cw_ref,
                    q_ref, k_ref, vt_ref, qi_ref, ki_ref, wi_ref, conv_ref, carry_ref):
    i = pl.program_id(0)
    tm = x_ref.shape[0]
    hn = _rms(x_ref[...], g_ref[...]).astype(jnp.bfloat16)
    c, sn, sp = c_ref[...], sn_ref[...], sp_ref[...]
    half = A_ROT_DIM // 2

    def rope_section(start, width, out_ref, scale=None):
        for t in range(width // LANES):
            y = _dot(hn, w_ref[:, start + t * LANES:start + (t + 1) * LANES])
            y = _rope_tile(y, c, sn, sp, half)
            if scale is not None:
                y = y * scale
            out_ref[:, t * LANES:(t + 1) * LANES] = y.astype(out_ref.dtype)

    rope_section(_EQ, A_WIDTH, q_ref, A_HEAD_DIM ** -0.5 * LOG2E)
    rope_section(_EK, A_WIDTH, k_ref)
    vt_ref[...] = _dot_nt(wvt_ref[...], hn).astype(vt_ref.dtype)
    rope_section(_EQI, IDX_HEADS * IDX_DIM, qi_ref)
    rope_section(_EKI, LANES, ki_ref)
    wi_ref[...] = _dot(hn, w_ref[:, _EWI:_EGB]) * (IDX_HEADS ** -0.5 * IDX_DIM ** -0.5)

    gate_b = _dot(hn, w_ref[:, _EGB:_EGC])
    u = _dot(hn, w_ref[:, _EGC:_EXI]) * _dot(hn, w_ref[:, _EXI:_EEND])
    seq_start = (i % seq_tiles) == 0
    prev = jnp.where(seq_start, 0.0, carry_ref[...])
    carry_ref[...] = u[tm - 8:, :]
    row = lax.broadcasted_iota(jnp.int32, u.shape, 0)
    u1 = jnp.where(row == 0, prev[7:8, :], pltpu.roll(u, 1, axis=0))
    u2 = pltpu.roll(u, 2, axis=0)
    u2 = jnp.where(row == 0, prev[6:7, :], jnp.where(row == 1, prev[7:8, :], u2))
    y = cw_ref[0:1, :] * u2 + cw_ref[1:2, :] * u1 + cw_ref[2:3, :] * u
    conv_ref[...] = (gate_b * y).astype(conv_ref.dtype)


def _even_in_proj(x, gain, w_packed, w_vt, tabs, conv_w, seq):
    n, d = x.shape
    tm = ROW_TILE
    row = lambda width: pl.BlockSpec((tm, width), lambda i: (i, 0))
    full = lambda a: pl.BlockSpec(a.shape, lambda i: (0,) * a.ndim)
    bf = jnp.bfloat16
    out_shape = (
        jax.ShapeDtypeStruct((n, A_WIDTH), bf), jax.ShapeDtypeStruct((n, A_WIDTH), bf),
        jax.ShapeDtypeStruct((A_WIDTH, n), bf), jax.ShapeDtypeStruct((n, IDX_HEADS * IDX_DIM), bf),
        jax.ShapeDtypeStruct((n, LANES), bf), jax.ShapeDtypeStruct((n, LANES), jnp.float32),
        jax.ShapeDtypeStruct((n, B_WIDTH), bf))
    return pl.pallas_call(
        functools.partial(_even_in_kernel, seq // tm),
        grid=(n // tm,),
        in_specs=[row(d), full(gain), full(w_packed), full(w_vt), row(LANES), row(LANES),
                  row(LANES), full(conv_w)],
        out_specs=(row(A_WIDTH), row(A_WIDTH), pl.BlockSpec((A_WIDTH, tm), lambda i: (0, i)),
                   row(IDX_HEADS * IDX_DIM), row(LANES), row(LANES), row(B_WIDTH)),
        out_shape=out_shape,
        scratch_shapes=[pltpu.VMEM((8, B_WIDTH), jnp.float32)],
        compiler_params=_params("arbitrary"),
        name="even_in_proj",
    )(x, gain, w_packed, w_vt, *tabs, conv_w)


def _key_to_float(key):
    bits = key ^ ((key >> 31) & jnp.int32(0x7FFFFFFF))
    return lax.bitcast_convert_type(bits, jnp.float32)


def _softmax_step_t(s_t, v_t, m_ref, acc_ref, h):
    kc = s_t.shape[0]
    m_prev = m_ref[h]
    m_new = jnp.maximum(m_prev, jnp.max(s_t, axis=0, keepdims=True))
    alpha = jnp.exp2(m_prev - m_new)
    p_t = jnp.exp2(s_t - m_new).astype(jnp.bfloat16)
    v_aug = jnp.concatenate([v_t, jnp.ones((SUM_ROWS, kc), jnp.bfloat16)], axis=0)
    acc_ref[h] = alpha * acc_ref[h] + _dot(v_aug, p_t)
    m_ref[h] = m_new


def _softmax_init(m_ref, acc_ref):
    m_ref[...] = jnp.full(m_ref.shape, NEG_BIG, jnp.float32)
    acc_ref[...] = jnp.zeros(acc_ref.shape, jnp.float32)


def _softmax_finish(acc_ref, o_ref):
    heads, rows, _ = acc_ref.shape
    dv = rows - SUM_ROWS
    o_t = jnp.concatenate(
        [acc_ref[h, 0:dv, :] / acc_ref[h, dv:dv + 1, :] for h in range(heads)], axis=0)
    o_ref[...] = o_t.T.astype(o_ref.dtype)


def _attention_pipeline(n_last, half_last, kc, issue_qk, consume, s_refs):
    s0, s1 = s_refs
    issue_qk(0, 0, s0, kc)

    def body(c, carry):
        issue_qk(c, 1, s1, kc)
        consume(c, 0, s0, False, kc)
        issue_qk(c + 1, 0, s0, kc)
        consume(c, 1, s1, False, kc)
        return carry

    lax.fori_loop(0, n_last, body, 0)

    def last_chunk(width):
        issue_qk(n_last, 1, s1, width)
        consume(n_last, 0, s0, True, width)
        consume(n_last, 1, s1, True, width)

    pl.when(half_last)(lambda: last_chunk(kc // 2))
    pl.when(jnp.logical_not(half_last))(lambda: last_chunk(kc))


def _dsa_kernel(top_k, q_ref, qi_ref, wi_ref, ki_ref, k_ref, vt_ref, o_ref,
                sc_ref, qm_ref, qp_ref, m_ref, acc_ref, s0_ref, s1_ref, bias_ref):
    i = pl.program_id(1)
    tq, kc, ks = DSA_Q_TILE, DSA_KEY_CHUNK, DSA_COUNT_CHUNK
    n_chunks = (i * tq) // kc + 1
    n_count = (i * tq) // ks + 1
    low = lax.broadcasted_iota(jnp.int32, (tq, LANES), 1) < IDX_DIM

    for j in range(IDX_HEADS // 2):
        t = qi_ref[:, j * LANES:(j + 1) * LANES].astype(jnp.float32)
        qm_ref[j, 0:tq, :] = jnp.where(low, t, 0.0).astype(jnp.bfloat16)
        qm_ref[j, tq:2 * tq, :] = jnp.where(low, 0.0, t).astype(jnp.bfloat16)
    for j in range(A_HEADS // 2):
        t = q_ref[:, j * LANES:(j + 1) * LANES].astype(jnp.float32)
        qp_ref[2 * j] = jnp.where(low, t, 0.0).astype(jnp.bfloat16)
        qp_ref[2 * j + 1] = jnp.where(low, 0.0, t).astype(jnp.bfloat16)
    w_t = wi_ref[...].T

    q_pos = i * tq + lax.broadcasted_iota(jnp.int32, (kc, tq), 1)
    k_off = lax.broadcasted_iota(jnp.int32, (kc, tq), 0)

    def score_chunk(c, carry):
        base = pl.multiple_of(c * kc, kc)
        kch = ki_ref[pl.ds(base, kc), :]
        acc = jnp.zeros((kc, tq), jnp.float32)
        for j in range(IDX_HEADS // 2):
            r = _dot_nt(kch, qm_ref[j])
            acc = acc + jnp.maximum(r[:, :tq], 0.0) * w_t[2 * j:2 * j + 1, :]
            acc = acc + jnp.maximum(r[:, tq:], 0.0) * w_t[2 * j + 1:2 * j + 2, :]
        sc_ref[pl.ds(base, kc), :] = jnp.where(base + k_off <= q_pos, acc, -jnp.inf)
        return carry

    lax.fori_loop(0, n_chunks, score_chunk, 0)

    part = DSA_COUNT_ROWS

    def count(pred):
        def count_chunk(c, cnt):
            base = pl.multiple_of(c * ks, ks)
            for r in range(ks // part):
                row0 = base + r * part
                cnt = jnp.where(pred(sc_ref[pl.ds(row0, part), :], row0), cnt + 1.0, cnt)
            return cnt

        cnt = lax.fori_loop(0, n_count, count_chunk, jnp.zeros((part, tq), jnp.float32))
        return jnp.sum(cnt, axis=0, keepdims=True)

    def bisect(step, carry):
        lo, n_lo = carry
        cand = lo + jnp.left_shift(jnp.int32(1), 31 - step)
        cand_f = _key_to_float(cand)
        total = count(lambda blk, row0: blk >= cand_f)
        take = total >= top_k
        return jnp.where(take, cand, lo), jnp.where(take, total, n_lo)

    lo, n_ge = lax.fori_loop(
        0, 32, bisect, (jnp.full((1, tq), jnp.iinfo(jnp.int32).min, jnp.int32),
                        jnp.zeros((1, tq), jnp.float32)))
    thr = _key_to_float(jnp.maximum(lo, KEY_LOWEST_FINITE))

    @pl.when(jnp.max(n_ge) > top_k)
    def _():
        n_gt = count(lambda blk, row0: blk > thr)
        need = top_k - n_gt
        row = lax.broadcasted_iota(jnp.int32, (part, tq), 0)
        index_bits = (sc_ref.shape[0] - 1).bit_length()

        def index_search(step, cut):
            cand = cut + jnp.left_shift(jnp.int32(1), index_bits - 1 - step)
            below = count(lambda blk, row0: (blk == thr) & (row0 + row < cand))
            return jnp.where(below < need, cand, cut)

        cut = lax.fori_loop(0, index_bits, index_search, jnp.zeros((1, tq), jnp.int32))

        def drop_chunk(c, carry):
            base = pl.multiple_of(c * ks, ks)
            blk = sc_ref[pl.ds(base, ks), :]
            pos = base + lax.broadcasted_iota(jnp.int32, (ks, tq), 0)
            sc_ref[pl.ds(base, ks), :] = jnp.where((blk == thr) & (pos > cut), -jnp.inf, blk)
            return carry

        lax.fori_loop(0, n_count, drop_chunk, 0)

    _softmax_init(m_ref, acc_ref)

    group = A_HEADS // 2

    def issue_qk(c, g, s_ref, width):
        base = pl.multiple_of(c * kc, kc)
        for u in range(group):
            h = g * group + u
            cols = slice((h // 2) * LANES, (h // 2 + 1) * LANES)
            s_ref[u, 0:width, :] = _dot_nt(k_ref[pl.ds(base, width), cols], qp_ref[h])

    def consume(c, g, s_ref, last, width):
        base = pl.multiple_of(c * kc, kc)
        if g == 0:
            bias_ref[0:width, :] = jnp.where(
                sc_ref[pl.ds(base, width), :] >= thr, 0.0, NEG_BIG)
        for u in range(group):
            h = g * group + u
            v_t = vt_ref[h * A_HEAD_DIM:(h + 1) * A_HEAD_DIM, pl.ds(base, width)]
            s_t = s_ref[u, 0:width, :] + bias_ref[0:width, :]
            _softmax_step_t(s_t, v_t, m_ref, acc_ref, h)

    half_last = (i + 1) * tq - (n_chunks - 1) * kc <= kc // 2
    _attention_pipeline(n_chunks - 1, half_last, kc, issue_qk, consume, (s0_ref, s1_ref))
    _softmax_finish(acc_ref, o_ref)


def _dsa_attention(q, qi, wi, ki, k, vt, batch, seq):
    n = q.shape[0]
    tq = DSA_Q_TILE
    nq = seq // tq
    top_k = min(TOPK_MAX, seq // 4)
    qrow = lambda width: pl.BlockSpec((tq, width), lambda b, i: (b * nq + i, 0))
    per_batch = lambda width: pl.BlockSpec((seq, width), lambda b, i: (b, 0))
    return pl.pallas_call(
        functools.partial(_dsa_kernel, top_k),
        grid=(batch, nq),
        in_specs=[qrow(A_WIDTH), qrow(IDX_HEADS * IDX_DIM), qrow(LANES),
                  per_batch(LANES), per_batch(A_WIDTH),
                  pl.BlockSpec((A_WIDTH, seq), lambda b, i: (0, b))],
        out_specs=qrow(A_WIDTH),
        out_shape=jax.ShapeDtypeStruct((n, A_WIDTH), jnp.bfloat16),
        scratch_shapes=[
            pltpu.VMEM((seq, tq), jnp.float32),
            pltpu.VMEM((IDX_HEADS // 2, 2 * tq, LANES), jnp.bfloat16),
            pltpu.VMEM((A_HEADS, tq, LANES), jnp.bfloat16),
            pltpu.VMEM((A_HEADS, 1, tq), jnp.float32),
            pltpu.VMEM((A_HEADS, A_HEAD_DIM + SUM_ROWS, tq), jnp.float32),
            pltpu.VMEM((A_HEADS // 2, DSA_KEY_CHUNK, tq), jnp.float32),
            pltpu.VMEM((A_HEADS // 2, DSA_KEY_CHUNK, tq), jnp.float32),
            pltpu.VMEM((DSA_KEY_CHUNK, tq), jnp.float32),
        ],
        compiler_params=_params("arbitrary", "arbitrary"),
        name="dsa_attention",
    )(q, qi, wi, ki, k, vt)


def _out_proj_kernel(n_in, *refs):
    a_refs = refs[:n_in]
    w_refs = refs[n_in:2 * n_in]
    x_ref, g_ref, o_ref = refs[2 * n_in:]
    y = _dot(a_refs[0][...], w_refs[0][...])
    for a_ref, w_ref in zip(a_refs[1:], w_refs[1:]):
        y = y + _dot(a_ref[...], w_ref[...])
    o_ref[...] = x_ref[...] + _rms(y, g_ref[...])


def _out_proj(acts, weights, x, gain):
    n, d = x.shape
    tm = ROW_TILE
    row = lambda a: pl.BlockSpec((tm, a.shape[1]), lambda i: (i, 0))
    full = lambda a: pl.BlockSpec(a.shape, lambda i: (0,) * a.ndim)
    return pl.pallas_call(
        functools.partial(_out_proj_kernel, len(acts)),
        grid=(n // tm,),
        in_specs=[row(a) for a in acts] + [full(w) for w in weights] + [row(x), full(gain)],
        out_specs=row(x),
        out_shape=jax.ShapeDtypeStruct((n, d), jnp.float32),
        compiler_params=_params("arbitrary"),
        name="out_proj",
    )(*acts, *weights, x, gain)


def _mlp_kernel(x_ref, gpre_ref, gpost_ref, w1_ref, w2_ref, o_ref, hn_ref, acc_ref):
    f = pl.program_id(1)

    @pl.when(f == 0)
    def _():
        hn_ref[...] = _rms(x_ref[...], gpre_ref[...]).astype(jnp.bfloat16)
        acc_ref[...] = jnp.zeros(acc_ref.shape, jnp.float32)

    h = jnp.maximum(_dot(hn_ref[...], w1_ref[...]), 0.0)
    acc_ref[...] += _dot((h * h).astype(jnp.bfloat16), w2_ref[...])

    @pl.when(f == pl.num_programs(1) - 1)
    def _():
        o_ref[...] = x_ref[...] + _rms(acc_ref[...], gpost_ref[...])


def _mlp(x, gpre, gpost, w1, w2):
    n, d = x.shape
    ff = w1.shape[1]
    tm, tf = MLP_ROW_TILE, MLP_FF_TILE
    return pl.pallas_call(
        _mlp_kernel,
        grid=(n // tm, ff // tf),
        in_specs=[pl.BlockSpec((tm, d), lambda i, f: (i, 0)),
                  pl.BlockSpec((1, d), lambda i, f: (0, 0)),
                  pl.BlockSpec((1, d), lambda i, f: (0, 0)),
                  pl.BlockSpec((d, tf), lambda i, f: (0, f)),
                  pl.BlockSpec((tf, d), lambda i, f: (f, 0))],
        out_specs=pl.BlockSpec((tm, d), lambda i, f: (i, 0)),
        out_shape=jax.ShapeDtypeStruct((n, d), jnp.float32),
        scratch_shapes=[pltpu.VMEM((tm, d), jnp.bfloat16), pltpu.VMEM((tm, d), jnp.float32)],
        compiler_params=_params("arbitrary", "arbitrary"),
        name="mlp",
    )(x, gpre, gpost, w1, w2)


_MLA_HEAD_PAD = LANES
_DQ, _DKV, _DKR, _DEND = 0, C_Q_RANK, C_Q_RANK + C_KV_RANK, C_Q_RANK + C_KV_RANK + LANES


def _mla_in_kernel(x_ref, g_ref, wd_ref, qn_ref, kvn_ref, wuq_ref, wuk_ref, wuvt_ref,
                   c_ref, sn_ref, sp_ref, q_ref, k_ref, vt_ref):
    hn = _rms(x_ref[...], g_ref[...]).astype(jnp.bfloat16)
    c, sn, sp = c_ref[...], sn_ref[...], sp_ref[...]
    half = C_ROPE_DIM // 2
    scale = (C_NOPE_DIM + C_ROPE_DIM) ** -0.5 * LOG2E

    qa = _rms(_dot(hn, wd_ref[:, _DQ:_DKV]), qn_ref[...]).astype(jnp.bfloat16)
    ckv = _rms(_dot(hn, wd_ref[:, _DKV:_DKR]), kvn_ref[...]).astype(jnp.bfloat16)
    kr = pltpu.roll(_dot(hn, wd_ref[:, _DKR:_DEND]), C_NOPE_DIM, axis=1)
    kr = _rope_tile(kr, c, sn, sp, half)
    for h in range(C_HEADS):
        cols = slice(h * LANES, (h + 1) * LANES)
        qh = _rope_tile(_dot(qa, wuq_ref[:, cols]), c, sn, sp, half) * scale
        q_ref[:, cols] = qh.astype(q_ref.dtype)
        k_ref[:, cols] = (_dot(ckv, wuk_ref[:, cols]) + kr).astype(k_ref.dtype)
    vt_ref[...] = _dot_nt(wuvt_ref[...], ckv).astype(vt_ref.dtype)


def _mla_in_proj(x, gain, w_down, q_norm, kv_norm, w_uq, w_uk, w_uvt, tabs):
    n, d = x.shape
    tm = ROW_TILE
    row = lambda width: pl.BlockSpec((tm, width), lambda i: (i, 0))
    full = lambda a: pl.BlockSpec(a.shape, lambda i: (0,) * a.ndim)
    bf = jnp.bfloat16
    return pl.pallas_call(
        _mla_in_kernel,
        grid=(n // tm,),
        in_specs=[row(d), full(gain), full(w_down), full(q_norm), full(kv_norm),
                  full(w_uq), full(w_uk), full(w_uvt), row(LANES), row(LANES), row(LANES)],
        out_specs=(row(C_HEADS * LANES), row(C_HEADS * LANES),
                   pl.BlockSpec((C_HEADS * C_V_DIM, tm), lambda i: (0, i))),
        out_shape=(jax.ShapeDtypeStruct((n, C_HEADS * LANES), bf),
                   jax.ShapeDtypeStruct((n, C_HEADS * LANES), bf),
                   jax.ShapeDtypeStruct((C_HEADS * C_V_DIM, n), bf)),
        compiler_params=_params("arbitrary"),
        name="mla_in_proj",
    )(x, gain, w_down, q_norm, kv_norm, w_uq, w_uk, w_uvt, *tabs)


def _mla_attn_kernel(q_ref, k_ref, vt_ref, o_ref, m_ref, acc_ref, s0_ref, s1_ref):
    i = pl.program_id(2)
    tq, kc, group = MLA_Q_TILE, MLA_KEY_CHUNK, MLA_HEAD_GROUP // 2
    n_full = (i * tq) // kc
    _softmax_init(m_ref, acc_ref)

    def issue_qk(c, g, s_ref, width):
        base = pl.multiple_of(c * kc, kc)
        for u in range(group):
            cols = slice((g * group + u) * LANES, (g * group + u + 1) * LANES)
            s_ref[u, 0:width, :] = _dot_nt(k_ref[pl.ds(base, width), cols], q_ref[:, cols])

    def consume(c, g, s_ref, last, width):
        base = pl.multiple_of(c * kc, kc)
        if last:
            q_pos = i * tq + lax.broadcasted_iota(jnp.int32, (width, tq), 1)
            k_pos = base + lax.broadcasted_iota(jnp.int32, (width, tq), 0)
            bias_t = jnp.where(k_pos <= q_pos, 0.0, NEG_BIG)
        for u in range(group):
            h = g * group + u
            s_t = s_ref[u, 0:width, :]
            if last:
                s_t = s_t + bias_t
            v_t = vt_ref[h * C_V_DIM:(h + 1) * C_V_DIM, pl.ds(base, width)]
            _softmax_step_t(s_t, v_t, m_ref, acc_ref, h)

    half_last = (i + 1) * tq - n_full * kc <= kc // 2
    _attention_pipeline(n_full, half_last, kc, issue_qk, consume, (s0_ref, s1_ref))
    _softmax_finish(acc_ref, o_ref)


def _mla_attention(q, k, vt, batch, seq):
    n = q.shape[0]
    tq, hg = MLA_Q_TILE, MLA_HEAD_GROUP
    nq = seq // tq
    return pl.pallas_call(
        _mla_attn_kernel,
        grid=(batch, C_HEADS // hg, nq),
        in_specs=[pl.BlockSpec((tq, hg * LANES), lambda b, g, i: (b * nq + i, g)),
                  pl.BlockSpec((seq, hg * LANES), lambda b, g, i: (b, g)),
                  pl.BlockSpec((hg * C_V_DIM, seq), lambda b, g, i: (g, b))],
        out_specs=pl.BlockSpec((tq, hg * C_V_DIM), lambda b, g, i: (b * nq + i, g)),
        out_shape=jax.ShapeDtypeStruct((n, C_HEADS * C_V_DIM), jnp.bfloat16),
        scratch_shapes=[pltpu.VMEM((hg, 1, tq), jnp.float32),
                        pltpu.VMEM((hg, C_V_DIM + SUM_ROWS, tq), jnp.float32),
                        pltpu.VMEM((hg // 2, MLA_KEY_CHUNK, tq), jnp.float32),
                        pltpu.VMEM((hg // 2, MLA_KEY_CHUNK, tq), jnp.float32)],
        compiler_params=_params("arbitrary", "arbitrary", "arbitrary"),
        name="mla_attention",
    )(q, k, vt)


def _rope_tables(positions, rot_dim, lead, period):
    inv_freq = ROPE_THETA ** (-jnp.arange(0, rot_dim, 2, dtype=jnp.float32) / rot_dim)
    ang = positions.astype(jnp.float32).reshape(-1)[:, None] * inv_freq
    cos, sin = jnp.cos(ang), jnp.sin(ang)
    n, half = cos.shape
    ones = lambda w: jnp.ones((n, w), jnp.float32)
    zeros = lambda w: jnp.zeros((n, w), jnp.float32)
    tail = period - lead - rot_dim
    c = jnp.concatenate([ones(lead), cos, cos, ones(tail)], axis=1)
    s_next = jnp.concatenate([zeros(lead), -sin, zeros(half + tail)], axis=1)
    s_prev = jnp.concatenate([zeros(lead + half), sin, zeros(tail)], axis=1)
    reps = LANES // period
    return tuple(jnp.tile(t, (1, reps)) for t in (c, s_next, s_prev))


def _pack_even_w_in(w):
    d = w.shape[0]
    q, k, v, qi, ki, wi, gb, gc, xi = jnp.split(
        w, [512, 1024, 1536, 2560, 2624, 2640, 3152, 3664], axis=1)
    pad = jnp.zeros((d, LANES - IDX_HEADS), w.dtype)
    packed = jnp.concatenate([q, k, qi, ki, ki, wi, pad, gb, gc, xi], axis=1)
    return packed.astype(jnp.bfloat16), v.T.astype(jnp.bfloat16)


def _pad_heads(w, heads, width):
    r = w.shape[0]
    w = w.reshape(r, heads, width)
    return jnp.pad(w, ((0, 0), (0, 0), (0, LANES - width))).reshape(r, heads * LANES)


def kernel(x, positions, norm_mix_pre, norm_mix_post, norm_ffn_pre, norm_ffn_post, even_w_in, even_conv_w, even_w_out, odd_w_dq, odd_q_norm, odd_w_uq, odd_w_dkv, odd_kv_norm, odd_w_ukv, odd_w_o, mlp_w1, mlp_w2):
    batch, seq, d = x.shape
    depth = norm_mix_pre.shape[0]
    bf = jnp.bfloat16
    tabs_a = _rope_tables(positions, A_ROT_DIM, 0, A_HEAD_DIM)
    tabs_c = _rope_tables(positions, C_ROPE_DIM, C_NOPE_DIM, LANES)
    h = x.reshape(batch * seq, d)
    row = lambda a: a.reshape(1, -1)
    for layer in range(depth):
        j = layer // 2
        g_pre, g_post = row(norm_mix_pre[layer]), row(norm_mix_post[layer])
        if layer % 2 == 0:
            w_packed, w_vt = _pack_even_w_in(even_w_in[j])
            q, k, vt, qi, ki, wi, conv = _even_in_proj(
                h, g_pre, w_packed, w_vt, tabs_a, even_conv_w[j], seq)
            attn = _dsa_attention(q, qi, wi, ki, k, vt, batch, seq)
            w_out = even_w_out[j].astype(bf)
            h = _out_proj([attn, conv], [w_out[:A_WIDTH], w_out[A_WIDTH:]], h, g_post)
        else:
            w_dkv = odd_w_dkv[j]
            w_down = jnp.concatenate(
                [odd_w_dq[j], w_dkv, jnp.zeros((d, LANES - C_ROPE_DIM), w_dkv.dtype)], axis=1)
            w_ukv = odd_w_ukv[j].reshape(C_KV_RANK, C_HEADS, C_NOPE_DIM + C_V_DIM)
            w_uk = _pad_heads(w_ukv[:, :, :C_NOPE_DIM].reshape(C_KV_RANK, -1), C_HEADS, C_NOPE_DIM)
            w_uv = w_ukv[:, :, C_NOPE_DIM:].reshape(C_KV_RANK, -1)
            w_uq = _pad_heads(odd_w_uq[j], C_HEADS, C_NOPE_DIM + C_ROPE_DIM)
            q, k, vt = _mla_in_proj(h, g_pre, w_down.astype(bf), row(odd_q_norm[j]),
                                    row(odd_kv_norm[j]), w_uq.astype(bf), w_uk.astype(bf),
                                    w_uv.T.astype(bf), tabs_c)
            attn = _mla_attention(q, k, vt, batch, seq)
            h = _out_proj([attn], [odd_w_o[j].astype(bf)], h, g_post)
        h = _mlp(h, row(norm_ffn_pre[layer]), row(norm_ffn_post[layer]),
                 mlp_w1[layer].astype(bf), mlp_w2[layer].astype(bf))
    return h.reshape(batch, seq, d)
```

```python
import functools

import jax
import jax.numpy as jnp
from jax import lax
from jax.experimental import pallas as pl
from jax.experimental.pallas import tpu as pltpu

ROPE_THETA = 500000.0
NORM_EPS = 1e-6
LANES = 128
MXU_COLS = 256
NEG_BIG = -1e30
LOG2E = 1.4426950408889634
SUM_ROWS = 16

A_HEADS = 8
A_HEAD_DIM = 64
A_WIDTH = A_HEADS * A_HEAD_DIM
A_ROT_DIM = A_HEAD_DIM // 4
IDX_HEADS = 16
IDX_DIM = 64
TOPK_MAX = 256
B_WIDTH = 512
CONV_WIDTH = 3
C_HEADS = 16
C_NOPE_DIM = 64
C_ROPE_DIM = 32
C_V_DIM = 64
C_Q_RANK = 384
C_KV_RANK = 256

KEY_LOWEST_FINITE = -2139095040

ROW_TILE = 512
MLA_ROW_TILE = 512
MLP_ROW_TILE = 512
MLP_FF_TILE = 1024
DSA_Q_TILE = 256
DSA_KEY_CHUNK = 512
DSA_COUNT_CHUNK = 256
DSA_COUNT_ROWS = 64
MLA_Q_TILE = 256
MLA_KEY_CHUNK = 512
MLA_HEAD_GROUP = 8
VMEM_LIMIT = 56 * 1024 * 1024


def _dot(a, b):
    return jnp.dot(a, b, preferred_element_type=jnp.float32)


def _dot_nt(a, b):
    return lax.dot_general(a, b, (((1,), (1,)), ((), ())), preferred_element_type=jnp.float32)


def _rms(x, gain):
    ms = jnp.mean(x * x, axis=-1, keepdims=True)
    return x * lax.rsqrt(ms + NORM_EPS) * gain


def _rope_tile(x, c, s_next, s_prev, half):
    return (x * c + pltpu.roll(x, LANES - half, axis=1) * s_next
            + pltpu.roll(x, half, axis=1) * s_prev)


def _params(*semantics):
    return pltpu.CompilerParams(dimension_semantics=semantics, vmem_limit_bytes=VMEM_LIMIT)


_EQ, _EK, _EQI, _EKI, _EWI, _EGB, _EGC, _EXI, _EEND = (
    0, 512, 1024, 2048, 2176, 2304, 2816, 3328, 3840)


def _even_in_kernel(seq_tiles, x_ref, g_ref, w_ref, wvt_ref, c_ref, sn_ref, sp_ref, cw_ref,
                    q_ref, k_ref, vt_ref, qi_ref, ki_ref, wi_ref, conv_ref, carry_ref):
    i = pl.program_id(0)
    tm = x_ref.shape[0]
    hn = _rms(x_ref[...], g_ref[...]).astype(jnp.bfloat16)
    c, sn, sp = c_ref[...].T, sn_ref[...].T, sp_ref[...].T
    half = A_ROT_DIM // 2

    def rope_section(start, width, out_ref, scale=None):
        for t in range(width // MXU_COLS):
            y2 = _dot(hn, w_ref[:, start + t * MXU_COLS:start + (t + 1) * MXU_COLS])
            for u in range(MXU_COLS // LANES):
                y = _rope_tile(y2[:, u * LANES:(u + 1) * LANES], c, sn, sp, half)
                if scale is not None:
                    y = y * scale
                lo = t * MXU_COLS + u * LANES
                out_ref[:, lo:lo + LANES] = y.astype(out_ref.dtype)

    rope_section(_EQ, A_WIDTH, q_ref, A_HEAD_DIM ** -0.5 * LOG2E)
    rope_section(_EK, A_WIDTH, k_ref)
    vt_ref[...] = _dot_nt(wvt_ref[...], hn).astype(vt_ref.dtype)
    rope_section(_EQI, IDX_HEADS * IDX_DIM, qi_ref)
    kw = _dot(hn, w_ref[:, _EKI:_EGB])
    ki_ref[...] = _rope_tile(kw[:, :LANES], c, sn, sp, half).astype(ki_ref.dtype)
    wi_ref[...] = kw[:, LANES:] * (IDX_HEADS ** -0.5 * IDX_DIM ** -0.5)

    gate_b = _dot(hn, w_ref[:, _EGB:_EGC])
    u = _dot(hn, w_ref[:, _EGC:_EXI]) * _dot(hn, w_ref[:, _EXI:_EEND])
    seq_start = (i % seq_tiles) == 0
    prev = jnp.where(seq_start, 0.0, carry_ref[...])
    carry_ref[...] = u[tm - 8:, :]
    row = lax.broadcasted_iota(jnp.int32, u.shape, 0)
    u1 = jnp.where(row == 0, prev[7:8, :], pltpu.roll(u, 1, axis=0))
    u2 = pltpu.roll(u, 2, axis=0)
    u2 = jnp.where(row == 0, prev[6:7, :], jnp.where(row == 1, prev[7:8, :], u2))
    y = cw_ref[0:1, :] * u2 + cw_ref[1:2, :] * u1 + cw_ref[2:3, :] * u
    conv_ref[...] = (gate_b * y).astype(conv_ref.dtype)


def _even_in_proj(x, gain, w_packed, w_vt, tabs, conv_w, seq):
    n, d = x.shape
    tm = ROW_TILE
    row = lambda width: pl.BlockSpec((tm, width), lambda i: (i, 0))
    full = lambda a: pl.BlockSpec(a.shape, lambda i: (0,) * a.ndim)
    tab = pl.BlockSpec((LANES, tm), lambda i: (0, i))
    bf = jnp.bfloat16
    out_shape = (
        jax.ShapeDtypeStruct((n, A_WIDTH), bf), jax.ShapeDtypeStruct((n, A_WIDTH), bf),
        jax.ShapeDtypeStruct((A_WIDTH, n), bf), jax.ShapeDtypeStruct((n, IDX_HEADS * IDX_DIM), bf),
        jax.ShapeDtypeStruct((n, LANES), bf), jax.ShapeDtypeStruct((n, LANES), jnp.float32),
        jax.ShapeDtypeStruct((n, B_WIDTH), bf))
    return pl.pallas_call(
        functools.partial(_even_in_kernel, seq // tm),
        grid=(n // tm,),
        in_specs=[row(d), full(gain), full(w_packed), full(w_vt), tab, tab, tab,
                  full(conv_w)],
        out_specs=(row(A_WIDTH), row(A_WIDTH), pl.BlockSpec((A_WIDTH, tm), lambda i: (0, i)),
                   row(IDX_HEADS * IDX_DIM), row(LANES), row(LANES), row(B_WIDTH)),
        out_shape=out_shape,
        scratch_shapes=[pltpu.VMEM((8, B_WIDTH), jnp.float32)],
        compiler_params=_params("arbitrary"),
        name="even_in_proj",
    )(x, gain, w_packed, w_vt, *tabs, conv_w)


def _key_to_float(key):
    bits = key ^ ((key >> 31) & jnp.int32(0x7FFFFFFF))
    return lax.bitcast_convert_type(bits, jnp.float32)


def _softmax_step_t(s_t, v_t, m_ref, acc_ref, h):
    kc = s_t.shape[0]
    m_prev = m_ref[h]
    m_new = jnp.maximum(m_prev, jnp.max(s_t, axis=0, keepdims=True))
    alpha = jnp.exp2(m_prev - m_new)
    p_t = jnp.exp2(s_t - m_new).astype(jnp.bfloat16)
    v_aug = jnp.concatenate([v_t, jnp.ones((SUM_ROWS, kc), jnp.bfloat16)], axis=0)
    acc_ref[h] = alpha * acc_ref[h] + _dot(v_aug, p_t)
    m_ref[h] = m_new


def _softmax_init(m_ref, acc_ref):
    m_ref[...] = jnp.full(m_ref.shape, NEG_BIG, jnp.float32)
    acc_ref[...] = jnp.zeros(acc_ref.shape, jnp.float32)


def _softmax_finish(acc_ref, o_ref):
    heads, rows, _ = acc_ref.shape
    dv = rows - SUM_ROWS
    o_t = jnp.concatenate(
        [acc_ref[h, 0:dv, :] / acc_ref[h, dv:dv + 1, :] for h in range(heads)], axis=0)
    o_ref[...] = o_t.T.astype(o_ref.dtype)


def _attention_pipeline(n_last, half_last, kc, issue_qk, consume, s_refs):
    s0, s1 = s_refs
    issue_qk(0, 0, s0, kc)

    def body(c, carry):
        issue_qk(c, 1, s1, kc)
        consume(c, 0, s0, False, kc)
        issue_qk(c + 1, 0, s0, kc)
        consume(c, 1, s1, False, kc)
        return carry

    lax.fori_loop(0, n_last, body, 0)

    def last_chunk(width):
        issue_qk(n_last, 1, s1, width)
        consume(n_last, 0, s0, True, width)
        consume(n_last, 1, s1, True, width)

    pl.when(half_last)(lambda: last_chunk(kc // 2))
    pl.when(jnp.logical_not(half_last))(lambda: last_chunk(kc))


def _dsa_kernel(top_k, q_ref, qi_ref, wi_ref, ki_ref, k_ref, vt_ref, o_ref,
                sc_ref, qm_ref, qp_ref, m_ref, acc_ref, s0_ref, s1_ref, bias_ref):
    i = pl.program_id(1)
    tq, kc, ks = DSA_Q_TILE, DSA_KEY_CHUNK, DSA_COUNT_CHUNK
    n_chunks = (i * tq) // kc + 1
    n_count = (i * tq) // ks + 1
    low = lax.broadcasted_iota(jnp.int32, (tq, LANES), 1) < IDX_DIM

    for j in range(IDX_HEADS // 2):
        t = qi_ref[:, j * LANES:(j + 1) * LANES].astype(jnp.float32)
        qm_ref[j, 0:tq, :] = jnp.where(low, t, 0.0).astype(jnp.bfloat16)
        qm_ref[j, tq:2 * tq, :] = jnp.where(low, 0.0, t).astype(jnp.bfloat16)
    for j in range(A_HEADS // 2):
        t = q_ref[:, j * LANES:(j + 1) * LANES].astype(jnp.float32)
        qp_ref[2 * j] = jnp.where(low, t, 0.0).astype(jnp.bfloat16)
        qp_ref[2 * j + 1] = jnp.where(low, 0.0, t).astype(jnp.bfloat16)
    w_t = wi_ref[...].T

    q_pos = i * tq + lax.broadcasted_iota(jnp.int32, (kc, tq), 1)
    k_off = lax.broadcasted_iota(jnp.int32, (kc, tq), 0)

    def score_chunk(c, carry):
        base = pl.multiple_of(c * kc, kc)
        kch = ki_ref[pl.ds(base, kc), :]
        acc = jnp.zeros((kc, tq), jnp.float32)
        for j in range(IDX_HEADS // 2):
            r = _dot_nt(kch, qm_ref[j])
            acc = acc + jnp.maximum(r[:, :tq], 0.0) * w_t[2 * j:2 * j + 1, :]
            acc = acc + jnp.maximum(r[:, tq:], 0.0) * w_t[2 * j + 1:2 * j + 2, :]
        sc_ref[pl.ds(base, kc), :] = jnp.where(base + k_off <= q_pos, acc, -jnp.inf)
        return carry

    lax.fori_loop(0, n_chunks, score_chunk, 0)

    part = DSA_COUNT_ROWS

    def count(pred):
        def count_chunk(c, cnt):
            base = pl.multiple_of(c * ks, ks)
            for r in range(ks // part):
                row0 = base + r * part
                cnt = jnp.where(pred(sc_ref[pl.ds(row0, part), :], row0), cnt + 1.0, cnt)
            return cnt

        cnt = lax.fori_loop(0, n_count, count_chunk, jnp.zeros((part, tq), jnp.float32))
        return jnp.sum(cnt, axis=0, keepdims=True)

    def bisect(step, carry):
        lo, n_lo = carry
        cand = lo + jnp.left_shift(jnp.int32(1), 31 - step)
        cand_f = _key_to_float(cand)
        total = count(lambda blk, row0: blk >= cand_f)
        take = total >= top_k
        return jnp.where(take, cand, lo), jnp.where(take, total, n_lo)

    lo, n_ge = lax.fori_loop(
        0, 32, bisect, (jnp.full((1, tq), jnp.iinfo(jnp.int32).min, jnp.int32),
                        jnp.zeros((1, tq), jnp.float32)))
    thr = _key_to_float(jnp.maximum(lo, KEY_LOWEST_FINITE))

    @pl.when(jnp.max(n_ge) > top_k)
    def _():
        n_gt = count(lambda blk, row0: blk > thr)
        need = top_k - n_gt
        row = lax.broadcasted_iota(jnp.int32, (part, tq), 0)
        index_bits = (sc_ref.shape[0] - 1).bit_length()

        def index_search(step, cut):
            cand = cut + jnp.left_shift(jnp.int32(1), index_bits - 1 - step)
            below = count(lambda blk, row0: (blk == thr) & (row0 + row < cand))
            return jnp.where(below < need, cand, cut)

        cut = lax.fori_loop(0, index_bits, index_search, jnp.zeros((1, tq), jnp.int32))

        def drop_chunk(c, carry):
            base = pl.multiple_of(c * ks, ks)
            blk = sc_ref[pl.ds(base, ks), :]
            pos = base + lax.broadcasted_iota(jnp.int32, (ks, tq), 0)
            sc_ref[pl.ds(base, ks), :] = jnp.where((blk == thr) & (pos > cut), -jnp.inf, blk)
            return carry

        lax.fori_loop(0, n_count, drop_chunk, 0)

    _softmax_init(m_ref, acc_ref)

    group = A_HEADS // 2

    def issue_qk(c, g, s_ref, width):
        base = pl.multiple_of(c * kc, kc)
        for u in range(group):
            h = g * group + u
            cols = slice((h // 2) * LANES, (h // 2 + 1) * LANES)
            s_ref[u, 0:width, :] = _dot_nt(k_ref[pl.ds(base, width), cols], qp_ref[h])

    def consume(c, g, s_ref, last, width):
        base = pl.multiple_of(c * kc, kc)
        if g == 0:
            bias_ref[0:width, :] = jnp.where(
                sc_ref[pl.ds(base, width), :] >= thr, 0.0, NEG_BIG)
        for u in range(group):
            h = g * group + u
            v_t = vt_ref[h * A_HEAD_DIM:(h + 1) * A_HEAD_DIM, pl.ds(base, width)]
            s_t = s_ref[u, 0:width, :] + bias_ref[0:width, :]
            _softmax_step_t(s_t, v_t, m_ref, acc_ref, h)

    half_last = (i + 1) * tq - (n_chunks - 1) * kc <= kc // 2
    _attention_pipeline(n_chunks - 1, half_last, kc, issue_qk, consume, (s0_ref, s1_ref))
    _softmax_finish(acc_ref, o_ref)


def _dsa_attention(q, qi, wi, ki, k, vt, batch, seq):
    n = q.shape[0]
    tq = DSA_Q_TILE
    nq = seq // tq
    top_k = min(TOPK_MAX, seq // 4)
    qrow = lambda width: pl.BlockSpec((tq, width), lambda b, i: (b * nq + i, 0))
    per_batch = lambda width: pl.BlockSpec((seq, width), lambda b, i: (b, 0))
    return pl.pallas_call(
        functools.partial(_dsa_kernel, top_k),
        grid=(batch, nq),
        in_specs=[qrow(A_WIDTH), qrow(IDX_HEADS * IDX_DIM), qrow(LANES),
                  per_batch(LANES), per_batch(A_WIDTH),
                  pl.BlockSpec((A_WIDTH, seq), lambda b, i: (0, b))],
        out_specs=qrow(A_WIDTH),
        out_shape=jax.ShapeDtypeStruct((n, A_WIDTH), jnp.bfloat16),
        scratch_shapes=[
            pltpu.VMEM((seq, tq), jnp.float32),
            pltpu.VMEM((IDX_HEADS // 2, 2 * tq, LANES), jnp.bfloat16),
            pltpu.VMEM((A_HEADS, tq, LANES), jnp.bfloat16),
            pltpu.VMEM((A_HEADS, 1, tq), jnp.float32),
            pltpu.VMEM((A_HEADS, A_HEAD_DIM + SUM_ROWS, tq), jnp.float32),
            pltpu.VMEM((A_HEADS // 2, DSA_KEY_CHUNK, tq), jnp.float32),
            pltpu.VMEM((A_HEADS // 2, DSA_KEY_CHUNK, tq), jnp.float32),
            pltpu.VMEM((DSA_KEY_CHUNK, tq), jnp.float32),
        ],
        compiler_params=_params("arbitrary", "arbitrary"),
        name="dsa_attention",
    )(q, qi, wi, ki, k, vt)


def _mix_mlp_kernel(n_in, *refs):
    a_refs, wo_refs = refs[:n_in], refs[n_in:2 * n_in]
    (x_ref, gmix_ref, gpre_ref, gpost_ref, w1_ref, w2_ref, o_ref,
     x1_ref, hn_ref, acc_ref) = refs[2 * n_in:]
    f = pl.program_id(1)

    @pl.when(f == 0)
    def _():
        y = _dot(a_refs[0][...], wo_refs[0][...])
        for a_ref, wo_ref in zip(a_refs[1:], wo_refs[1:]):
            y = y + _dot(a_ref[...], wo_ref[...])
        x1 = x_ref[...] + _rms(y, gmix_ref[...])
        x1_ref[...] = x1
        hn_ref[...] = _rms(x1, gpre_ref[...]).astype(jnp.bfloat16)
        acc_ref[...] = jnp.zeros(acc_ref.shape, jnp.float32)

    h = jnp.maximum(_dot(hn_ref[...], w1_ref[...]), 0.0)
    acc_ref[...] += _dot((h * h).astype(jnp.bfloat16), w2_ref[...])

    @pl.when(f == pl.num_programs(1) - 1)
    def _():
        o_ref[...] = x1_ref[...] + _rms(acc_ref[...], gpost_ref[...])


def _mix_mlp(acts, w_outs, x, g_mix, gpre, gpost, w1, w2):
    n, d = x.shape
    ff = w1.shape[1]
    tm, tf = MLP_ROW_TILE, MLP_FF_TILE
    row = lambda a: pl.BlockSpec((tm, a.shape[1]), lambda i, f: (i, 0))
    full = lambda a: pl.BlockSpec(a.shape, lambda i, f: (0,) * a.ndim)
    return pl.pallas_call(
        functools.partial(_mix_mlp_kernel, len(acts)),
        grid=(n // tm, ff // tf),
        in_specs=[row(a) for a in acts] + [full(w) for w in w_outs]
        + [row(x), full(g_mix), full(gpre), full(gpost),
           pl.BlockSpec((d, tf), lambda i, f: (0, f)),
           pl.BlockSpec((tf, d), lambda i, f: (f, 0))],
        out_specs=row(x),
        out_shape=jax.ShapeDtypeStruct((n, d), jnp.float32),
        scratch_shapes=[pltpu.VMEM((tm, d), jnp.float32), pltpu.VMEM((tm, d), jnp.bfloat16),
                        pltpu.VMEM((tm, d), jnp.float32)],
        compiler_params=_params("arbitrary", "arbitrary"),
        name="mix_mlp",
    )(*acts, *w_outs, x, g_mix, gpre, gpost, w1, w2)


_DKR, _DKV, _DEND = C_Q_RANK, C_Q_RANK + LANES, C_Q_RANK + LANES + C_KV_RANK


def _mla_in_kernel(x_ref, g_ref, wd_ref, qn_ref, kvn_ref, wuq_ref, wuk_ref, wuvt_ref,
                   c_ref, s_ref, q_ref, k_ref, vt_ref):
    hn = _rms(x_ref[...], g_ref[...]).astype(jnp.bfloat16)
    c, s = c_ref[...].T, s_ref[...].T
    scale = (C_NOPE_DIM + C_ROPE_DIM) ** -0.5 * LOG2E
    cq, sq = c * scale, s * scale

    def rope(y, ct, st):
        return y * ct + pltpu.roll(y, LANES - C_ROPE_DIM, axis=1) * st

    qr = _dot(hn, wd_ref[:, 0:_DKV])
    qa = _rms(qr[:, :_DKR], qn_ref[...]).astype(jnp.bfloat16)
    ckv = _rms(_dot(hn, wd_ref[:, _DKV:_DEND]), kvn_ref[...]).astype(jnp.bfloat16)
    kr = rope(qr[:, _DKR:], c, s)
    vt_ref[...] = _dot_nt(wuvt_ref[...], ckv).astype(vt_ref.dtype)

    n_wide = C_HEADS * LANES // MXU_COLS

    def up(j):
        cols = slice(j * MXU_COLS, (j + 1) * MXU_COLS)
        return _dot(qa, wuq_ref[:, cols]), _dot(ckv, wuk_ref[:, cols])

    nxt = up(0)
    for j in range(n_wide):
        q2, k2 = nxt
        if j + 1 < n_wide:
            nxt = up(j + 1)
        for u in range(MXU_COLS // LANES):
            cols = slice(j * MXU_COLS + u * LANES, j * MXU_COLS + (u + 1) * LANES)
            q_ref[:, cols] = rope(q2[:, u * LANES:(u + 1) * LANES], cq, sq).astype(q_ref.dtype)
            k_ref[:, cols] = (k2[:, u * LANES:(u + 1) * LANES] + kr).astype(k_ref.dtype)


def _mla_in_proj(x, gain, w_down, q_norm, kv_norm, w_uq, w_uk, w_uvt, tabs):
    n, d = x.shape
    tm = MLA_ROW_TILE
    row = lambda width: pl.BlockSpec((tm, width), lambda i: (i, 0))
    full = lambda a: pl.BlockSpec(a.shape, lambda i: (0,) * a.ndim)
    tab = pl.BlockSpec((LANES, tm), lambda i: (0, i))
    bf = jnp.bfloat16
    return pl.pallas_call(
        _mla_in_kernel,
        grid=(n // tm,),
        in_specs=[row(d), full(gain), full(w_down), full(q_norm), full(kv_norm),
                  full(w_uq), full(w_uk), full(w_uvt), tab, tab],
        out_specs=(row(C_HEADS * LANES), row(C_HEADS * LANES),
                   pl.BlockSpec((C_HEADS * C_V_DIM, tm), lambda i: (0, i))),
        out_shape=(jax.ShapeDtypeStruct((n, C_HEADS * LANES), bf),
                   jax.ShapeDtypeStruct((n, C_HEADS * LANES), bf),
                   jax.ShapeDtypeStruct((C_HEADS * C_V_DIM, n), bf)),
        compiler_params=_params("arbitrary"),
        name="mla_in_proj",
    )(x, gain, w_down, q_norm, kv_norm, w_uq, w_uk, w_uvt, *tabs)


def _mla_attn_kernel(q_ref, k_ref, vt_ref, o_ref, m_ref, acc_ref, s0_ref, s1_ref):
    i = pl.program_id(2)
    tq, kc, group = MLA_Q_TILE, MLA_KEY_CHUNK, MLA_HEAD_GROUP // 2
    n_full = (i * tq) // kc
    _softmax_init(m_ref, acc_ref)

    def issue_qk(c, g, s_ref, width):
        base = pl.multiple_of(c * kc, kc)
        for u in range(group):
            cols = slice((g * group + u) * LANES, (g * group + u + 1) * LANES)
            s_ref[u, 0:width, :] = _dot_nt(k_ref[pl.ds(base, width), cols], q_ref[:, cols])

    def consume(c, g, s_ref, last, width):
        base = pl.multiple_of(c * kc, kc)
        if last:
            q_pos = i * tq + lax.broadcasted_iota(jnp.int32, (width, tq), 1)
            k_pos = base + lax.broadcasted_iota(jnp.int32, (width, tq), 0)
            bias_t = jnp.where(k_pos <= q_pos, 0.0, NEG_BIG)
        for u in range(group):
            h = g * group + u
            s_t = s_ref[u, 0:width, :]
            if last:
                s_t = s_t + bias_t
            v_t = vt_ref[h * C_V_DIM:(h + 1) * C_V_DIM, pl.ds(base, width)]
            _softmax_step_t(s_t, v_t, m_ref, acc_ref, h)

    half_last = (i + 1) * tq - n_full * kc <= kc // 2
    _attention_pipeline(n_full, half_last, kc, issue_qk, consume, (s0_ref, s1_ref))
    _softmax_finish(acc_ref, o_ref)


def _mla_attention(q, k, vt, batch, seq):
    n = q.shape[0]
    tq, hg = MLA_Q_TILE, MLA_HEAD_GROUP
    nq = seq // tq
    return pl.pallas_call(
        _mla_attn_kernel,
        grid=(batch, C_HEADS // hg, nq),
        in_specs=[pl.BlockSpec((tq, hg * LANES), lambda b, g, i: (b * nq + i, g)),
                  pl.BlockSpec((seq, hg * LANES), lambda b, g, i: (b, g)),
                  pl.BlockSpec((hg * C_V_DIM, seq), lambda b, g, i: (g, b))],
        out_specs=pl.BlockSpec((tq, hg * C_V_DIM), lambda b, g, i: (b * nq + i, g)),
        out_shape=jax.ShapeDtypeStruct((n, C_HEADS * C_V_DIM), jnp.bfloat16),
        scratch_shapes=[pltpu.VMEM((hg, 1, tq), jnp.float32),
                        pltpu.VMEM((hg, C_V_DIM + SUM_ROWS, tq), jnp.float32),
                        pltpu.VMEM((hg // 2, MLA_KEY_CHUNK, tq), jnp.float32),
                        pltpu.VMEM((hg // 2, MLA_KEY_CHUNK, tq), jnp.float32)],
        compiler_params=_params("arbitrary", "arbitrary", "arbitrary"),
        name="mla_attention",
    )(q, k, vt)


def _rope_tables(positions, rot_dim, lead, period):
    inv_freq = ROPE_THETA ** (-jnp.arange(0, rot_dim, 2, dtype=jnp.float32) / rot_dim)
    ang = inv_freq[:, None] * positions.astype(jnp.float32).reshape(1, -1)
    cos, sin = jnp.cos(ang), jnp.sin(ang)
    half, n = cos.shape
    ones = lambda w: jnp.ones((w, n), jnp.float32)
    zeros = lambda w: jnp.zeros((w, n), jnp.float32)
    tail = period - lead - rot_dim
    c = jnp.concatenate([ones(lead), cos, cos, ones(tail)], axis=0)
    s_next = jnp.concatenate([zeros(lead), -sin, zeros(half + tail)], axis=0)
    s_prev = jnp.concatenate([zeros(lead + half), sin, zeros(tail)], axis=0)
    reps = LANES // period
    return tuple(jnp.tile(t, (reps, 1)) for t in (c, s_next, s_prev))


def _pack_even_w_in(w):
    d = w.shape[0]
    q, k, v, qi, ki, wi, gb, gc, xi = jnp.split(
        w, [512, 1024, 1536, 2560, 2624, 2640, 3152, 3664], axis=1)
    pad = jnp.zeros((d, LANES - IDX_HEADS), w.dtype)
    packed = jnp.concatenate([q, k, qi, ki, ki, wi, pad, gb, gc, xi], axis=1)
    return packed.astype(jnp.bfloat16), v.T.astype(jnp.bfloat16)


def _pad_heads(w, heads, width):
    r = w.shape[0]
    w = w.reshape(r, heads, width)
    return jnp.pad(w, ((0, 0), (0, 0), (0, LANES - width))).reshape(r, heads * LANES)


def _rope_swapped(w_rope):
    half = C_ROPE_DIM // 2
    return jnp.concatenate([w_rope, w_rope[..., half:], w_rope[..., :half]], axis=-1)


def _mla_rope_tables(positions):
    inv_freq = ROPE_THETA ** (-jnp.arange(0, C_ROPE_DIM, 2, dtype=jnp.float32) / C_ROPE_DIM)
    ang = inv_freq[:, None] * positions.astype(jnp.float32).reshape(1, -1)
    cos, sin = jnp.cos(ang), jnp.sin(ang)
    n = cos.shape[1]
    tail = jnp.zeros((LANES - C_NOPE_DIM - C_ROPE_DIM, n), jnp.float32)
    c = jnp.concatenate([jnp.ones((C_NOPE_DIM, n), jnp.float32), cos, cos, tail], axis=0)
    s = jnp.concatenate([jnp.zeros((C_NOPE_DIM, n), jnp.float32), -sin, sin, tail], axis=0)
    return c, s


def kernel(x, positions, norm_mix_pre, norm_mix_post, norm_ffn_pre, norm_ffn_post, even_w_in, even_conv_w, even_w_out, odd_w_dq, odd_q_norm, odd_w_uq, odd_w_dkv, odd_kv_norm, odd_w_ukv, odd_w_o, mlp_w1, mlp_w2):
    batch, seq, d = x.shape
    depth = norm_mix_pre.shape[0]
    bf = jnp.bfloat16
    tabs_a = _rope_tables(positions, A_ROT_DIM, 0, A_HEAD_DIM)
    tabs_c = _mla_rope_tables(positions)
    h = x.reshape(batch * seq, d)
    row = lambda a: a.reshape(1, -1)
    for layer in range(depth):
        j = layer // 2
        g_pre, g_post = row(norm_mix_pre[layer]), row(norm_mix_post[layer])
        if layer % 2 == 0:
            w_packed, w_vt = _pack_even_w_in(even_w_in[j])
            q, k, vt, qi, ki, wi, conv = _even_in_proj(
                h, g_pre, w_packed, w_vt, tabs_a, even_conv_w[j], seq)
            attn = _dsa_attention(q, qi, wi, ki, k, vt, batch, seq)
            w_out = even_w_out[j].astype(bf)
            acts, w_outs = [attn, conv], [w_out[:A_WIDTH], w_out[A_WIDTH:]]
        else:
            w_dkv = odd_w_dkv[j]
            w_down = jnp.concatenate(
                [odd_w_dq[j], jnp.zeros((d, C_NOPE_DIM), w_dkv.dtype),
                 _rope_swapped(w_dkv[:, C_KV_RANK:]), w_dkv[:, :C_KV_RANK]], axis=1)
            w_ukv = odd_w_ukv[j].reshape(C_KV_RANK, C_HEADS, C_NOPE_DIM + C_V_DIM)
            w_uk = _pad_heads(w_ukv[:, :, :C_NOPE_DIM].reshape(C_KV_RANK, -1), C_HEADS, C_NOPE_DIM)
            w_uv = w_ukv[:, :, C_NOPE_DIM:].reshape(C_KV_RANK, -1)
            w_uq = odd_w_uq[j].reshape(C_Q_RANK, C_HEADS, C_NOPE_DIM + C_ROPE_DIM)
            w_uq = jnp.concatenate(
                [w_uq[:, :, :C_NOPE_DIM], _rope_swapped(w_uq[:, :, C_NOPE_DIM:])],
                axis=-1).reshape(C_Q_RANK, C_HEADS * LANES)
            q, k, vt = _mla_in_proj(h, g_pre, w_down.astype(bf), row(odd_q_norm[j]),
                                    row(odd_kv_norm[j]), w_uq.astype(bf), w_uk.astype(bf),
                                    w_uv.T.astype(bf), tabs_c)
            acts, w_outs = [_mla_attention(q, k, vt, batch, seq)], [odd_w_o[j].astype(bf)]
        h = _mix_mlp(acts, w_outs, h, g_post, row(norm_ffn_pre[layer]),
                     row(norm_ffn_post[layer]), mlp_w1[layer].astype(bf),
                     mlp_w2[layer].astype(bf))
    return h.reshape(batch, seq, d)
```

```python
import functools

import jax
import jax.numpy as jnp
from jax import lax
from jax.experimental import pallas as pl
from jax.experimental.pallas import tpu as pltpu

ROPE_THETA = 500000.0
NORM_EPS = 1e-6
LANES = 128
MXU_COLS = 256
NEG_BIG = -1e30
LOG2E = 1.4426950408889634
SUM_ROWS = 16

A_HEADS = 8
A_HEAD_DIM = 64
A_WIDTH = A_HEADS * A_HEAD_DIM
A_ROT_DIM = A_HEAD_DIM // 4
IDX_HEADS = 16
IDX_DIM = 64
TOPK_MAX = 256
B_WIDTH = 512
CONV_WIDTH = 3
C_HEADS = 16
C_NOPE_DIM = 64
C_ROPE_DIM = 32
C_V_DIM = 64
C_Q_RANK = 384
C_KV_RANK = 256

KEY_LOWEST_FINITE = -2139095040
COARSE_SLACK = 0x8000

ROW_TILE = 512
MLA_ROW_TILE = 512
MLP_ROW_TILE = 512
MLP_FF_TILE = 1024
DSA_Q_TILE = 256
DSA_KEY_CHUNK = 512
DSA_COUNT_CHUNK = 256
DSA_COUNT_ROWS = 64
MLA_Q_TILE = 256
MLA_KEY_CHUNK = 512
MLA_HEAD_GROUP = 8
VMEM_LIMIT = 56 * 1024 * 1024


def _dot(a, b):
    return jnp.dot(a, b, preferred_element_type=jnp.float32)


def _dot_nt(a, b):
    return lax.dot_general(a, b, (((1,), (1,)), ((), ())), preferred_element_type=jnp.float32)


def _rms(x, gain):
    ms = jnp.mean(x * x, axis=-1, keepdims=True)
    return x * lax.rsqrt(ms + NORM_EPS) * gain


def _rope_tile(x, c, s_next, s_prev, half):
    return (x * c + pltpu.roll(x, LANES - half, axis=1) * s_next
            + pltpu.roll(x, half, axis=1) * s_prev)


def _params(*semantics):
    return pltpu.CompilerParams(dimension_semantics=semantics, vmem_limit_bytes=VMEM_LIMIT)


_EQ, _EK, _EQI, _EKI, _EWI, _EGB, _EGC, _EXI, _EEND = (
    0, 512, 1024, 2048, 2176, 2304, 2816, 3328, 3840)


def _even_in_kernel(seq_tiles, x_ref, g_ref, w_ref, wvt_ref, c_ref, sn_ref, sp_ref, cw_ref,
                    q_ref, k_ref, vt_ref, qi_ref, ki_ref, wi_ref, conv_ref, carry_ref):
    i = pl.program_id(0)
    tm = x_ref.shape[0]
    hn = _rms(x_ref[...], g_ref[...]).astype(jnp.bfloat16)
    c, sn, sp = c_ref[...].T, sn_ref[...].T, sp_ref[...].T
    half = A_ROT_DIM // 2

    def rope_section(start, width, out_ref, scale=None):
        for t in range(width // MXU_COLS):
            y2 = _dot(hn, w_ref[:, start + t * MXU_COLS:start + (t + 1) * MXU_COLS])
            for u in range(MXU_COLS // LANES):
                y = _rope_tile(y2[:, u * LANES:(u + 1) * LANES], c, sn, sp, half)
                if scale is not None:
                    y = y * scale
                lo = t * MXU_COLS + u * LANES
                out_ref[:, lo:lo + LANES] = y.astype(out_ref.dtype)

    rope_section(_EQ, A_WIDTH, q_ref, A_HEAD_DIM ** -0.5 * LOG2E)
    rope_section(_EK, A_WIDTH, k_ref)
    vt_ref[...] = _dot_nt(wvt_ref[...], hn).astype(vt_ref.dtype)
    rope_section(_EQI, IDX_HEADS * IDX_DIM, qi_ref)
    kw = _dot(hn, w_ref[:, _EKI:_EGB])
    ki_ref[...] = _rope_tile(kw[:, :LANES], c, sn, sp, half).astype(ki_ref.dtype)
    wi_ref[...] = kw[:, LANES:] * (IDX_HEADS ** -0.5 * IDX_DIM ** -0.5)

    gate_b = _dot(hn, w_ref[:, _EGB:_EGC])
    u = _dot(hn, w_ref[:, _EGC:_EXI]) * _dot(hn, w_ref[:, _EXI:_EEND])
    seq_start = (i % seq_tiles) == 0
    prev = jnp.where(seq_start, 0.0, carry_ref[...])
    carry_ref[...] = u[tm - 8:, :]
    row = lax.broadcasted_iota(jnp.int32, u.shape, 0)
    u1 = jnp.where(row == 0, prev[7:8, :], pltpu.roll(u, 1, axis=0))
    u2 = pltpu.roll(u, 2, axis=0)
    u2 = jnp.where(row == 0, prev[6:7, :], jnp.where(row == 1, prev[7:8, :], u2))
    y = cw_ref[0:1, :] * u2 + cw_ref[1:2, :] * u1 + cw_ref[2:3, :] * u
    conv_ref[...] = (gate_b * y).astype(conv_ref.dtype)


def _even_in_proj(x, gain, w_packed, w_vt, tabs, conv_w, seq):
    n, d = x.shape
    tm = ROW_TILE
    row = lambda width: pl.BlockSpec((tm, width), lambda i: (i, 0))
    full = lambda a: pl.BlockSpec(a.shape, lambda i: (0,) * a.ndim)
    tab = pl.BlockSpec((LANES, tm), lambda i: (0, i))
    bf = jnp.bfloat16
    out_shape = (
        jax.ShapeDtypeStruct((n, A_WIDTH), bf), jax.ShapeDtypeStruct((n, A_WIDTH), bf),
        jax.ShapeDtypeStruct((A_WIDTH, n), bf), jax.ShapeDtypeStruct((n, IDX_HEADS * IDX_DIM), bf),
        jax.ShapeDtypeStruct((n, LANES), bf), jax.ShapeDtypeStruct((n, LANES), jnp.float32),
        jax.ShapeDtypeStruct((n, B_WIDTH), bf))
    return pl.pallas_call(
        functools.partial(_even_in_kernel, seq // tm),
        grid=(n // tm,),
        in_specs=[row(d), full(gain), full(w_packed), full(w_vt), tab, tab, tab,
                  full(conv_w)],
        out_specs=(row(A_WIDTH), row(A_WIDTH), pl.BlockSpec((A_WIDTH, tm), lambda i: (0, i)),
                   row(IDX_HEADS * IDX_DIM), row(LANES), row(LANES), row(B_WIDTH)),
        out_shape=out_shape,
        scratch_shapes=[pltpu.VMEM((8, B_WIDTH), jnp.float32)],
        compiler_params=_params("arbitrary"),
        name="even_in_proj",
    )(x, gain, w_packed, w_vt, *tabs, conv_w)


def _key_to_float(key):
    bits = key ^ ((key >> 31) & jnp.int32(0x7FFFFFFF))
    return lax.bitcast_convert_type(bits, jnp.float32)


def _softmax_step_t(s_t, v_t, m_ref, acc_ref, h):
    kc = s_t.shape[0]
    m_prev = m_ref[h]
    m_new = jnp.maximum(m_prev, jnp.max(s_t, axis=0, keepdims=True))
    alpha = jnp.exp2(m_prev - m_new)
    p_t = jnp.exp2(s_t - m_new).astype(jnp.bfloat16)
    v_aug = jnp.concatenate([v_t, jnp.ones((SUM_ROWS, kc), jnp.bfloat16)], axis=0)
    acc_ref[h] = alpha * acc_ref[h] + _dot(v_aug, p_t)
    m_ref[h] = m_new


def _softmax_init(m_ref, acc_ref):
    m_ref[...] = jnp.full(m_ref.shape, NEG_BIG, jnp.float32)
    acc_ref[...] = jnp.zeros(acc_ref.shape, jnp.float32)


def _softmax_finish(acc_ref, o_ref):
    heads, rows, _ = acc_ref.shape
    dv = rows - SUM_ROWS
    o_t = jnp.concatenate(
        [acc_ref[h, 0:dv, :] / acc_ref[h, dv:dv + 1, :] for h in range(heads)], axis=0)
    o_ref[...] = o_t.T.astype(o_ref.dtype)


def _attention_pipeline(n_last, half_last, kc, issue_qk, consume, s_refs):
    s0, s1 = s_refs
    issue_qk(0, 0, s0, kc)

    def body(c, carry):
        issue_qk(c, 1, s1, kc)
        consume(c, 0, s0, False, kc)
        issue_qk(c + 1, 0, s0, kc)
        consume(c, 1, s1, False, kc)
        return carry

    lax.fori_loop(0, n_last, body, 0)

    def last_chunk(width):
        issue_qk(n_last, 1, s1, width)
        consume(n_last, 0, s0, True, width)
        consume(n_last, 1, s1, True, width)

    pl.when(half_last)(lambda: last_chunk(kc // 2))
    pl.when(jnp.logical_not(half_last))(lambda: last_chunk(kc))


def _dsa_kernel(top_k, q_ref, qi_ref, wi_ref, ki_ref, k_ref, vt_ref, o_ref,
                sc_ref, hb_ref, qm_ref, qp_ref, m_ref, acc_ref, s0_ref, s1_ref, bias_ref):
    i = pl.program_id(1)
    tq, kc, ks = DSA_Q_TILE, DSA_KEY_CHUNK, DSA_COUNT_CHUNK
    n_chunks = (i * tq) // kc + 1
    n_count = (i * tq) // ks + 1
    low = lax.broadcasted_iota(jnp.int32, (tq, LANES), 1) < IDX_DIM

    for j in range(IDX_HEADS // 2):
        t = qi_ref[:, j * LANES:(j + 1) * LANES].astype(jnp.float32)
        qm_ref[j, 0:tq, :] = jnp.where(low, t, 0.0).astype(jnp.bfloat16)
        qm_ref[j, tq:2 * tq, :] = jnp.where(low, 0.0, t).astype(jnp.bfloat16)
    for j in range(A_HEADS // 2):
        t = q_ref[:, j * LANES:(j + 1) * LANES].astype(jnp.float32)
        qp_ref[2 * j] = jnp.where(low, t, 0.0).astype(jnp.bfloat16)
        qp_ref[2 * j + 1] = jnp.where(low, 0.0, t).astype(jnp.bfloat16)
    w_t = wi_ref[...].T

    q_pos = i * tq + lax.broadcasted_iota(jnp.int32, (kc, tq), 1)
    k_off = lax.broadcasted_iota(jnp.int32, (kc, tq), 0)

    def score_chunk(c, carry):
        base = pl.multiple_of(c * kc, kc)
        kch = ki_ref[pl.ds(base, kc), :]
        acc = jnp.zeros((kc, tq), jnp.float32)
        for j in range(IDX_HEADS // 2):
            r = _dot_nt(kch, qm_ref[j])
            acc = acc + jnp.maximum(r[:, :tq], 0.0) * w_t[2 * j:2 * j + 1, :]
            acc = acc + jnp.maximum(r[:, tq:], 0.0) * w_t[2 * j + 1:2 * j + 2, :]
        masked = jnp.where(base + k_off <= q_pos, acc, -jnp.inf)
        sc_ref[pl.ds(base, kc), :] = masked
        hb_ref[pl.ds(base, kc), :] = masked.astype(jnp.bfloat16)
        return carry

    lax.fori_loop(0, n_chunks, score_chunk, 0)

    part = DSA_COUNT_ROWS

    def count(src_ref, pred):
        one = jnp.ones((), src_ref.dtype)

        def count_chunk(c, cnt):
            base = pl.multiple_of(c * ks, ks)
            for r in range(ks // part):
                row0 = base + r * part
                cnt = jnp.where(pred(src_ref[pl.ds(row0, part), :], row0), cnt + one, cnt)
            return cnt

        cnt = lax.fori_loop(0, n_count, count_chunk, jnp.zeros((part, tq), src_ref.dtype))
        return jnp.sum(cnt.astype(jnp.float32), axis=0, keepdims=True)

    def coarse(step, lo):
        cand = lo + jnp.left_shift(jnp.int32(1), 31 - step)
        cand_b = jnp.broadcast_to(_key_to_float(cand).astype(jnp.bfloat16), (part, tq))
        total = count(hb_ref, lambda blk, row0: blk >= cand_b)
        return jnp.where(total >= top_k, cand, lo)

    k_coarse = lax.fori_loop(
        0, 16, coarse, jnp.full((1, tq), jnp.iinfo(jnp.int32).min, jnp.int32))

    lo0 = jnp.maximum(k_coarse, KEY_LOWEST_FINITE + COARSE_SLACK) - COARSE_SLACK
    hi0 = jnp.maximum(k_coarse + 2 * COARSE_SLACK, lo0)

    def fine(step, carry):
        lo, hi = carry
        mid = lo + jnp.right_shift(hi - lo + 1, 1)
        mid_f = _key_to_float(mid)
        take = count(sc_ref, lambda blk, row0: blk >= mid_f) >= top_k
        return jnp.where(take, mid, lo), jnp.where(take, hi, mid - 1)

    lo, _ = lax.fori_loop(0, (3 * COARSE_SLACK).bit_length(), fine, (lo0, hi0))
    thr = _key_to_float(lo)
    n_ge = count(sc_ref, lambda blk, row0: blk >= thr)

    @pl.when(jnp.max(n_ge) > top_k)
    def _():
        n_gt = count(sc_ref, lambda blk, row0: blk > thr)
        need = top_k - n_gt
        row = lax.broadcasted_iota(jnp.int32, (part, tq), 0)
        index_bits = (sc_ref.shape[0] - 1).bit_length()

        def index_search(step, cut):
            cand = cut + jnp.left_shift(jnp.int32(1), index_bits - 1 - step)
            below = count(sc_ref, lambda blk, row0: (blk == thr) & (row0 + row < cand))
            return jnp.where(below < need, cand, cut)

        cut = lax.fori_loop(0, index_bits, index_search, jnp.zeros((1, tq), jnp.int32))

        def drop_chunk(c, carry):
            base = pl.multiple_of(c * ks, ks)
            blk = sc_ref[pl.ds(base, ks), :]
            pos = base + lax.broadcasted_iota(jnp.int32, (ks, tq), 0)
            sc_ref[pl.ds(base, ks), :] = jnp.where((blk == thr) & (pos > cut), -jnp.inf, blk)
            return carry

        lax.fori_loop(0, n_count, drop_chunk, 0)

    _softmax_init(m_ref, acc_ref)

    group = A_HEADS // 2

    def issue_qk(c, g, s_ref, width):
        base = pl.multiple_of(c * kc, kc)
        for u in range(group):
            h = g * group + u
            cols = slice((h // 2) * LANES, (h // 2 + 1) * LANES)
            s_ref[u, 0:width, :] = _dot_nt(k_ref[pl.ds(base, width), cols], qp_ref[h])

    def consume(c, g, s_ref, last, width):
        base = pl.multiple_of(c * kc, kc)
        if g == 0:
            bias_ref[0:width, :] = jnp.where(
                sc_ref[pl.ds(base, width), :] >= thr, 0.0, NEG_BIG)
        for u in range(group):
            h = g * group + u
            v_t = vt_ref[h * A_HEAD_DIM:(h + 1) * A_HEAD_DIM, pl.ds(base, width)]
            s_t = s_ref[u, 0:width, :] + bias_ref[0:width, :]
            _softmax_step_t(s_t, v_t, m_ref, acc_ref, h)

    half_last = (i + 1) * tq - (n_chunks - 1) * kc <= kc // 2
    _attention_pipeline(n_chunks - 1, half_last, kc, issue_qk, consume, (s0_ref, s1_ref))
    _softmax_finish(acc_ref, o_ref)


def _dsa_attention(q, qi, wi, ki, k, vt, batch, seq):
    n = q.shape[0]
    tq = DSA_Q_TILE
    nq = seq // tq
    top_k = min(TOPK_MAX, seq // 4)
    qrow = lambda width: pl.BlockSpec((tq, width), lambda b, i: (b * nq + i, 0))
    per_batch = lambda width: pl.BlockSpec((seq, width), lambda b, i: (b, 0))
    return pl.pallas_call(
        functools.partial(_dsa_kernel, top_k),
        grid=(batch, nq),
        in_specs=[qrow(A_WIDTH), qrow(IDX_HEADS * IDX_DIM), qrow(LANES),
                  per_batch(LANES), per_batch(A_WIDTH),
                  pl.BlockSpec((A_WIDTH, seq), lambda b, i: (0, b))],
        out_specs=qrow(A_WIDTH),
        out_shape=jax.ShapeDtypeStruct((n, A_WIDTH), jnp.bfloat16),
        scratch_shapes=[
            pltpu.VMEM((seq, tq), jnp.float32),
            pltpu.VMEM((seq, tq), jnp.bfloat16),
            pltpu.VMEM((IDX_HEADS // 2, 2 * tq, LANES), jnp.bfloat16),
            pltpu.VMEM((A_HEADS, tq, LANES), jnp.bfloat16),
            pltpu.VMEM((A_HEADS, 1, tq), jnp.float32),
            pltpu.VMEM((A_HEADS, A_HEAD_DIM + SUM_ROWS, tq), jnp.float32),
            pltpu.VMEM((A_HEADS // 2, DSA_KEY_CHUNK, tq), jnp.float32),
            pltpu.VMEM((A_HEADS // 2, DSA_KEY_CHUNK, tq), jnp.float32),
            pltpu.VMEM((DSA_KEY_CHUNK, tq), jnp.float32),
        ],
        compiler_params=_params("arbitrary", "arbitrary"),
        name="dsa_attention",
    )(q, qi, wi, ki, k, vt)


def _mix_mlp_kernel(n_in, *refs):
    a_refs, wo_refs = refs[:n_in], refs[n_in:2 * n_in]
    (x_ref, gmix_ref, gpre_ref, gpost_ref, w1_ref, w2_ref, o_ref,
     x1_ref, hn_ref, acc_ref) = refs[2 * n_in:]
    f = pl.program_id(1)

    @pl.when(f == 0)
    def _():
        y = _dot(a_refs[0][...], wo_refs[0][...])
        for a_ref, wo_ref in zip(a_refs[1:], wo_refs[1:]):
            y = y + _dot(a_ref[...], wo_ref[...])
        x1 = x_ref[...] + _rms(y, gmix_ref[...])
        x1_ref[...] = x1
        hn_ref[...] = _rms(x1, gpre_ref[...]).astype(jnp.bfloat16)
        acc_ref[...] = jnp.zeros(acc_ref.shape, jnp.float32)

    h = jnp.maximum(_dot(hn_ref[...], w1_ref[...]), 0.0)
    acc_ref[...] += _dot((h * h).astype(jnp.bfloat16), w2_ref[...])

    @pl.when(f == pl.num_programs(1) - 1)
    def _():
        o_ref[...] = x1_ref[...] + _rms(acc_ref[...], gpost_ref[...])


def _mix_mlp(acts, w_outs, x, g_mix, gpre, gpost, w1, w2):
    n, d = x.shape
    ff = w1.shape[1]
    tm, tf = MLP_ROW_TILE, MLP_FF_TILE
    row = lambda a: pl.BlockSpec((tm, a.shape[1]), lambda i, f: (i, 0))
    full = lambda a: pl.BlockSpec(a.shape, lambda i, f: (0,) * a.ndim)
    return pl.pallas_call(
        functools.partial(_mix_mlp_kernel, len(acts)),
        grid=(n // tm, ff // tf),
        in_specs=[row(a) for a in acts] + [full(w) for w in w_outs]
        + [row(x), full(g_mix), full(gpre), full(gpost),
           pl.BlockSpec((d, tf), lambda i, f: (0, f)),
           pl.BlockSpec((tf, d), lambda i, f: (f, 0))],
        out_specs=row(x),
        out_shape=jax.ShapeDtypeStruct((n, d), jnp.float32),
        scratch_shapes=[pltpu.VMEM((tm, d), jnp.float32), pltpu.VMEM((tm, d), jnp.bfloat16),
                        pltpu.VMEM((tm, d), jnp.float32)],
        compiler_params=_params("arbitrary", "arbitrary"),
        name="mix_mlp",
    )(*acts, *w_outs, x, g_mix, gpre, gpost, w1, w2)


_DKR, _DKV, _DEND = C_Q_RANK, C_Q_RANK + LANES, C_Q_RANK + LANES + C_KV_RANK


def _mla_in_kernel(x_ref, g_ref, wd_ref, qn_ref, kvn_ref, wuq_ref, wuk_ref, wuvt_ref,
                   c_ref, s_ref, q_ref, k_ref, vt_ref):
    hn = _rms(x_ref[...], g_ref[...]).astype(jnp.bfloat16)
    c, s = c_ref[...].T, s_ref[...].T
    scale = (C_NOPE_DIM + C_ROPE_DIM) ** -0.5 * LOG2E
    cq, sq = c * scale, s * scale

    def rope(y, ct, st):
        return y * ct + pltpu.roll(y, LANES - C_ROPE_DIM, axis=1) * st

    qr = _dot(hn, wd_ref[:, 0:_DKV])
    qa = _rms(qr[:, :_DKR], qn_ref[...]).astype(jnp.bfloat16)
    ckv = _rms(_dot(hn, wd_ref[:, _DKV:_DEND]), kvn_ref[...]).astype(jnp.bfloat16)
    kr = rope(qr[:, _DKR:], c, s)
    vt_ref[...] = _dot_nt(wuvt_ref[...], ckv).astype(vt_ref.dtype)

    n_wide = C_HEADS * LANES // MXU_COLS

    def up(j):
        cols = slice(j * MXU_COLS, (j + 1) * MXU_COLS)
        return _dot(qa, wuq_ref[:, cols]), _dot(ckv, wuk_ref[:, cols])

    nxt = up(0)
    for j in range(n_wide):
        q2, k2 = nxt
        if j + 1 < n_wide:
            nxt = up(j + 1)
        for u in range(MXU_COLS // LANES):
            cols = slice(j * MXU_COLS + u * LANES, j * MXU_COLS + (u + 1) * LANES)
            q_ref[:, cols] = rope(q2[:, u * LANES:(u + 1) * LANES], cq, sq).astype(q_ref.dtype)
            k_ref[:, cols] = (k2[:, u * LANES:(u + 1) * LANES] + kr).astype(k_ref.dtype)


def _mla_in_proj(x, gain, w_down, q_norm, kv_norm, w_uq, w_uk, w_uvt, tabs):
    n, d = x.shape
    tm = MLA_ROW_TILE
    row = lambda width: pl.BlockSpec((tm, width), lambda i: (i, 0))
    full = lambda a: pl.BlockSpec(a.shape, lambda i: (0,) * a.ndim)
    tab = pl.BlockSpec((LANES, tm), lambda i: (0, i))
    bf = jnp.bfloat16
    return pl.pallas_call(
        _mla_in_kernel,
        grid=(n // tm,),
        in_specs=[row(d), full(gain), full(w_down), full(q_norm), full(kv_norm),
                  full(w_uq), full(w_uk), full(w_uvt), tab, tab],
        out_specs=(row(C_HEADS * LANES), row(C_HEADS * LANES),
                   pl.BlockSpec((C_HEADS * C_V_DIM, tm), lambda i: (0, i))),
        out_shape=(jax.ShapeDtypeStruct((n, C_HEADS * LANES), bf),
                   jax.ShapeDtypeStruct((n, C_HEADS * LANES), bf),
                   jax.ShapeDtypeStruct((C_HEADS * C_V_DIM, n), bf)),
        compiler_params=_params("arbitrary"),
        name="mla_in_proj",
    )(x, gain, w_down, q_norm, kv_norm, w_uq, w_uk, w_uvt, *tabs)


def _mla_attn_kernel(q_ref, k_ref, vt_ref, o_ref, m_ref, acc_ref, s0_ref, s1_ref):
    i = pl.program_id(2)
    tq, kc, group = MLA_Q_TILE, MLA_KEY_CHUNK, MLA_HEAD_GROUP // 2
    n_full = (i * tq) // kc
    _softmax_init(m_ref, acc_ref)

    def issue_qk(c, g, s_ref, width):
        base = pl.multiple_of(c * kc, kc)
        for u in range(group):
            cols = slice((g * group + u) * LANES, (g * group + u + 1) * LANES)
            s_ref[u, 0:width, :] = _dot_nt(k_ref[pl.ds(base, width), cols], q_ref[:, cols])

    def consume(c, g, s_ref, last, width):
        base = pl.multiple_of(c * kc, kc)
        if last:
            q_pos = i * tq + lax.broadcasted_iota(jnp.int32, (width, tq), 1)
            k_pos = base + lax.broadcasted_iota(jnp.int32, (width, tq), 0)
            bias_t = jnp.where(k_pos <= q_pos, 0.0, NEG_BIG)
        for u in range(group):
            h = g * group + u
            s_t = s_ref[u, 0:width, :]
            if last:
                s_t = s_t + bias_t
            v_t = vt_ref[h * C_V_DIM:(h + 1) * C_V_DIM, pl.ds(base, width)]
            _softmax_step_t(s_t, v_t, m_ref, acc_ref, h)

    half_last = (i + 1) * tq - n_full * kc <= kc // 2
    _attention_pipeline(n_full, half_last, kc, issue_qk, consume, (s0_ref, s1_ref))
    _softmax_finish(acc_ref, o_ref)


def _mla_attention(q, k, vt, batch, seq):
    n = q.shape[0]
    tq, hg = MLA_Q_TILE, MLA_HEAD_GROUP
    nq = seq // tq
    return pl.pallas_call(
        _mla_attn_kernel,
        grid=(batch, C_HEADS // hg, nq),
        in_specs=[pl.BlockSpec((tq, hg * LANES), lambda b, g, i: (b * nq + i, g)),
                  pl.BlockSpec((seq, hg * LANES), lambda b, g, i: (b, g)),
                  pl.BlockSpec((hg * C_V_DIM, seq), lambda b, g, i: (g, b))],
        out_specs=pl.BlockSpec((tq, hg * C_V_DIM), lambda b, g, i: (b * nq + i, g)),
        out_shape=jax.ShapeDtypeStruct((n, C_HEADS * C_V_DIM), jnp.bfloat16),
        scratch_shapes=[pltpu.VMEM((hg, 1, tq), jnp.float32),
                        pltpu.VMEM((hg, C_V_DIM + SUM_ROWS, tq), jnp.float32),
                        pltpu.VMEM((hg // 2, MLA_KEY_CHUNK, tq), jnp.float32),
                        pltpu.VMEM((hg // 2, MLA_KEY_CHUNK, tq), jnp.float32)],
        compiler_params=_params("arbitrary", "arbitrary", "arbitrary"),
        name="mla_attention",
    )(q, k, vt)


def _rope_tables(positions, rot_dim, lead, period):
    inv_freq = ROPE_THETA ** (-jnp.arange(0, rot_dim, 2, dtype=jnp.float32) / rot_dim)
    ang = inv_freq[:, None] * positions.astype(jnp.float32).reshape(1, -1)
    cos, sin = jnp.cos(ang), jnp.sin(ang)
    half, n = cos.shape
    ones = lambda w: jnp.ones((w, n), jnp.float32)
    zeros = lambda w: jnp.zeros((w, n), jnp.float32)
    tail = period - lead - rot_dim
    c = jnp.concatenate([ones(lead), cos, cos, ones(tail)], axis=0)
    s_next = jnp.concatenate([zeros(lead), -sin, zeros(half + tail)], axis=0)
    s_prev = jnp.concatenate([zeros(lead + half), sin, zeros(tail)], axis=0)
    reps = LANES // period
    return tuple(jnp.tile(t, (reps, 1)) for t in (c, s_next, s_prev))


def _pack_even_w_in(w):
    d = w.shape[0]
    q, k, v, qi, ki, wi, gb, gc, xi = jnp.split(
        w, [512, 1024, 1536, 2560, 2624, 2640, 3152, 3664], axis=1)
    pad = jnp.zeros((d, LANES - IDX_HEADS), w.dtype)
    packed = jnp.concatenate([q, k, qi, ki, ki, wi, pad, gb, gc, xi], axis=1)
    return packed.astype(jnp.bfloat16), v.T.astype(jnp.bfloat16)


def _pad_heads(w, heads, width):
    r = w.shape[0]
    w = w.reshape(r, heads, width)
    return jnp.pad(w, ((0, 0), (0, 0), (0, LANES - width))).reshape(r, heads * LANES)


def _rope_swapped(w_rope):
    half = C_ROPE_DIM // 2
    return jnp.concatenate([w_rope, w_rope[..., half:], w_rope[..., :half]], axis=-1)


def _mla_rope_tables(positions):
    inv_freq = ROPE_THETA ** (-jnp.arange(0, C_ROPE_DIM, 2, dtype=jnp.float32) / C_ROPE_DIM)
    ang = inv_freq[:, None] * positions.astype(jnp.float32).reshape(1, -1)
    cos, sin = jnp.cos(ang), jnp.sin(ang)
    n = cos.shape[1]
    tail = jnp.zeros((LANES - C_NOPE_DIM - C_ROPE_DIM, n), jnp.float32)
    c = jnp.concatenate([jnp.ones((C_NOPE_DIM, n), jnp.float32), cos, cos, tail], axis=0)
    s = jnp.concatenate([jnp.zeros((C_NOPE_DIM, n), jnp.float32), -sin, sin, tail], axis=0)
    return c, s


def kernel(x, positions, norm_mix_pre, norm_mix_post, norm_ffn_pre, norm_ffn_post, even_w_in, even_conv_w, even_w_out, odd_w_dq, odd_q_norm, odd_w_uq, odd_w_dkv, odd_kv_norm, odd_w_ukv, odd_w_o, mlp_w1, mlp_w2):
    batch, seq, d = x.shape
    depth = norm_mix_pre.shape[0]
    bf = jnp.bfloat16
    tabs_a = _rope_tables(positions, A_ROT_DIM, 0, A_HEAD_DIM)
    tabs_c = _mla_rope_tables(positions)
    h = x.reshape(batch * seq, d)
    row = lambda a: a.reshape(1, -1)
    for layer in range(depth):
        j = layer // 2
        g_pre, g_post = row(norm_mix_pre[layer]), row(norm_mix_post[layer])
        if layer % 2 == 0:
            w_packed, w_vt = _pack_even_w_in(even_w_in[j])
            q, k, vt, qi, ki, wi, conv = _even_in_proj(
                h, g_pre, w_packed, w_vt, tabs_a, even_conv_w[j], seq)
            attn = _dsa_attention(q, qi, wi, ki, k, vt, batch, seq)
            w_out = even_w_out[j].astype(bf)
            acts, w_outs = [attn, conv], [w_out[:A_WIDTH], w_out[A_WIDTH:]]
        else:
            w_dkv = odd_w_dkv[j]
            w_down = jnp.concatenate(
                [odd_w_dq[j], jnp.zeros((d, C_NOPE_DIM), w_dkv.dtype),
                 _rope_swapped(w_dkv[:, C_KV_RANK:]), w_dkv[:, :C_KV_RANK]], axis=1)
            w_ukv = odd_w_ukv[j].reshape(C_KV_RANK, C_HEADS, C_NOPE_DIM + C_V_DIM)
            w_uk = _pad_heads(w_ukv[:, :, :C_NOPE_DIM].reshape(C_KV_RANK, -1), C_HEADS, C_NOPE_DIM)
            w_uv = w_ukv[:, :, C_NOPE_DIM:].reshape(C_KV_RANK, -1)
            w_uq = odd_w_uq[j].reshape(C_Q_RANK, C_HEADS, C_NOPE_DIM + C_ROPE_DIM)
            w_uq = jnp.concatenate(
                [w_uq[:, :, :C_NOPE_DIM], _rope_swapped(w_uq[:, :, C_NOPE_DIM:])],
                axis=-1).reshape(C_Q_RANK, C_HEADS * LANES)
            q, k, vt = _mla_in_proj(h, g_pre, w_down.astype(bf), row(odd_q_norm[j]),
                                    row(odd_kv_norm[j]), w_uq.astype(bf), w_uk.astype(bf),
                                    w_uv.T.astype(bf), tabs_c)
            acts, w_outs = [_mla_attention(q, k, vt, batch, seq)], [odd_w_o[j].astype(bf)]
        h = _mix_mlp(acts, w_outs, h, g_post, row(norm_ffn_pre[layer]),
                     row(norm_ffn_post[layer]), mlp_w1[layer].astype(bf),
                     mlp_w2[layer].astype(bf))
    return h.reshape(batch, seq, d)
```

```python
import functools

import jax
import jax.numpy as jnp
from jax import lax
from jax.experimental import pallas as pl
from jax.experimental.pallas import tpu as pltpu

ROPE_THETA = 500000.0
NORM_EPS = 1e-6
LANES = 128
MXU_COLS = 256
NEG_BIG = -1e30
LOG2E = 1.4426950408889634
SUM_ROWS = 16

A_HEADS = 8
A_HEAD_DIM = 64
A_WIDTH = A_HEADS * A_HEAD_DIM
A_ROT_DIM = A_HEAD_DIM // 4
IDX_HEADS = 16
IDX_DIM = 64
TOPK_MAX = 256
B_WIDTH = 512
CONV_WIDTH = 3
C_HEADS = 16
C_NOPE_DIM = 64
C_ROPE_DIM = 32
C_V_DIM = 64
C_Q_RANK = 384
C_KV_RANK = 256

KEY_LOWEST_FINITE = -2139095040
COARSE_SLACK = 0x8000

ROW_TILE = 512
MLA_ROW_TILE = 512
MLP_ROW_TILE = 512
MLP_FF_TILE = 1024
DSA_Q_TILE = 256
DSA_KEY_CHUNK = 512
DSA_COUNT_CHUNK = 256
DSA_COUNT_ROWS = 64
DSA_UNROLL = 4
DSA_SCORE_UNROLL = 2
MLA_UNROLL = 4
MLA_Q_TILE = 256
MLA_KEY_CHUNK = 512
MLA_HEAD_GROUP = 8
VMEM_LIMIT = 56 * 1024 * 1024


def _dot(a, b):
    return jnp.dot(a, b, preferred_element_type=jnp.float32)


def _dot_nt(a, b):
    return lax.dot_general(a, b, (((1,), (1,)), ((), ())), preferred_element_type=jnp.float32)


def _rms(x, gain):
    ms = jnp.mean(x * x, axis=-1, keepdims=True)
    return x * lax.rsqrt(ms + NORM_EPS) * gain


def _rope_tile(x, c, s_next, s_prev, half):
    return (x * c + pltpu.roll(x, LANES - half, axis=1) * s_next
            + pltpu.roll(x, half, axis=1) * s_prev)


def _params(*semantics):
    return pltpu.CompilerParams(dimension_semantics=semantics, vmem_limit_bytes=VMEM_LIMIT)


_EQ, _EK, _EQI, _EKI, _EWI, _EGB, _EGC, _EXI, _EEND = (
    0, 512, 1024, 2048, 2176, 2304, 2816, 3328, 3840)


def _even_in_kernel(seq_tiles, x_ref, g_ref, w_ref, wvt_ref, c_ref, sn_ref, sp_ref, cw_ref,
                    q_ref, k_ref, vt_ref, qi_ref, ki_ref, wi_ref, conv_ref, carry_ref):
    i = pl.program_id(0)
    tm = x_ref.shape[0]
    hn = _rms(x_ref[...], g_ref[...]).astype(jnp.bfloat16)
    c, sn, sp = c_ref[...].T, sn_ref[...].T, sp_ref[...].T
    half = A_ROT_DIM // 2

    def rope_section(start, width, out_ref, scale=None):
        for t in range(width // MXU_COLS):
            y2 = _dot(hn, w_ref[:, start + t * MXU_COLS:start + (t + 1) * MXU_COLS])
            for u in range(MXU_COLS // LANES):
                y = _rope_tile(y2[:, u * LANES:(u + 1) * LANES], c, sn, sp, half)
                if scale is not None:
                    y = y * scale
                lo = t * MXU_COLS + u * LANES
                out_ref[:, lo:lo + LANES] = y.astype(out_ref.dtype)

    rope_section(_EQ, A_WIDTH, q_ref, A_HEAD_DIM ** -0.5 * LOG2E)
    rope_section(_EK, A_WIDTH, k_ref)
    vt_ref[...] = _dot_nt(wvt_ref[...], hn).astype(vt_ref.dtype)
    rope_section(_EQI, IDX_HEADS * IDX_DIM, qi_ref)
    kw = _dot(hn, w_ref[:, _EKI:_EGB])
    ki_ref[...] = _rope_tile(kw[:, :LANES], c, sn, sp, half).astype(ki_ref.dtype)
    wi_ref[...] = kw[:, LANES:] * (IDX_HEADS ** -0.5 * IDX_DIM ** -0.5)

    gate_b = _dot(hn, w_ref[:, _EGB:_EGC])
    u = _dot(hn, w_ref[:, _EGC:_EXI]) * _dot(hn, w_ref[:, _EXI:_EEND])
    seq_start = (i % seq_tiles) == 0
    prev = jnp.where(seq_start, 0.0, carry_ref[...])
    carry_ref[...] = u[tm - 8:, :]
    row = lax.broadcasted_iota(jnp.int32, u.shape, 0)
    u1 = jnp.where(row == 0, prev[7:8, :], pltpu.roll(u, 1, axis=0))
    u2 = pltpu.roll(u, 2, axis=0)
    u2 = jnp.where(row == 0, prev[6:7, :], jnp.where(row == 1, prev[7:8, :], u2))
    y = cw_ref[0:1, :] * u2 + cw_ref[1:2, :] * u1 + cw_ref[2:3, :] * u
    conv_ref[...] = (gate_b * y).astype(conv_ref.dtype)


def _even_in_proj(x, gain, w_packed, w_vt, tabs, conv_w, seq):
    n, d = x.shape
    tm = ROW_TILE
    row = lambda width: pl.BlockSpec((tm, width), lambda i: (i, 0))
    full = lambda a: pl.BlockSpec(a.shape, lambda i: (0,) * a.ndim)
    tab = pl.BlockSpec((LANES, tm), lambda i: (0, i))
    bf = jnp.bfloat16
    out_shape = (
        jax.ShapeDtypeStruct((n, A_WIDTH), bf), jax.ShapeDtypeStruct((n, A_WIDTH), bf),
        jax.ShapeDtypeStruct((A_WIDTH, n), bf), jax.ShapeDtypeStruct((n, IDX_HEADS * IDX_DIM), bf),
        jax.ShapeDtypeStruct((n, LANES), bf), jax.ShapeDtypeStruct((n, LANES), jnp.float32),
        jax.ShapeDtypeStruct((n, B_WIDTH), bf))
    return pl.pallas_call(
        functools.partial(_even_in_kernel, seq // tm),
        grid=(n // tm,),
        in_specs=[row(d), full(gain), full(w_packed), full(w_vt), tab, tab, tab,
                  full(conv_w)],
        out_specs=(row(A_WIDTH), row(A_WIDTH), pl.BlockSpec((A_WIDTH, tm), lambda i: (0, i)),
                   row(IDX_HEADS * IDX_DIM), row(LANES), row(LANES), row(B_WIDTH)),
        out_shape=out_shape,
        scratch_shapes=[pltpu.VMEM((8, B_WIDTH), jnp.float32)],
        compiler_params=_params("arbitrary"),
        name="even_in_proj",
    )(x, gain, w_packed, w_vt, *tabs, conv_w)


def _key_to_float(key):
    bits = key ^ ((key >> 31) & jnp.int32(0x7FFFFFFF))
    return lax.bitcast_convert_type(bits, jnp.float32)


def _softmax_step_t(s_t, v_t, m_ref, acc_ref, h):
    kc = s_t.shape[0]
    m_prev = m_ref[h]
    m_new = jnp.maximum(m_prev, jnp.max(s_t, axis=0, keepdims=True))
    alpha = jnp.exp2(m_prev - m_new)
    p_t = jnp.exp2(s_t - m_new).astype(jnp.bfloat16)
    v_aug = jnp.concatenate([v_t, jnp.ones((SUM_ROWS, kc), jnp.bfloat16)], axis=0)
    acc_ref[h] = alpha * acc_ref[h] + _dot(v_aug, p_t)
    m_ref[h] = m_new


def _softmax_init(m_ref, acc_ref):
    m_ref[...] = jnp.full(m_ref.shape, NEG_BIG, jnp.float32)
    acc_ref[...] = jnp.zeros(acc_ref.shape, jnp.float32)


def _softmax_finish(acc_ref, o_ref):
    heads, rows, _ = acc_ref.shape
    dv = rows - SUM_ROWS
    o_t = jnp.concatenate(
        [acc_ref[h, 0:dv, :] / acc_ref[h, dv:dv + 1, :] for h in range(heads)], axis=0)
    o_ref[...] = o_t.T.astype(o_ref.dtype)


def _unrolled_loop(n, unroll, step):
    def steps(first, count):
        for t in range(count):
            step(first + t)

    def unrolled(p, carry):
        steps(unroll * p, unroll)
        return carry

    lax.fori_loop(0, n // unroll, unrolled, 0)
    done = (n // unroll) * unroll
    size = unroll // 2
    while size >= 1:
        has = (n & size) != 0
        pl.when(has)(functools.partial(steps, done, size))
        done = done + jnp.where(has, size, 0)
        size //= 2


def _attention_pipeline(n_last, half_last, kc, unroll, issue_qk, consume, s_refs):
    s0, s1 = s_refs
    issue_qk(0, 0, s0, kc)

    def step(c):
        issue_qk(c, 1, s1, kc)
        consume(c, 0, s0, False, kc)
        issue_qk(c + 1, 0, s0, kc)
        consume(c, 1, s1, False, kc)

    _unrolled_loop(n_last, unroll, step)

    def last_chunk(width):
        issue_qk(n_last, 1, s1, width)
        consume(n_last, 0, s0, True, width)
        consume(n_last, 1, s1, True, width)

    pl.when(half_last)(lambda: last_chunk(kc // 2))
    pl.when(jnp.logical_not(half_last))(lambda: last_chunk(kc))


def _dsa_kernel(top_k, q_ref, qi_ref, wi_ref, ki_ref, k_ref, vt_ref, o_ref,
                sc_ref, hb_ref, qm_ref, qp_ref, m_ref, acc_ref, s0_ref, s1_ref, bias_ref):
    i = pl.program_id(1)
    tq, kc, ks = DSA_Q_TILE, DSA_KEY_CHUNK, DSA_COUNT_CHUNK
    n_chunks = (i * tq) // kc + 1
    n_count = (i * tq) // ks + 1
    low = lax.broadcasted_iota(jnp.int32, (tq, LANES), 1) < IDX_DIM

    for j in range(IDX_HEADS // 2):
        t = qi_ref[:, j * LANES:(j + 1) * LANES].astype(jnp.float32)
        qm_ref[j, 0:tq, :] = jnp.where(low, t, 0.0).astype(jnp.bfloat16)
        qm_ref[j, tq:2 * tq, :] = jnp.where(low, 0.0, t).astype(jnp.bfloat16)
    for j in range(A_HEADS // 2):
        t = q_ref[:, j * LANES:(j + 1) * LANES].astype(jnp.float32)
        qp_ref[2 * j] = jnp.where(low, t, 0.0).astype(jnp.bfloat16)
        qp_ref[2 * j + 1] = jnp.where(low, 0.0, t).astype(jnp.bfloat16)
    w_t = wi_ref[...].T

    q_pos = i * tq + lax.broadcasted_iota(jnp.int32, (kc, tq), 1)
    k_off = lax.broadcasted_iota(jnp.int32, (kc, tq), 0)

    def score_chunk(c):
        base = pl.multiple_of(c * kc, kc)
        kch = ki_ref[pl.ds(base, kc), :]
        acc = jnp.zeros((kc, tq), jnp.float32)
        for j in range(IDX_HEADS // 2):
            r = _dot_nt(kch, qm_ref[j])
            acc = acc + jnp.maximum(r[:, :tq], 0.0) * w_t[2 * j:2 * j + 1, :]
            acc = acc + jnp.maximum(r[:, tq:], 0.0) * w_t[2 * j + 1:2 * j + 2, :]
        masked = jnp.where(base + k_off <= q_pos, acc, -jnp.inf)
        sc_ref[pl.ds(base, kc), :] = masked
        hb_ref[pl.ds(base, kc), :] = masked.astype(jnp.bfloat16)

    _unrolled_loop(n_chunks, DSA_SCORE_UNROLL, score_chunk)

    part = DSA_COUNT_ROWS

    def count(src_ref, pred):
        one = jnp.ones((), src_ref.dtype)

        def count_chunk(c, cnt):
            base = pl.multiple_of(c * ks, ks)
            for r in range(ks // part):
                row0 = base + r * part
                cnt = jnp.where(pred(src_ref[pl.ds(row0, part), :], row0), cnt + one, cnt)
            return cnt

        cnt = lax.fori_loop(0, n_count, count_chunk, jnp.zeros((part, tq), src_ref.dtype))
        return jnp.sum(cnt.astype(jnp.float32), axis=0, keepdims=True)

    def coarse(step, lo):
        cand = lo + jnp.left_shift(jnp.int32(1), 31 - step)
        cand_b = jnp.broadcast_to(_key_to_float(cand).astype(jnp.bfloat16), (part, tq))
        total = count(hb_ref, lambda blk, row0: blk >= cand_b)
        return jnp.where(total >= top_k, cand, lo)

    k_coarse = lax.fori_loop(
        0, 16, coarse, jnp.full((1, tq), jnp.iinfo(jnp.int32).min, jnp.int32))

    lo0 = jnp.maximum(k_coarse, KEY_LOWEST_FINITE + COARSE_SLACK) - COARSE_SLACK
    hi0 = jnp.maximum(k_coarse + 2 * COARSE_SLACK, lo0)

    def fine(step, carry):
        lo, hi = carry
        mid = lo + jnp.right_shift(hi - lo + 1, 1)
        mid_f = _key_to_float(mid)
        take = count(sc_ref, lambda blk, row0: blk >= mid_f) >= top_k
        return jnp.where(take, mid, lo), jnp.where(take, hi, mid - 1)

    lo, _ = lax.fori_loop(0, (3 * COARSE_SLACK).bit_length(), fine, (lo0, hi0))
    thr = _key_to_float(lo)
    n_ge = count(sc_ref, lambda blk, row0: blk >= thr)

    @pl.when(jnp.max(n_ge) > top_k)
    def _():
        n_gt = count(sc_ref, lambda blk, row0: blk > thr)
        need = top_k - n_gt
        row = lax.broadcasted_iota(jnp.int32, (part, tq), 0)
        index_bits = (sc_ref.shape[0] - 1).bit_length()

        def index_search(step, cut):
            cand = cut + jnp.left_shift(jnp.int32(1), index_bits - 1 - step)
            below = count(sc_ref, lambda blk, row0: (blk == thr) & (row0 + row < cand))
            return jnp.where(below < need, cand, cut)

        cut = lax.fori_loop(0, index_bits, index_search, jnp.zeros((1, tq), jnp.int32))

        def drop_chunk(c, carry):
            base = pl.multiple_of(c * ks, ks)
            blk = sc_ref[pl.ds(base, ks), :]
            pos = base + lax.broadcasted_iota(jnp.int32, (ks, tq), 0)
            sc_ref[pl.ds(base, ks), :] = jnp.where((blk == thr) & (pos > cut), -jnp.inf, blk)
            return carry

        lax.fori_loop(0, n_count, drop_chunk, 0)

    _softmax_init(m_ref, acc_ref)

    group = A_HEADS // 2

    def issue_qk(c, g, s_ref, width):
        base = pl.multiple_of(c * kc, kc)
        for u in range(group):
            h = g * group + u
            cols = slice((h // 2) * LANES, (h // 2 + 1) * LANES)
            s_ref[u, 0:width, :] = _dot_nt(k_ref[pl.ds(base, width), cols], qp_ref[h])

    def consume(c, g, s_ref, last, width):
        base = pl.multiple_of(c * kc, kc)
        if g == 0:
            bias_ref[0:width, :] = jnp.where(
                sc_ref[pl.ds(base, width), :] >= thr, 0.0, NEG_BIG)
        for u in range(group):
            h = g * group + u
            v_t = vt_ref[h * A_HEAD_DIM:(h + 1) * A_HEAD_DIM, pl.ds(base, width)]
            s_t = s_ref[u, 0:width, :] + bias_ref[0:width, :]
            _softmax_step_t(s_t, v_t, m_ref, acc_ref, h)

    half_last = (i + 1) * tq - (n_chunks - 1) * kc <= kc // 2
    _attention_pipeline(n_chunks - 1, half_last, kc, DSA_UNROLL, issue_qk, consume,
                        (s0_ref, s1_ref))
    _softmax_finish(acc_ref, o_ref)


def _dsa_attention(q, qi, wi, ki, k, vt, batch, seq):
    n = q.shape[0]
    tq = DSA_Q_TILE
    nq = seq // tq
    top_k = min(TOPK_MAX, seq // 4)
    qrow = lambda width: pl.BlockSpec((tq, width), lambda b, i: (b * nq + i, 0))
    per_batch = lambda width: pl.BlockSpec((seq, width), lambda b, i: (b, 0))
    return pl.pallas_call(
        functools.partial(_dsa_kernel, top_k),
        grid=(batch, nq),
        in_specs=[qrow(A_WIDTH), qrow(IDX_HEADS * IDX_DIM), qrow(LANES),
                  per_batch(LANES), per_batch(A_WIDTH),
                  pl.BlockSpec((A_WIDTH, seq), lambda b, i: (0, b))],
        out_specs=qrow(A_WIDTH),
        out_shape=jax.ShapeDtypeStruct((n, A_WIDTH), jnp.bfloat16),
        scratch_shapes=[
            pltpu.VMEM((seq, tq), jnp.float32),
            pltpu.VMEM((seq, tq), jnp.bfloat16),
            pltpu.VMEM((IDX_HEADS // 2, 2 * tq, LANES), jnp.bfloat16),
            pltpu.VMEM((A_HEADS, tq, LANES), jnp.bfloat16),
            pltpu.VMEM((A_HEADS, 1, tq), jnp.float32),
            pltpu.VMEM((A_HEADS, A_HEAD_DIM + SUM_ROWS, tq), jnp.float32),
            pltpu.VMEM((A_HEADS // 2, DSA_KEY_CHUNK, tq), jnp.float32),
            pltpu.VMEM((A_HEADS // 2, DSA_KEY_CHUNK, tq), jnp.float32),
            pltpu.VMEM((DSA_KEY_CHUNK, tq), jnp.float32),
        ],
        compiler_params=_params("arbitrary", "arbitrary"),
        name="dsa_attention",
    )(q, qi, wi, ki, k, vt)


def _mix_mlp_kernel(n_in, *refs):
    a_refs, wo_refs = refs[:n_in], refs[n_in:2 * n_in]
    (x_ref, gmix_ref, gpre_ref, gpost_ref, w1_ref, w2_ref, o_ref,
     x1_ref, hn_ref, acc_ref) = refs[2 * n_in:]
    f = pl.program_id(1)

    @pl.when(f == 0)
    def _():
        y = _dot(a_refs[0][...], wo_refs[0][...])
        for a_ref, wo_ref in zip(a_refs[1:], wo_refs[1:]):
            y = y + _dot(a_ref[...], wo_ref[...])
        x1 = x_ref[...] + _rms(y, gmix_ref[...])
        x1_ref[...] = x1
        hn_ref[...] = _rms(x1, gpre_ref[...]).astype(jnp.bfloat16)
        acc_ref[...] = jnp.zeros(acc_ref.shape, jnp.float32)

    h = jnp.maximum(_dot(hn_ref[...], w1_ref[...]), 0.0)
    acc_ref[...] += _dot((h * h).astype(jnp.bfloat16), w2_ref[...])

    @pl.when(f == pl.num_programs(1) - 1)
    def _():
        o_ref[...] = x1_ref[...] + _rms(acc_ref[...], gpost_ref[...])


def _mix_mlp(acts, w_outs, x, g_mix, gpre, gpost, w1, w2):
    n, d = x.shape
    ff = w1.shape[1]
    tm, tf = MLP_ROW_TILE, MLP_FF_TILE
    row = lambda a: pl.BlockSpec((tm, a.shape[1]), lambda i, f: (i, 0))
    full = lambda a: pl.BlockSpec(a.shape, lambda i, f: (0,) * a.ndim)
    return pl.pallas_call(
        functools.partial(_mix_mlp_kernel, len(acts)),
        grid=(n // tm, ff // tf),
        in_specs=[row(a) for a in acts] + [full(w) for w in w_outs]
        + [row(x), full(g_mix), full(gpre), full(gpost),
           pl.BlockSpec((d, tf), lambda i, f: (0, f)),
           pl.BlockSpec((tf, d), lambda i, f: (f, 0))],
        out_specs=row(x),
        out_shape=jax.ShapeDtypeStruct((n, d), jnp.float32),
        scratch_shapes=[pltpu.VMEM((tm, d), jnp.float32), pltpu.VMEM((tm, d), jnp.bfloat16),
                        pltpu.VMEM((tm, d), jnp.float32)],
        compiler_params=_params("arbitrary", "arbitrary"),
        name="mix_mlp",
    )(*acts, *w_outs, x, g_mix, gpre, gpost, w1, w2)


_DKR, _DKV, _DEND = C_Q_RANK, C_Q_RANK + LANES, C_Q_RANK + LANES + C_KV_RANK


def _mla_in_kernel(x_ref, g_ref, wd_ref, qn_ref, kvn_ref, wuq_ref, wuk_ref, wuvt_ref,
                   c_ref, s_ref, q_ref, k_ref, vt_ref):
    hn = _rms(x_ref[...], g_ref[...]).astype(jnp.bfloat16)
    c, s = c_ref[...].T, s_ref[...].T
    scale = (C_NOPE_DIM + C_ROPE_DIM) ** -0.5 * LOG2E
    cq, sq = c * scale, s * scale

    def rope(y, ct, st):
        return y * ct + pltpu.roll(y, LANES - C_ROPE_DIM, axis=1) * st

    qr = _dot(hn, wd_ref[:, 0:_DKV])
    qa = _rms(qr[:, :_DKR], qn_ref[...]).astype(jnp.bfloat16)
    ckv = _rms(_dot(hn, wd_ref[:, _DKV:_DEND]), kvn_ref[...]).astype(jnp.bfloat16)
    kr = rope(qr[:, _DKR:], c, s)
    vt_ref[...] = _dot_nt(wuvt_ref[...], ckv).astype(vt_ref.dtype)

    n_wide = C_HEADS * LANES // MXU_COLS

    def up(j):
        cols = slice(j * MXU_COLS, (j + 1) * MXU_COLS)
        return _dot(qa, wuq_ref[:, cols]), _dot(ckv, wuk_ref[:, cols])

    nxt = up(0)
    for j in range(n_wide):
        q2, k2 = nxt
        if j + 1 < n_wide:
            nxt = up(j + 1)
        for u in range(MXU_COLS // LANES):
            cols = slice(j * MXU_COLS + u * LANES, j * MXU_COLS + (u + 1) * LANES)
            q_ref[:, cols] = rope(q2[:, u * LANES:(u + 1) * LANES], cq, sq).astype(q_ref.dtype)
            k_ref[:, cols] = (k2[:, u * LANES:(u + 1) * LANES] + kr).astype(k_ref.dtype)


def _mla_in_proj(x, gain, w_down, q_norm, kv_norm, w_uq, w_uk, w_uvt, tabs):
    n, d = x.shape
    tm = MLA_ROW_TILE
    row = lambda width: pl.BlockSpec((tm, width), lambda i: (i, 0))
    full = lambda a: pl.BlockSpec(a.shape, lambda i: (0,) * a.ndim)
    tab = pl.BlockSpec((LANES, tm), lambda i: (0, i))
    bf = jnp.bfloat16
    return pl.pallas_call(
        _mla_in_kernel,
        grid=(n // tm,),
        in_specs=[row(d), full(gain), full(w_down), full(q_norm), full(kv_norm),
                  full(w_uq), full(w_uk), full(w_uvt), tab, tab],
        out_specs=(row(C_HEADS * LANES), row(C_HEADS * LANES),
                   pl.BlockSpec((C_HEADS * C_V_DIM, tm), lambda i: (0, i))),
        out_shape=(jax.ShapeDtypeStruct((n, C_HEADS * LANES), bf),
                   jax.ShapeDtypeStruct((n, C_HEADS * LANES), bf),
                   jax.ShapeDtypeStruct((C_HEADS * C_V_DIM, n), bf)),
        compiler_params=_params("arbitrary"),
        name="mla_in_proj",
    )(x, gain, w_down, q_norm, kv_norm, w_uq, w_uk, w_uvt, *tabs)


def _mla_attn_kernel(q_ref, k_ref, vt_ref, o_ref, m_ref, acc_ref, s0_ref, s1_ref):
    i = pl.program_id(2)
    tq, kc, group = MLA_Q_TILE, MLA_KEY_CHUNK, MLA_HEAD_GROUP // 2
    n_full = (i * tq) // kc
    _softmax_init(m_ref, acc_ref)

    def issue_qk(c, g, s_ref, width):
        base = pl.multiple_of(c * kc, kc)
        for u in range(group):
            cols = slice((g * group + u) * LANES, (g * group + u + 1) * LANES)
            s_ref[u, 0:width, :] = _dot_nt(k_ref[pl.ds(base, width), cols], q_ref[:, cols])

    def consume(c, g, s_ref, last, width):
        base = pl.multiple_of(c * kc, kc)
        if last:
            q_pos = i * tq + lax.broadcasted_iota(jnp.int32, (width, tq), 1)
            k_pos = base + lax.broadcasted_iota(jnp.int32, (width, tq), 0)
            bias_t = jnp.where(k_pos <= q_pos, 0.0, NEG_BIG)
        for u in range(group):
            h = g * group + u
            s_t = s_ref[u, 0:width, :]
            if last:
                s_t = s_t + bias_t
            v_t = vt_ref[h * C_V_DIM:(h + 1) * C_V_DIM, pl.ds(base, width)]
            _softmax_step_t(s_t, v_t, m_ref, acc_ref, h)

    half_last = (i + 1) * tq - n_full * kc <= kc // 2
    _attention_pipeline(n_full, half_last, kc, MLA_UNROLL, issue_qk, consume, (s0_ref, s1_ref))
    _softmax_finish(acc_ref, o_ref)


def _mla_attention(q, k, vt, batch, seq):
    n = q.shape[0]
    tq, hg = MLA_Q_TILE, MLA_HEAD_GROUP
    nq = seq // tq
    return pl.pallas_call(
        _mla_attn_kernel,
        grid=(batch, C_HEADS // hg, nq),
        in_specs=[pl.BlockSpec((tq, hg * LANES), lambda b, g, i: (b * nq + i, g)),
                  pl.BlockSpec((seq, hg * LANES), lambda b, g, i: (b, g)),
                  pl.BlockSpec((hg * C_V_DIM, seq), lambda b, g, i: (g, b))],
        out_specs=pl.BlockSpec((tq, hg * C_V_DIM), lambda b, g, i: (b * nq + i, g)),
        out_shape=jax.ShapeDtypeStruct((n, C_HEADS * C_V_DIM), jnp.bfloat16),
        scratch_shapes=[pltpu.VMEM((hg, 1, tq), jnp.float32),
                        pltpu.VMEM((hg, C_V_DIM + SUM_ROWS, tq), jnp.float32),
                        pltpu.VMEM((hg // 2, MLA_KEY_CHUNK, tq), jnp.float32),
                        pltpu.VMEM((hg // 2, MLA_KEY_CHUNK, tq), jnp.float32)],
        compiler_params=_params("arbitrary", "arbitrary", "arbitrary"),
        name="mla_attention",
    )(q, k, vt)


def _rope_tables(positions, rot_dim, lead, period):
    inv_freq = ROPE_THETA ** (-jnp.arange(0, rot_dim, 2, dtype=jnp.float32) / rot_dim)
    ang = inv_freq[:, None] * positions.astype(jnp.float32).reshape(1, -1)
    cos, sin = jnp.cos(ang), jnp.sin(ang)
    half, n = cos.shape
    ones = lambda w: jnp.ones((w, n), jnp.float32)
    zeros = lambda w: jnp.zeros((w, n), jnp.float32)
    tail = period - lead - rot_dim
    c = jnp.concatenate([ones(lead), cos, cos, ones(tail)], axis=0)
    s_next = jnp.concatenate([zeros(lead), -sin, zeros(half + tail)], axis=0)
    s_prev = jnp.concatenate([zeros(lead + half), sin, zeros(tail)], axis=0)
    reps = LANES // period
    return tuple(jnp.tile(t, (reps, 1)) for t in (c, s_next, s_prev))


def _pack_even_w_in(w):
    d = w.shape[0]
    q, k, v, qi, ki, wi, gb, gc, xi = jnp.split(
        w, [512, 1024, 1536, 2560, 2624, 2640, 3152, 3664], axis=1)
    pad = jnp.zeros((d, LANES - IDX_HEADS), w.dtype)
    packed = jnp.concatenate([q, k, qi, ki, ki, wi, pad, gb, gc, xi], axis=1)
    return packed.astype(jnp.bfloat16), v.T.astype(jnp.bfloat16)


def _pad_heads(w, heads, width):
    r = w.shape[0]
    w = w.reshape(r, heads, width)
    return jnp.pad(w, ((0, 0), (0, 0), (0, LANES - width))).reshape(r, heads * LANES)


def _rope_swapped(w_rope):
    half = C_ROPE_DIM // 2
    return jnp.concatenate([w_rope, w_rope[..., half:], w_rope[..., :half]], axis=-1)


def _mla_rope_tables(positions):
    inv_freq = ROPE_THETA ** (-jnp.arange(0, C_ROPE_DIM, 2, dtype=jnp.float32) / C_ROPE_DIM)
    ang = inv_freq[:, None] * positions.astype(jnp.float32).reshape(1, -1)
    cos, sin = jnp.cos(ang), jnp.sin(ang)
    n = cos.shape[1]
    tail = jnp.zeros((LANES - C_NOPE_DIM - C_ROPE_DIM, n), jnp.float32)
    c = jnp.concatenate([jnp.ones((C_NOPE_DIM, n), jnp.float32), cos, cos, tail], axis=0)
    s = jnp.concatenate([jnp.zeros((C_NOPE_DIM, n), jnp.float32), -sin, sin, tail], axis=0)
    return c, s


def kernel(x, positions, norm_mix_pre, norm_mix_post, norm_ffn_pre, norm_ffn_post, even_w_in, even_conv_w, even_w_out, odd_w_dq, odd_q_norm, odd_w_uq, odd_w_dkv, odd_kv_norm, odd_w_ukv, odd_w_o, mlp_w1, mlp_w2):
    batch, seq, d = x.shape
    depth = norm_mix_pre.shape[0]
    bf = jnp.bfloat16
    tabs_a = _rope_tables(positions, A_ROT_DIM, 0, A_HEAD_DIM)
    tabs_c = _mla_rope_tables(positions)
    h = x.reshape(batch * seq, d)
    row = lambda a: a.reshape(1, -1)
    for layer in range(depth):
        j = layer // 2
        g_pre, g_post = row(norm_mix_pre[layer]), row(norm_mix_post[layer])
        if layer % 2 == 0:
            w_packed, w_vt = _pack_even_w_in(even_w_in[j])
            q, k, vt, qi, ki, wi, conv = _even_in_proj(
                h, g_pre, w_packed, w_vt, tabs_a, even_conv_w[j], seq)
            attn = _dsa_attention(q, qi, wi, ki, k, vt, batch, seq)
            w_out = even_w_out[j].astype(bf)
            acts, w_outs = [attn, conv], [w_out[:A_WIDTH], w_out[A_WIDTH:]]
        else:
            w_dkv = odd_w_dkv[j]
            w_down = jnp.concatenate(
                [odd_w_dq[j], jnp.zeros((d, C_NOPE_DIM), w_dkv.dtype),
                 _rope_swapped(w_dkv[:, C_KV_RANK:]), w_dkv[:, :C_KV_RANK]], axis=1)
            w_ukv = odd_w_ukv[j].reshape(C_KV_RANK, C_HEADS, C_NOPE_DIM + C_V_DIM)
            w_uk = _pad_heads(w_ukv[:, :, :C_NOPE_DIM].reshape(C_KV_RANK, -1), C_HEADS, C_NOPE_DIM)
            w_uv = w_ukv[:, :, C_NOPE_DIM:].reshape(C_KV_RANK, -1)
            w_uq = odd_w_uq[j].reshape(C_Q_RANK, C_HEADS, C_NOPE_DIM + C_ROPE_DIM)
            w_uq = jnp.concatenate(
                [w_uq[:, :, :C_NOPE_DIM], _rope_swapped(w_uq[:, :, C_NOPE_DIM:])],
                axis=-1).reshape(C_Q_RANK, C_HEADS * LANES)
            q, k, vt = _mla_in_proj(h, g_pre, w_down.astype(bf), row(odd_q_norm[j]),
                                    row(odd_kv_norm[j]), w_uq.astype(bf), w_uk.astype(bf),
                                    w_uv.T.astype(bf), tabs_c)
            acts, w_outs = [_mla_attention(q, k, vt, batch, seq)], [odd_w_o[j].astype(bf)]
        h = _mix_mlp(acts, w_outs, h, g_post, row(norm_ffn_pre[layer]),
                     row(norm_ffn_post[layer]), mlp_w1[layer].astype(bf),
                     mlp_w2[layer].astype(bf))
    return h.reshape(batch, seq, d)
```

```python
import functools

import jax
import jax.numpy as jnp
from jax import lax
from jax.experimental import pallas as pl
from jax.experimental.pallas import tpu as pltpu

ROPE_THETA = 500000.0
NORM_EPS = 1e-6
LANES = 128
MXU_COLS = 256
NEG_BIG = -1e30
LOG2E = 1.4426950408889634
SUM_ROWS = 16

A_HEADS = 8
A_HEAD_DIM = 64
A_WIDTH = A_HEADS * A_HEAD_DIM
A_ROT_DIM = A_HEAD_DIM // 4
IDX_HEADS = 16
IDX_DIM = 64
TOPK_MAX = 256
B_WIDTH = 512
CONV_WIDTH = 3
C_HEADS = 16
C_NOPE_DIM = 64
C_ROPE_DIM = 32
C_V_DIM = 64
C_Q_RANK = 384
C_KV_RANK = 256

KEY_LOWEST_FINITE = -2139095040
COARSE_SLACK = 0x8000

ROW_TILE = 512
MLA_ROW_TILE = 512
MLP_ROW_TILE = 512
MLP_FF_TILE = 1024
DSA_Q_TILE = 256
DSA_KEY_CHUNK = 512
DSA_COUNT_CHUNK = 256
DSA_COUNT_ROWS = 64
DSA_UNROLL = 4
DSA_SCORE_UNROLL = 2
MLA_UNROLL = 4
MLA_Q_TILE = 256
MLA_KEY_CHUNK = 512
MLA_HEAD_GROUP = 8
VMEM_LIMIT = 56 * 1024 * 1024


def _dot(a, b):
    return jnp.dot(a, b, preferred_element_type=jnp.float32)


def _dot_nt(a, b):
    return lax.dot_general(a, b, (((1,), (1,)), ((), ())), preferred_element_type=jnp.float32)


def _rms(x, gain):
    ms = jnp.mean(x * x, axis=-1, keepdims=True)
    return x * lax.rsqrt(ms + NORM_EPS) * gain


def _rope_tile(x, c, s_next, s_prev, half):
    return (x * c + pltpu.roll(x, LANES - half, axis=1) * s_next
            + pltpu.roll(x, half, axis=1) * s_prev)


def _params(*semantics):
    return pltpu.CompilerParams(dimension_semantics=semantics, vmem_limit_bytes=VMEM_LIMIT)


_EQ, _EK, _EQI, _EKI, _EWI, _EGB, _EGC, _EXI, _EEND = (
    0, 512, 1024, 2048, 2176, 2304, 2816, 3328, 3840)


def _even_in_kernel(seq_tiles, x_ref, g_ref, w_ref, wvt_ref, c_ref, sn_ref, sp_ref, cw_ref,
                    q_ref, k_ref, vt_ref, qi_ref, ki_ref, wi_ref, conv_ref, carry_ref):
    i = pl.program_id(0)
    tm = x_ref.shape[0]
    hn = _rms(x_ref[...], g_ref[...]).astype(jnp.bfloat16)
    c, sn, sp = c_ref[...].T, sn_ref[...].T, sp_ref[...].T
    half = A_ROT_DIM // 2

    def rope_section(start, width, out_ref, scale=None):
        for t in range(width // MXU_COLS):
            y2 = _dot(hn, w_ref[:, start + t * MXU_COLS:start + (t + 1) * MXU_COLS])
            for u in range(MXU_COLS // LANES):
                y = _rope_tile(y2[:, u * LANES:(u + 1) * LANES], c, sn, sp, half)
                if scale is not None:
                    y = y * scale
                lo = t * MXU_COLS + u * LANES
                out_ref[:, lo:lo + LANES] = y.astype(out_ref.dtype)

    rope_section(_EQ, A_WIDTH, q_ref, A_HEAD_DIM ** -0.5 * LOG2E)
    rope_section(_EK, A_WIDTH, k_ref)
    vt_ref[...] = _dot_nt(wvt_ref[...], hn).astype(vt_ref.dtype)
    rope_section(_EQI, IDX_HEADS * IDX_DIM, qi_ref)
    kw = _dot(hn, w_ref[:, _EKI:_EGB])
    ki_ref[...] = _rope_tile(kw[:, :LANES], c, sn, sp, half).astype(ki_ref.dtype)
    wi_ref[...] = kw[:, LANES:] * (IDX_HEADS ** -0.5 * IDX_DIM ** -0.5)

    gate_b = _dot(hn, w_ref[:, _EGB:_EGC])
    u = _dot(hn, w_ref[:, _EGC:_EXI]) * _dot(hn, w_ref[:, _EXI:_EEND])
    seq_start = (i % seq_tiles) == 0
    prev = jnp.where(seq_start, 0.0, carry_ref[...])
    carry_ref[...] = u[tm - 8:, :]
    row = lax.broadcasted_iota(jnp.int32, u.shape, 0)
    u1 = jnp.where(row == 0, prev[7:8, :], pltpu.roll(u, 1, axis=0))
    u2 = pltpu.roll(u, 2, axis=0)
    u2 = jnp.where(row == 0, prev[6:7, :], jnp.where(row == 1, prev[7:8, :], u2))
    y = cw_ref[0:1, :] * u2 + cw_ref[1:2, :] * u1 + cw_ref[2:3, :] * u
    conv_ref[...] = (gate_b * y).astype(conv_ref.dtype)


def _even_in_proj(x, gain, w_packed, w_vt, tabs, conv_w, seq):
    n, d = x.shape
    tm = ROW_TILE
    row = lambda width: pl.BlockSpec((tm, width), lambda i: (i, 0))
    full = lambda a: pl.BlockSpec(a.shape, lambda i: (0,) * a.ndim)
    tab = pl.BlockSpec((LANES, tm), lambda i: (0, i))
    bf = jnp.bfloat16
    out_shape = (
        jax.ShapeDtypeStruct((n, A_WIDTH), bf), jax.ShapeDtypeStruct((n, A_WIDTH), bf),
        jax.ShapeDtypeStruct((A_WIDTH, n), bf), jax.ShapeDtypeStruct((n, IDX_HEADS * IDX_DIM), bf),
        jax.ShapeDtypeStruct((n, LANES), bf), jax.ShapeDtypeStruct((n, LANES), jnp.float32),
        jax.ShapeDtypeStruct((n, B_WIDTH), bf))
    return pl.pallas_call(
        functools.partial(_even_in_kernel, seq // tm),
        grid=(n // tm,),
        in_specs=[row(d), full(gain), full(w_packed), full(w_vt), tab, tab, tab,
                  full(conv_w)],
        out_specs=(row(A_WIDTH), row(A_WIDTH), pl.BlockSpec((A_WIDTH, tm), lambda i: (0, i)),
                   row(IDX_HEADS * IDX_DIM), row(LANES), row(LANES), row(B_WIDTH)),
        out_shape=out_shape,
        scratch_shapes=[pltpu.VMEM((8, B_WIDTH), jnp.float32)],
        compiler_params=_params("arbitrary"),
        name="even_in_proj",
    )(x, gain, w_packed, w_vt, *tabs, conv_w)


def _key_to_float(key):
    bits = key ^ ((key >> 31) & jnp.int32(0x7FFFFFFF))
    return lax.bitcast_convert_type(bits, jnp.float32)


def _softmax_step_t(s_t, v_t, m_ref, acc_ref, h):
    kc = s_t.shape[0]
    m_prev = m_ref[h]
    m_new = jnp.maximum(m_prev, jnp.max(s_t, axis=0, keepdims=True))
    alpha = jnp.exp2(m_prev - m_new)
    p_t = jnp.exp2(s_t - m_new).astype(jnp.bfloat16)
    v_aug = jnp.concatenate([v_t, jnp.ones((SUM_ROWS, kc), jnp.bfloat16)], axis=0)
    acc_ref[h] = alpha * acc_ref[h] + _dot(v_aug, p_t)
    m_ref[h] = m_new


def _softmax_init(m_ref, acc_ref):
    m_ref[...] = jnp.full(m_ref.shape, NEG_BIG, jnp.float32)
    acc_ref[...] = jnp.zeros(acc_ref.shape, jnp.float32)


def _softmax_finish(acc_ref, o_ref):
    heads, rows, _ = acc_ref.shape
    dv = rows - SUM_ROWS
    o_t = jnp.concatenate(
        [acc_ref[h, 0:dv, :] / acc_ref[h, dv:dv + 1, :] for h in range(heads)], axis=0)
    o_ref[...] = o_t.T.astype(o_ref.dtype)


def _unrolled_loop(n, unroll, step):
    def steps(first, count):
        for t in range(count):
            step(first + t)

    def unrolled(p, carry):
        steps(unroll * p, unroll)
        return carry

    lax.fori_loop(0, n // unroll, unrolled, 0)
    done = (n // unroll) * unroll
    size = unroll // 2
    while size >= 1:
        has = (n & size) != 0
        pl.when(has)(functools.partial(steps, done, size))
        done = done + jnp.where(has, size, 0)
        size //= 2


def _attention_pipeline(n_last, half_last, first_tile, kc, unroll, issue_qk, consume,
                        issue_next_tile, s_refs):
    s0, s1 = s_refs
    pl.when(first_tile)(lambda: issue_qk(0, 0, s0, kc))

    def step(c):
        issue_qk(c, 1, s1, kc)
        consume(c, 0, s0, False, kc)
        issue_qk(c + 1, 0, s0, kc)
        consume(c, 1, s1, False, kc)

    _unrolled_loop(n_last, unroll, step)

    def last_chunk(width):
        issue_qk(n_last, 1, s1, width)
        consume(n_last, 0, s0, True, width)
        issue_next_tile(s0)
        consume(n_last, 1, s1, True, width)

    pl.when(half_last)(lambda: last_chunk(kc // 2))
    pl.when(jnp.logical_not(half_last))(lambda: last_chunk(kc))


def _dsa_kernel(top_k, q_ref, qn_ref, qi_ref, wi_ref, ki_ref, k_ref, vt_ref, o_ref,
                sc_ref, hb_ref, qm_ref, qp_ref, m_ref, acc_ref, s0_ref, s1_ref, bias_ref):
    i = pl.program_id(1)
    tq, kc, ks = DSA_Q_TILE, DSA_KEY_CHUNK, DSA_COUNT_CHUNK
    n_chunks = (i * tq) // kc + 1
    n_count = (i * tq) // ks + 1
    low = lax.broadcasted_iota(jnp.int32, (tq, LANES), 1) < IDX_DIM

    def head_halves(tile):
        t = tile.astype(jnp.float32)
        return (jnp.where(low, t, 0.0).astype(jnp.bfloat16),
                jnp.where(low, 0.0, t).astype(jnp.bfloat16))

    for j in range(IDX_HEADS // 2):
        qm_ref[j, 0:tq, :], qm_ref[j, tq:2 * tq, :] = head_halves(
            qi_ref[:, j * LANES:(j + 1) * LANES])
    for j in range(A_HEADS // 2):
        qp_ref[2 * j], qp_ref[2 * j + 1] = head_halves(q_ref[:, j * LANES:(j + 1) * LANES])
    w_t = wi_ref[...].T

    q_pos = i * tq + lax.broadcasted_iota(jnp.int32, (kc, tq), 1)
    k_off = lax.broadcasted_iota(jnp.int32, (kc, tq), 0)

    def score_chunk(c):
        base = pl.multiple_of(c * kc, kc)
        kch = ki_ref[pl.ds(base, kc), :]
        acc = jnp.zeros((kc, tq), jnp.float32)
        for j in range(IDX_HEADS // 2):
            r = _dot_nt(kch, qm_ref[j])
            acc = acc + jnp.maximum(r[:, :tq], 0.0) * w_t[2 * j:2 * j + 1, :]
            acc = acc + jnp.maximum(r[:, tq:], 0.0) * w_t[2 * j + 1:2 * j + 2, :]
        masked = jnp.where(base + k_off <= q_pos, acc, -jnp.inf)
        sc_ref[pl.ds(base, kc), :] = masked
        hb_ref[pl.ds(base, kc), :] = masked.astype(jnp.bfloat16)

    _unrolled_loop(n_chunks, DSA_SCORE_UNROLL, score_chunk)

    part = DSA_COUNT_ROWS

    def count(src_ref, pred):
        one = jnp.ones((), src_ref.dtype)

        def count_chunk(c, cnt):
            base = pl.multiple_of(c * ks, ks)
            for r in range(ks // part):
                row0 = base + r * part
                cnt = jnp.where(pred(src_ref[pl.ds(row0, part), :], row0), cnt + one, cnt)
            return cnt

        cnt = lax.fori_loop(0, n_count, count_chunk, jnp.zeros((part, tq), src_ref.dtype))
        return jnp.sum(cnt.astype(jnp.float32), axis=0, keepdims=True)

    def coarse(step, lo):
        cand = lo + jnp.left_shift(jnp.int32(1), 31 - step)
        cand_b = jnp.broadcast_to(_key_to_float(cand).astype(jnp.bfloat16), (part, tq))
        total = count(hb_ref, lambda blk, row0: blk >= cand_b)
        return jnp.where(total >= top_k, cand, lo)

    k_coarse = lax.fori_loop(
        0, 16, coarse, jnp.full((1, tq), jnp.iinfo(jnp.int32).min, jnp.int32))

    lo0 = jnp.maximum(k_coarse, KEY_LOWEST_FINITE + COARSE_SLACK) - COARSE_SLACK
    hi0 = jnp.maximum(k_coarse + 2 * COARSE_SLACK, lo0)

    def fine(step, carry):
        lo, hi = carry
        mid = lo + jnp.right_shift(hi - lo + 1, 1)
        mid_f = _key_to_float(mid)
        take = count(sc_ref, lambda blk, row0: blk >= mid_f) >= top_k
        return jnp.where(take, mid, lo), jnp.where(take, hi, mid - 1)

    lo, _ = lax.fori_loop(0, (3 * COARSE_SLACK).bit_length(), fine, (lo0, hi0))
    thr = _key_to_float(lo)
    n_ge = count(sc_ref, lambda blk, row0: blk >= thr)

    @pl.when(jnp.max(n_ge) > top_k)
    def _():
        n_gt = count(sc_ref, lambda blk, row0: blk > thr)
        need = top_k - n_gt
        row = lax.broadcasted_iota(jnp.int32, (part, tq), 0)
        index_bits = (sc_ref.shape[0] - 1).bit_length()

        def index_search(step, cut):
            cand = cut + jnp.left_shift(jnp.int32(1), index_bits - 1 - step)
            below = count(sc_ref, lambda blk, row0: (blk == thr) & (row0 + row < cand))
            return jnp.where(below < need, cand, cut)

        cut = lax.fori_loop(0, index_bits, index_search, jnp.zeros((1, tq), jnp.int32))

        def drop_chunk(c, carry):
            base = pl.multiple_of(c * ks, ks)
            blk = sc_ref[pl.ds(base, ks), :]
            pos = base + lax.broadcasted_iota(jnp.int32, (ks, tq), 0)
            sc_ref[pl.ds(base, ks), :] = jnp.where((blk == thr) & (pos > cut), -jnp.inf, blk)
            return carry

        lax.fori_loop(0, n_count, drop_chunk, 0)

    _softmax_init(m_ref, acc_ref)

    group = A_HEADS // 2

    def issue_qk(c, g, s_ref, width):
        base = pl.multiple_of(c * kc, kc)
        for u in range(group):
            h = g * group + u
            cols = slice((h // 2) * LANES, (h // 2 + 1) * LANES)
            s_ref[u, 0:width, :] = _dot_nt(k_ref[pl.ds(base, width), cols], qp_ref[h])

    def consume(c, g, s_ref, last, width):
        base = pl.multiple_of(c * kc, kc)
        if g == 0:
            bias_ref[0:width, :] = jnp.where(
                sc_ref[pl.ds(base, width), :] >= thr, 0.0, NEG_BIG)
        for u in range(group):
            h = g * group + u
            v_t = vt_ref[h * A_HEAD_DIM:(h + 1) * A_HEAD_DIM, pl.ds(base, width)]
            s_t = s_ref[u, 0:width, :] + bias_ref[0:width, :]
            _softmax_step_t(s_t, v_t, m_ref, acc_ref, h)

    def issue_next_tile(s_ref):
        for j in range(group // 2):
            halves = head_halves(qn_ref[:, j * LANES:(j + 1) * LANES])
            for u in range(2):
                s_ref[2 * j + u] = _dot_nt(k_ref[0:kc, j * LANES:(j + 1) * LANES], halves[u])

    half_last = (i + 1) * tq - (n_chunks - 1) * kc <= kc // 2
    _attention_pipeline(n_chunks - 1, half_last, i == 0, kc, DSA_UNROLL, issue_qk, consume,
                        issue_next_tile, (s0_ref, s1_ref))
    _softmax_finish(acc_ref, o_ref)


def _dsa_attention(q, qi, wi, ki, k, vt, batch, seq):
    n = q.shape[0]
    tq = DSA_Q_TILE
    nq = seq // tq
    top_k = min(TOPK_MAX, seq // 4)
    qrow = lambda width: pl.BlockSpec((tq, width), lambda b, i: (b * nq + i, 0))
    q_next = pl.BlockSpec((tq, A_WIDTH), lambda b, i: (b * nq + jnp.minimum(i + 1, nq - 1), 0))
    per_batch = lambda width: pl.BlockSpec((seq, width), lambda b, i: (b, 0))
    return pl.pallas_call(
        functools.partial(_dsa_kernel, top_k),
        grid=(batch, nq),
        in_specs=[qrow(A_WIDTH), q_next, qrow(IDX_HEADS * IDX_DIM), qrow(LANES),
                  per_batch(LANES), per_batch(A_WIDTH),
                  pl.BlockSpec((A_WIDTH, seq), lambda b, i: (0, b))],
        out_specs=qrow(A_WIDTH),
        out_shape=jax.ShapeDtypeStruct((n, A_WIDTH), jnp.bfloat16),
        scratch_shapes=[
            pltpu.VMEM((seq, tq), jnp.float32),
            pltpu.VMEM((seq, tq), jnp.bfloat16),
            pltpu.VMEM((IDX_HEADS // 2, 2 * tq, LANES), jnp.bfloat16),
            pltpu.VMEM((A_HEADS, tq, LANES), jnp.bfloat16),
            pltpu.VMEM((A_HEADS, 1, tq), jnp.float32),
            pltpu.VMEM((A_HEADS, A_HEAD_DIM + SUM_ROWS, tq), jnp.float32),
            pltpu.VMEM((A_HEADS // 2, DSA_KEY_CHUNK, tq), jnp.float32),
            pltpu.VMEM((A_HEADS // 2, DSA_KEY_CHUNK, tq), jnp.float32),
            pltpu.VMEM((DSA_KEY_CHUNK, tq), jnp.float32),
        ],
        compiler_params=_params("arbitrary", "arbitrary"),
        name="dsa_attention",
    )(q, q, qi, wi, ki, k, vt)


def _mix_mlp_kernel(n_in, *refs):
    a_refs, wo_refs = refs[:n_in], refs[n_in:2 * n_in]
    (x_ref, gmix_ref, gpre_ref, gpost_ref, w1_ref, w2_ref, o_ref,
     x1_ref, hn_ref, acc_ref) = refs[2 * n_in:]
    f = pl.program_id(1)

    @pl.when(f == 0)
    def _():
        y = _dot(a_refs[0][...], wo_refs[0][...])
        for a_ref, wo_ref in zip(a_refs[1:], wo_refs[1:]):
            y = y + _dot(a_ref[...], wo_ref[...])
        x1 = x_ref[...] + _rms(y, gmix_ref[...])
        x1_ref[...] = x1
        hn_ref[...] = _rms(x1, gpre_ref[...]).astype(jnp.bfloat16)
        acc_ref[...] = jnp.zeros(acc_ref.shape, jnp.float32)

    h = jnp.maximum(_dot(hn_ref[...], w1_ref[...]), 0.0)
    acc_ref[...] += _dot((h * h).astype(jnp.bfloat16), w2_ref[...])

    @pl.when(f == pl.num_programs(1) - 1)
    def _():
        o_ref[...] = x1_ref[...] + _rms(acc_ref[...], gpost_ref[...])


def _mix_mlp(acts, w_outs, x, g_mix, gpre, gpost, w1, w2):
    n, d = x.shape
    ff = w1.shape[1]
    tm, tf = MLP_ROW_TILE, MLP_FF_TILE
    row = lambda a: pl.BlockSpec((tm, a.shape[1]), lambda i, f: (i, 0))
    full = lambda a: pl.BlockSpec(a.shape, lambda i, f: (0,) * a.ndim)
    return pl.pallas_call(
        functools.partial(_mix_mlp_kernel, len(acts)),
        grid=(n // tm, ff // tf),
        in_specs=[row(a) for a in acts] + [full(w) for w in w_outs]
        + [row(x), full(g_mix), full(gpre), full(gpost),
           pl.BlockSpec((d, tf), lambda i, f: (0, f)),
           pl.BlockSpec((tf, d), lambda i, f: (f, 0))],
        out_specs=row(x),
        out_shape=jax.ShapeDtypeStruct((n, d), jnp.float32),
        scratch_shapes=[pltpu.VMEM((tm, d), jnp.float32), pltpu.VMEM((tm, d), jnp.bfloat16),
                        pltpu.VMEM((tm, d), jnp.float32)],
        compiler_params=_params("arbitrary", "arbitrary"),
        name="mix_mlp",
    )(*acts, *w_outs, x, g_mix, gpre, gpost, w1, w2)


_DKR, _DKV, _DEND = C_Q_RANK, C_Q_RANK + LANES, C_Q_RANK + LANES + C_KV_RANK


def _mla_in_kernel(x_ref, g_ref, wd_ref, qn_ref, kvn_ref, wuq_ref, wuk_ref, wuvt_ref,
                   c_ref, s_ref, q_ref, k_ref, vt_ref):
    hn = _rms(x_ref[...], g_ref[...]).astype(jnp.bfloat16)
    c, s = c_ref[...].T, s_ref[...].T
    scale = (C_NOPE_DIM + C_ROPE_DIM) ** -0.5 * LOG2E
    cq, sq = c * scale, s * scale

    def rope(y, ct, st):
        return y * ct + pltpu.roll(y, LANES - C_ROPE_DIM, axis=1) * st

    qr = _dot(hn, wd_ref[:, 0:_DKV])
    qa = _rms(qr[:, :_DKR], qn_ref[...]).astype(jnp.bfloat16)
    ckv = _rms(_dot(hn, wd_ref[:, _DKV:_DEND]), kvn_ref[...]).astype(jnp.bfloat16)
    kr = rope(qr[:, _DKR:], c, s)
    vt_ref[...] = _dot_nt(wuvt_ref[...], ckv).astype(vt_ref.dtype)

    n_wide = C_HEADS * LANES // MXU_COLS

    def up(j):
        cols = slice(j * MXU_COLS, (j + 1) * MXU_COLS)
        return _dot(qa, wuq_ref[:, cols]), _dot(ckv, wuk_ref[:, cols])

    nxt = up(0)
    for j in range(n_wide):
        q2, k2 = nxt
        if j + 1 < n_wide:
            nxt = up(j + 1)
        for u in range(MXU_COLS // LANES):
            cols = slice(j * MXU_COLS + u * LANES, j * MXU_COLS + (u + 1) * LANES)
            q_ref[:, cols] = rope(q2[:, u * LANES:(u + 1) * LANES], cq, sq).astype(q_ref.dtype)
            k_ref[:, cols] = (k2[:, u * LANES:(u + 1) * LANES] + kr).astype(k_ref.dtype)


def _mla_in_proj(x, gain, w_down, q_norm, kv_norm, w_uq, w_uk, w_uvt, tabs):
    n, d = x.shape
    tm = MLA_ROW_TILE
    row = lambda width: pl.BlockSpec((tm, width), lambda i: (i, 0))
    full = lambda a: pl.BlockSpec(a.shape, lambda i: (0,) * a.ndim)
    tab = pl.BlockSpec((LANES, tm), lambda i: (0, i))
    bf = jnp.bfloat16
    return pl.pallas_call(
        _mla_in_kernel,
        grid=(n // tm,),
        in_specs=[row(d), full(gain), full(w_down), full(q_norm), full(kv_norm),
                  full(w_uq), full(w_uk), full(w_uvt), tab, tab],
        out_specs=(row(C_HEADS * LANES), row(C_HEADS * LANES),
                   pl.BlockSpec((C_HEADS * C_V_DIM, tm), lambda i: (0, i))),
        out_shape=(jax.ShapeDtypeStruct((n, C_HEADS * LANES), bf),
                   jax.ShapeDtypeStruct((n, C_HEADS * LANES), bf),
                   jax.ShapeDtypeStruct((C_HEADS * C_V_DIM, n), bf)),
        compiler_params=_params("arbitrary"),
        name="mla_in_proj",
    )(x, gain, w_down, q_norm, kv_norm, w_uq, w_uk, w_uvt, *tabs)


def _mla_attn_kernel(q_ref, qn_ref, k_ref, vt_ref, o_ref, m_ref, acc_ref, s0_ref, s1_ref):
    i = pl.program_id(2)
    tq, kc, group = MLA_Q_TILE, MLA_KEY_CHUNK, MLA_HEAD_GROUP // 2
    n_full = (i * tq) // kc
    _softmax_init(m_ref, acc_ref)

    def issue_qk(c, g, s_ref, width):
        base = pl.multiple_of(c * kc, kc)
        for u in range(group):
            cols = slice((g * group + u) * LANES, (g * group + u + 1) * LANES)
            s_ref[u, 0:width, :] = _dot_nt(k_ref[pl.ds(base, width), cols], q_ref[:, cols])

    def consume(c, g, s_ref, last, width):
        base = pl.multiple_of(c * kc, kc)
        if last:
            q_pos = i * tq + lax.broadcasted_iota(jnp.int32, (width, tq), 1)
            k_pos = base + lax.broadcasted_iota(jnp.int32, (width, tq), 0)
            bias_t = jnp.where(k_pos <= q_pos, 0.0, NEG_BIG)
        for u in range(group):
            h = g * group + u
            s_t = s_ref[u, 0:width, :]
            if last:
                s_t = s_t + bias_t
            v_t = vt_ref[h * C_V_DIM:(h + 1) * C_V_DIM, pl.ds(base, width)]
            _softmax_step_t(s_t, v_t, m_ref, acc_ref, h)

    def issue_next_tile(s_ref):
        for u in range(group):
            cols = slice(u * LANES, (u + 1) * LANES)
            s_ref[u] = _dot_nt(k_ref[0:kc, cols], qn_ref[:, cols])

    half_last = (i + 1) * tq - n_full * kc <= kc // 2
    _attention_pipeline(n_full, half_last, i == 0, kc, MLA_UNROLL, issue_qk, consume,
                        issue_next_tile, (s0_ref, s1_ref))
    _softmax_finish(acc_ref, o_ref)


def _mla_attention(q, k, vt, batch, seq):
    n = q.shape[0]
    tq, hg = MLA_Q_TILE, MLA_HEAD_GROUP
    nq = seq // tq
    return pl.pallas_call(
        _mla_attn_kernel,
        grid=(batch, C_HEADS // hg, nq),
        in_specs=[pl.BlockSpec((tq, hg * LANES), lambda b, g, i: (b * nq + i, g)),
                  pl.BlockSpec((tq, hg * LANES),
                               lambda b, g, i: (b * nq + jnp.minimum(i + 1, nq - 1), g)),
                  pl.BlockSpec((seq, hg * LANES), lambda b, g, i: (b, g)),
                  pl.BlockSpec((hg * C_V_DIM, seq), lambda b, g, i: (g, b))],
        out_specs=pl.BlockSpec((tq, hg * C_V_DIM), lambda b, g, i: (b * nq + i, g)),
        out_shape=jax.ShapeDtypeStruct((n, C_HEADS * C_V_DIM), jnp.bfloat16),
        scratch_shapes=[pltpu.VMEM((hg, 1, tq), jnp.float32),
                        pltpu.VMEM((hg, C_V_DIM + SUM_ROWS, tq), jnp.float32),
                        pltpu.VMEM((hg // 2, MLA_KEY_CHUNK, tq), jnp.float32),
                        pltpu.VMEM((hg // 2, MLA_KEY_CHUNK, tq), jnp.float32)],
        compiler_params=_params("arbitrary", "arbitrary", "arbitrary"),
        name="mla_attention",
    )(q, q, k, vt)


def _rope_tables(positions, rot_dim, lead, period):
    inv_freq = ROPE_THETA ** (-jnp.arange(0, rot_dim, 2, dtype=jnp.float32) / rot_dim)
    ang = inv_freq[:, None] * positions.astype(jnp.float32).reshape(1, -1)
    cos, sin = jnp.cos(ang), jnp.sin(ang)
    half, n = cos.shape
    ones = lambda w: jnp.ones((w, n), jnp.float32)
    zeros = lambda w: jnp.zeros((w, n), jnp.float32)
    tail = period - lead - rot_dim
    c = jnp.concatenate([ones(lead), cos, cos, ones(tail)], axis=0)
    s_next = jnp.concatenate([zeros(lead), -sin, zeros(half + tail)], axis=0)
    s_prev = jnp.concatenate([zeros(lead + half), sin, zeros(tail)], axis=0)
    reps = LANES // period
    return tuple(jnp.tile(t, (reps, 1)) for t in (c, s_next, s_prev))


def _pack_even_w_in(w):
    d = w.shape[0]
    q, k, v, qi, ki, wi, gb, gc, xi = jnp.split(
        w, [512, 1024, 1536, 2560, 2624, 2640, 3152, 3664], axis=1)
    pad = jnp.zeros((d, LANES - IDX_HEADS), w.dtype)
    packed = jnp.concatenate([q, k, qi, ki, ki, wi, pad, gb, gc, xi], axis=1)
    return packed.astype(jnp.bfloat16), v.T.astype(jnp.bfloat16)


def _pad_heads(w, heads, width):
    r = w.shape[0]
    w = w.reshape(r, heads, width)
    return jnp.pad(w, ((0, 0), (0, 0), (0, LANES - width))).reshape(r, heads * LANES)


def _rope_swapped(w_rope):
    half = C_ROPE_DIM // 2
    return jnp.concatenate([w_rope, w_rope[..., half:], w_rope[..., :half]], axis=-1)


def _mla_rope_tables(positions):
    inv_freq = ROPE_THETA ** (-jnp.arange(0, C_ROPE_DIM, 2, dtype=jnp.float32) / C_ROPE_DIM)
    ang = inv_freq[:, None] * positions.astype(jnp.float32).reshape(1, -1)
    cos, sin = jnp.cos(ang), jnp.sin(ang)
    n = cos.shape[1]
    tail = jnp.zeros((LANES - C_NOPE_DIM - C_ROPE_DIM, n), jnp.float32)
    c = jnp.concatenate([jnp.ones((C_NOPE_DIM, n), jnp.float32), cos, cos, tail], axis=0)
    s = jnp.concatenate([jnp.zeros((C_NOPE_DIM, n), jnp.float32), -sin, sin, tail], axis=0)
    return c, s


def kernel(x, positions, norm_mix_pre, norm_mix_post, norm_ffn_pre, norm_ffn_post, even_w_in, even_conv_w, even_w_out, odd_w_dq, odd_q_norm, odd_w_uq, odd_w_dkv, odd_kv_norm, odd_w_ukv, odd_w_o, mlp_w1, mlp_w2):
    batch, seq, d = x.shape
    depth = norm_mix_pre.shape[0]
    bf = jnp.bfloat16
    tabs_a = _rope_tables(positions, A_ROT_DIM, 0, A_HEAD_DIM)
    tabs_c = _mla_rope_tables(positions)
    h = x.reshape(batch * seq, d)
    row = lambda a: a.reshape(1, -1)
    for layer in range(depth):
        j = layer // 2
        g_pre, g_post = row(norm_mix_pre[layer]), row(norm_mix_post[layer])
        if layer % 2 == 0:
            w_packed, w_vt = _pack_even_w_in(even_w_in[j])
            q, k, vt, qi, ki, wi, conv = _even_in_proj(
                h, g_pre, w_packed, w_vt, tabs_a, even_conv_w[j], seq)
            attn = _dsa_attention(q, qi, wi, ki, k, vt, batch, seq)
            w_out = even_w_out[j].astype(bf)
            acts, w_outs = [attn, conv], [w_out[:A_WIDTH], w_out[A_WIDTH:]]
        else:
            w_dkv = odd_w_dkv[j]
            w_down = jnp.concatenate(
                [odd_w_dq[j], jnp.zeros((d, C_NOPE_DIM), w_dkv.dtype),
                 _rope_swapped(w_dkv[:, C_KV_RANK:]), w_dkv[:, :C_KV_RANK]], axis=1)
            w_ukv = odd_w_ukv[j].reshape(C_KV_RANK, C_HEADS, C_NOPE_DIM + C_V_DIM)
            w_uk = _pad_heads(w_ukv[:, :, :C_NOPE_DIM].reshape(C_KV_RANK, -1), C_HEADS, C_NOPE_DIM)
            w_uv = w_ukv[:, :, C_NOPE_DIM:].reshape(C_KV_RANK, -1)
            w_uq = odd_w_uq[j].reshape(C_Q_RANK, C_HEADS, C_NOPE_DIM + C_ROPE_DIM)
            w_uq = jnp.concatenate(
                [w_uq[:, :, :C_NOPE_DIM], _rope_swapped(w_uq[:, :, C_NOPE_DIM:])],
                axis=-1).reshape(C_Q_RANK, C_HEADS * LANES)
            q, k, vt = _mla_in_proj(h, g_pre, w_down.astype(bf), row(odd_q_norm[j]),
                                    row(odd_kv_norm[j]), w_uq.astype(bf), w_uk.astype(bf),
                                    w_uv.T.astype(bf), tabs_c)
            acts, w_outs = [_mla_attention(q, k, vt, batch, seq)], [odd_w_o[j].astype(bf)]
        h = _mix_mlp(acts, w_outs, h, g_post, row(norm_ffn_pre[layer]),
                     row(norm_ffn_post[layer]), mlp_w1[layer].astype(bf),
                     mlp_w2[layer].astype(bf))
    return h.reshape(batch, seq, d)
```

```python
import functools

import jax
import jax.numpy as jnp
from jax import lax
from jax.experimental import pallas as pl
from jax.experimental.pallas import tpu as pltpu

ROPE_THETA = 500000.0
NORM_EPS = 1e-6
LANES = 128
MXU_COLS = 256
NEG_BIG = -1e30
LOG2E = 1.4426950408889634
SUM_ROWS = 16

A_HEADS = 8
A_HEAD_DIM = 64
A_WIDTH = A_HEADS * A_HEAD_DIM
A_ROT_DIM = A_HEAD_DIM // 4
IDX_HEADS = 16
IDX_DIM = 64
TOPK_MAX = 256
B_WIDTH = 512
CONV_WIDTH = 3
C_HEADS = 16
C_NOPE_DIM = 64
C_ROPE_DIM = 32
C_V_DIM = 64
C_Q_RANK = 384
C_KV_RANK = 256

KEY_LOWEST_FINITE = -2139095040
COARSE_SLACK = 0x8000

ROW_TILE = 512
MLA_ROW_TILE = 512
MLP_ROW_TILE = 512
MLP_FF_TILE = 2048
DSA_Q_TILE = 256
DSA_KEY_CHUNK = 512
DSA_COUNT_CHUNK = 256
DSA_COUNT_ROWS = 64
DSA_UNROLL = 4
DSA_SCORE_UNROLL = 2
MLA_UNROLL = 4
MLA_Q_TILE = 256
MLA_KEY_CHUNK = 512
MLA_HEAD_GROUP = 8
VMEM_LIMIT = 56 * 1024 * 1024


def _dot(a, b):
    return jnp.dot(a, b, preferred_element_type=jnp.float32)


def _dot_nt(a, b):
    return lax.dot_general(a, b, (((1,), (1,)), ((), ())), preferred_element_type=jnp.float32)


def _rms(x, gain):
    ms = jnp.mean(x * x, axis=-1, keepdims=True)
    return x * lax.rsqrt(ms + NORM_EPS) * gain


def _rope_tile(x, c, s_next, s_prev, half):
    return (x * c + pltpu.roll(x, LANES - half, axis=1) * s_next
            + pltpu.roll(x, half, axis=1) * s_prev)


def _params(*semantics):
    return pltpu.CompilerParams(dimension_semantics=semantics, vmem_limit_bytes=VMEM_LIMIT)


_EQ, _EK, _EQI, _EKI, _EWI, _EGB, _EGC, _EXI, _EEND = (
    0, 512, 1024, 2048, 2176, 2304, 2816, 3328, 3840)


def _even_in_kernel(seq_tiles, x_ref, g_ref, w_ref, wvt_ref, c_ref, sn_ref, sp_ref, cw_ref,
                    q_ref, k_ref, vt_ref, qi_ref, ki_ref, wi_ref, conv_ref, carry_ref):
    i = pl.program_id(0)
    tm = x_ref.shape[0]
    hn = _rms(x_ref[...], g_ref[...]).astype(jnp.bfloat16)
    c, sn, sp = c_ref[...].T, sn_ref[...].T, sp_ref[...].T
    half = A_ROT_DIM // 2

    def rope_section(start, width, out_ref, scale=None):
        for t in range(width // MXU_COLS):
            y2 = _dot(hn, w_ref[:, start + t * MXU_COLS:start + (t + 1) * MXU_COLS])
            for u in range(MXU_COLS // LANES):
                y = _rope_tile(y2[:, u * LANES:(u + 1) * LANES], c, sn, sp, half)
                if scale is not None:
                    y = y * scale
                lo = t * MXU_COLS + u * LANES
                out_ref[:, lo:lo + LANES] = y.astype(out_ref.dtype)

    gate_b = _dot(hn, w_ref[:, _EGB:_EGC])
    u = _dot(hn, w_ref[:, _EGC:_EXI]) * _dot(hn, w_ref[:, _EXI:_EEND])
    seq_start = (i % seq_tiles) == 0
    prev = jnp.where(seq_start, 0.0, carry_ref[...])
    carry_ref[...] = u[tm - 8:, :]
    row = lax.broadcasted_iota(jnp.int32, u.shape, 0)
    u1 = jnp.where(row == 0, prev[7:8, :], pltpu.roll(u, 1, axis=0))
    u2 = pltpu.roll(u, 2, axis=0)
    u2 = jnp.where(row == 0, prev[6:7, :], jnp.where(row == 1, prev[7:8, :], u2))
    y = cw_ref[0:1, :] * u2 + cw_ref[1:2, :] * u1 + cw_ref[2:3, :] * u
    conv_ref[...] = (gate_b * y).astype(conv_ref.dtype)

    rope_section(_EQ, A_WIDTH, q_ref, A_HEAD_DIM ** -0.5 * LOG2E)
    rope_section(_EK, A_WIDTH, k_ref)
    rope_section(_EQI, IDX_HEADS * IDX_DIM, qi_ref)
    kw = _dot(hn, w_ref[:, _EKI:_EGB])
    ki_ref[...] = _rope_tile(kw[:, :LANES], c, sn, sp, half).astype(ki_ref.dtype)
    wi_ref[...] = kw[:, LANES:] * (IDX_HEADS ** -0.5 * IDX_DIM ** -0.5)
    vt_ref[...] = _dot_nt(wvt_ref[...], hn).astype(vt_ref.dtype)


def _even_in_proj(x, gain, w_packed, w_vt, tabs, conv_w, seq):
    n, d = x.shape
    tm = ROW_TILE
    row = lambda width: pl.BlockSpec((tm, width), lambda i: (i, 0))
    full = lambda a: pl.BlockSpec(a.shape, lambda i: (0,) * a.ndim)
    tab = pl.BlockSpec((LANES, tm), lambda i: (0, i))
    bf = jnp.bfloat16
    out_shape = (
        jax.ShapeDtypeStruct((n, A_WIDTH), bf), jax.ShapeDtypeStruct((n, A_WIDTH), bf),
        jax.ShapeDtypeStruct((A_WIDTH, n), bf), jax.ShapeDtypeStruct((n, IDX_HEADS * IDX_DIM), bf),
        jax.ShapeDtypeStruct((n, LANES), bf), jax.ShapeDtypeStruct((n, LANES), jnp.float32),
        jax.ShapeDtypeStruct((n, B_WIDTH), bf))
    return pl.pallas_call(
        functools.partial(_even_in_kernel, seq // tm),
        grid=(n // tm,),
        in_specs=[row(d), full(gain), full(w_packed), full(w_vt), tab, tab, tab,
                  full(conv_w)],
        out_specs=(row(A_WIDTH), row(A_WIDTH), pl.BlockSpec((A_WIDTH, tm), lambda i: (0, i)),
                   row(IDX_HEADS * IDX_DIM), row(LANES), row(LANES), row(B_WIDTH)),
        out_shape=out_shape,
        scratch_shapes=[pltpu.VMEM((8, B_WIDTH), jnp.float32)],
        compiler_params=_params("arbitrary"),
        name="even_in_proj",
    )(x, gain, w_packed, w_vt, *tabs, conv_w)


def _key_to_float(key):
    bits = key ^ ((key >> 31) & jnp.int32(0x7FFFFFFF))
    return lax.bitcast_convert_type(bits, jnp.float32)


def _softmax_step_t(s_t, v_t, m_ref, acc_ref, h):
    kc = s_t.shape[0]
    m_prev = m_ref[h]
    m_new = jnp.maximum(m_prev, jnp.max(s_t, axis=0, keepdims=True))
    alpha = jnp.exp2(m_prev - m_new)
    p_t = jnp.exp2(s_t - m_new).astype(jnp.bfloat16)
    v_aug = jnp.concatenate([v_t, jnp.ones((SUM_ROWS, kc), jnp.bfloat16)], axis=0)
    acc_ref[h] = alpha * acc_ref[h] + _dot(v_aug, p_t)
    m_ref[h] = m_new


def _softmax_init(m_ref, acc_ref):
    m_ref[...] = jnp.full(m_ref.shape, NEG_BIG, jnp.float32)
    acc_ref[...] = jnp.zeros(acc_ref.shape, jnp.float32)


def _softmax_finish(acc_ref, o_ref):
    heads, rows, _ = acc_ref.shape
    dv = rows - SUM_ROWS
    o_t = jnp.concatenate(
        [acc_ref[h, 0:dv, :] / acc_ref[h, dv:dv + 1, :] for h in range(heads)], axis=0)
    o_ref[...] = o_t.astype(o_ref.dtype)


def _unrolled_loop(n, unroll, step):
    def steps(first, count):
        for t in range(count):
            step(first + t)

    def unrolled(p, carry):
        steps(unroll * p, unroll)
        return carry

    lax.fori_loop(0, n // unroll, unrolled, 0)
    done = (n // unroll) * unroll
    size = unroll // 2
    while size >= 1:
        has = (n & size) != 0
        pl.when(has)(functools.partial(steps, done, size))
        done = done + jnp.where(has, size, 0)
        size //= 2


def _attention_pipeline(n_last, half_last, first_tile, kc, unroll, issue_qk, consume,
                        issue_next_tile, s_refs):
    s0, s1 = s_refs
    pl.when(first_tile)(lambda: issue_qk(0, 0, s0, kc))

    def step(c):
        issue_qk(c, 1, s1, kc)
        consume(c, 0, s0, False, kc)
        issue_qk(c + 1, 0, s0, kc)
        consume(c, 1, s1, False, kc)

    _unrolled_loop(n_last, unroll, step)

    def last_chunk(width):
        issue_qk(n_last, 1, s1, width)
        consume(n_last, 0, s0, True, width)
        issue_next_tile(s0)
        consume(n_last, 1, s1, True, width)

    pl.when(half_last)(lambda: last_chunk(kc // 2))
    pl.when(jnp.logical_not(half_last))(lambda: last_chunk(kc))


def _dsa_kernel(top_k, q_ref, qn_ref, qi_ref, wi_ref, ki_ref, k_ref, vt_ref, o_ref,
                sc_ref, hb_ref, qm_ref, qp_ref, m_ref, acc_ref, s0_ref, s1_ref, bias_ref):
    i = pl.program_id(1)
    tq, kc, ks = DSA_Q_TILE, DSA_KEY_CHUNK, DSA_COUNT_CHUNK
    n_chunks = (i * tq) // kc + 1
    n_count = (i * tq) // ks + 1
    low = lax.broadcasted_iota(jnp.int32, (tq, LANES), 1) < IDX_DIM

    def head_halves(tile):
        t = tile.astype(jnp.float32)
        return (jnp.where(low, t, 0.0).astype(jnp.bfloat16),
                jnp.where(low, 0.0, t).astype(jnp.bfloat16))

    for j in range(IDX_HEADS // 2):
        qm_ref[j, 0:tq, :], qm_ref[j, tq:2 * tq, :] = head_halves(
            qi_ref[:, j * LANES:(j + 1) * LANES])
    for j in range(A_HEADS // 2):
        qp_ref[2 * j], qp_ref[2 * j + 1] = head_halves(q_ref[:, j * LANES:(j + 1) * LANES])
    w_t = wi_ref[...].T

    q_pos = i * tq + lax.broadcasted_iota(jnp.int32, (kc, tq), 1)
    k_off = lax.broadcasted_iota(jnp.int32, (kc, tq), 0)

    def score_chunk(c):
        base = pl.multiple_of(c * kc, kc)
        kch = ki_ref[pl.ds(base, kc), :]
        acc = jnp.zeros((kc, tq), jnp.float32)
        for j in range(IDX_HEADS // 2):
            r = _dot_nt(kch, qm_ref[j])
            acc = acc + jnp.maximum(r[:, :tq], 0.0) * w_t[2 * j:2 * j + 1, :]
            acc = acc + jnp.maximum(r[:, tq:], 0.0) * w_t[2 * j + 1:2 * j + 2, :]
        masked = jnp.where(base + k_off <= q_pos, acc, -jnp.inf)
        sc_ref[pl.ds(base, kc), :] = masked
        hb_ref[pl.ds(base, kc), :] = masked.astype(jnp.bfloat16)

    _unrolled_loop(n_chunks, DSA_SCORE_UNROLL, score_chunk)

    part = DSA_COUNT_ROWS

    def count(src_ref, pred):
        one = jnp.ones((), src_ref.dtype)

        def count_chunk(c, cnt):
            base = pl.multiple_of(c * ks, ks)
            for r in range(ks // part):
                row0 = base + r * part
                cnt = jnp.where(pred(src_ref[pl.ds(row0, part), :], row0), cnt + one, cnt)
            return cnt

        cnt = lax.fori_loop(0, n_count, count_chunk, jnp.zeros((part, tq), src_ref.dtype))
        return jnp.sum(cnt.astype(jnp.float32), axis=0, keepdims=True)

    def coarse(step, lo):
        cand = lo + jnp.left_shift(jnp.int32(1), 31 - step)
        cand_b = jnp.broadcast_to(_key_to_float(cand).astype(jnp.bfloat16), (part, tq))
        total = count(hb_ref, lambda blk, row0: blk >= cand_b)
        return jnp.where(total >= top_k, cand, lo)

    k_coarse = lax.fori_loop(
        0, 16, coarse, jnp.full((1, tq), jnp.iinfo(jnp.int32).min, jnp.int32))

    lo0 = jnp.maximum(k_coarse, KEY_LOWEST_FINITE + COARSE_SLACK) - COARSE_SLACK
    hi0 = jnp.maximum(k_coarse + 2 * COARSE_SLACK, lo0)

    def fine(step, carry):
        lo, hi = carry
        mid = lo + jnp.right_shift(hi - lo + 1, 1)
        mid_f = _key_to_float(mid)
        take = count(sc_ref, lambda blk, row0: blk >= mid_f) >= top_k
        return jnp.where(take, mid, lo), jnp.where(take, hi, mid - 1)

    lo, _ = lax.fori_loop(0, (3 * COARSE_SLACK).bit_length(), fine, (lo0, hi0))
    thr = _key_to_float(lo)
    n_ge = count(sc_ref, lambda blk, row0: blk >= thr)

    @pl.when(jnp.max(n_ge) > top_k)
    def _():
        n_gt = count(sc_ref, lambda blk, row0: blk > thr)
        need = top_k - n_gt
        row = lax.broadcasted_iota(jnp.int32, (part, tq), 0)
        index_bits = (sc_ref.shape[0] - 1).bit_length()

        def index_search(step, cut):
            cand = cut + jnp.left_shift(jnp.int32(1), index_bits - 1 - step)
            below = count(sc_ref, lambda blk, row0: (blk == thr) & (row0 + row < cand))
            return jnp.where(below < need, cand, cut)

        cut = lax.fori_loop(0, index_bits, index_search, jnp.zeros((1, tq), jnp.int32))

        def drop_chunk(c, carry):
            base = pl.multiple_of(c * ks, ks)
            blk = sc_ref[pl.ds(base, ks), :]
            pos = base + lax.broadcasted_iota(jnp.int32, (ks, tq), 0)
            sc_ref[pl.ds(base, ks), :] = jnp.where((blk == thr) & (pos > cut), -jnp.inf, blk)
            return carry

        lax.fori_loop(0, n_count, drop_chunk, 0)

    _softmax_init(m_ref, acc_ref)

    group = A_HEADS // 2

    def issue_qk(c, g, s_ref, width):
        base = pl.multiple_of(c * kc, kc)
        for u in range(group):
            h = g * group + u
            cols = slice((h // 2) * LANES, (h // 2 + 1) * LANES)
            s_ref[u, 0:width, :] = _dot_nt(k_ref[pl.ds(base, width), cols], qp_ref[h])

    def consume(c, g, s_ref, last, width):
        base = pl.multiple_of(c * kc, kc)
        if g == 0:
            bias_ref[0:width, :] = jnp.where(
                sc_ref[pl.ds(base, width), :] >= thr, 0.0, NEG_BIG)
        for u in range(group):
            h = g * group + u
            v_t = vt_ref[h * A_HEAD_DIM:(h + 1) * A_HEAD_DIM, pl.ds(base, width)]
            s_t = s_ref[u, 0:width, :] + bias_ref[0:width, :]
            _softmax_step_t(s_t, v_t, m_ref, acc_ref, h)

    def issue_next_tile(s_ref):
        for j in range(group // 2):
            halves = head_halves(qn_ref[:, j * LANES:(j + 1) * LANES])
            for u in range(2):
                s_ref[2 * j + u] = _dot_nt(k_ref[0:kc, j * LANES:(j + 1) * LANES], halves[u])

    half_last = (i + 1) * tq - (n_chunks - 1) * kc <= kc // 2
    _attention_pipeline(n_chunks - 1, half_last, i == 0, kc, DSA_UNROLL, issue_qk, consume,
                        issue_next_tile, (s0_ref, s1_ref))
    _softmax_finish(acc_ref, o_ref)


def _dsa_attention(q, qi, wi, ki, k, vt, batch, seq):
    n = q.shape[0]
    tq = DSA_Q_TILE
    nq = seq // tq
    top_k = min(TOPK_MAX, seq // 4)
    qrow = lambda width: pl.BlockSpec((tq, width), lambda b, i: (b * nq + i, 0))
    q_next = pl.BlockSpec((tq, A_WIDTH), lambda b, i: (b * nq + jnp.minimum(i + 1, nq - 1), 0))
    per_batch = lambda width: pl.BlockSpec((seq, width), lambda b, i: (b, 0))
    return pl.pallas_call(
        functools.partial(_dsa_kernel, top_k),
        grid=(batch, nq),
        in_specs=[qrow(A_WIDTH), q_next, qrow(IDX_HEADS * IDX_DIM), qrow(LANES),
                  per_batch(LANES), per_batch(A_WIDTH),
                  pl.BlockSpec((A_WIDTH, seq), lambda b, i: (0, b))],
        out_specs=pl.BlockSpec((A_WIDTH, tq), lambda b, i: (0, b * nq + i)),
        out_shape=jax.ShapeDtypeStruct((A_WIDTH, n), jnp.bfloat16),
        scratch_shapes=[
            pltpu.VMEM((seq, tq), jnp.float32),
            pltpu.VMEM((seq, tq), jnp.bfloat16),
            pltpu.VMEM((IDX_HEADS // 2, 2 * tq, LANES), jnp.bfloat16),
            pltpu.VMEM((A_HEADS, tq, LANES), jnp.bfloat16),
            pltpu.VMEM((A_HEADS, 1, tq), jnp.float32),
            pltpu.VMEM((A_HEADS, A_HEAD_DIM + SUM_ROWS, tq), jnp.float32),
            pltpu.VMEM((A_HEADS // 2, DSA_KEY_CHUNK, tq), jnp.float32),
            pltpu.VMEM((A_HEADS // 2, DSA_KEY_CHUNK, tq), jnp.float32),
            pltpu.VMEM((DSA_KEY_CHUNK, tq), jnp.float32),
        ],
        compiler_params=_params("arbitrary", "arbitrary"),
        name="dsa_attention",
    )(q, q, qi, wi, ki, k, vt)


def _dot_tn(a_t, b):
    return lax.dot_general(a_t, b, (((0,), (0,)), ((), ())), preferred_element_type=jnp.float32)


def _mix_mlp_kernel(n_in, *refs):
    a_refs, wo_refs = refs[:n_in], refs[n_in:2 * n_in]
    (x_ref, gmix_ref, gpre_ref, gpost_ref, w1_ref, w2_ref, o_ref,
     x1_ref, hn_ref, acc_ref) = refs[2 * n_in:]
    f = pl.program_id(1)

    @pl.when(f == 0)
    def _():
        y = _dot_tn(a_refs[0][...], wo_refs[0][...])
        for a_ref, wo_ref in zip(a_refs[1:], wo_refs[1:]):
            y = y + _dot(a_ref[...], wo_ref[...])
        x1 = x_ref[...] + _rms(y, gmix_ref[...])
        x1_ref[...] = x1
        hn_ref[...] = _rms(x1, gpre_ref[...]).astype(jnp.bfloat16)
        acc_ref[...] = jnp.zeros(acc_ref.shape, jnp.float32)

    h = jnp.maximum(_dot(hn_ref[...], w1_ref[...]), 0.0)
    acc_ref[...] += _dot((h * h).astype(jnp.bfloat16), w2_ref[...])

    @pl.when(f == pl.num_programs(1) - 1)
    def _():
        o_ref[...] = x1_ref[...] + _rms(acc_ref[...], gpost_ref[...])


def _mix_mlp(acts, w_outs, x, g_mix, gpre, gpost, w1, w2):
    n, d = x.shape
    ff = w1.shape[1]
    tm, tf = MLP_ROW_TILE, MLP_FF_TILE
    row = lambda a: pl.BlockSpec((tm, a.shape[1]), lambda i, f: (i, 0))
    col = lambda a: pl.BlockSpec((a.shape[0], tm), lambda i, f: (0, i))
    full = lambda a: pl.BlockSpec(a.shape, lambda i, f: (0,) * a.ndim)
    return pl.pallas_call(
        functools.partial(_mix_mlp_kernel, len(acts)),
        grid=(n // tm, ff // tf),
        in_specs=[col(acts[0])] + [row(a) for a in acts[1:]] + [full(w) for w in w_outs]
        + [row(x), full(g_mix), full(gpre), full(gpost),
           pl.BlockSpec((d, tf), lambda i, f: (0, f)),
           pl.BlockSpec((tf, d), lambda i, f: (f, 0))],
        out_specs=row(x),
        out_shape=jax.ShapeDtypeStruct((n, d), jnp.float32),
        scratch_shapes=[pltpu.VMEM((tm, d), jnp.float32), pltpu.VMEM((tm, d), jnp.bfloat16),
                        pltpu.VMEM((tm, d), jnp.float32)],
        compiler_params=_params("arbitrary", "arbitrary"),
        name="mix_mlp",
    )(*acts, *w_outs, x, g_mix, gpre, gpost, w1, w2)


_DKR, _DKV, _DEND = C_Q_RANK, C_Q_RANK + LANES, C_Q_RANK + LANES + C_KV_RANK


def _mla_in_kernel(x_ref, g_ref, wd_ref, qn_ref, kvn_ref, wuq_ref, wuk_ref, wuvt_ref,
                   c_ref, s_ref, q_ref, k_ref, vt_ref):
    hn = _rms(x_ref[...], g_ref[...]).astype(jnp.bfloat16)
    c, s = c_ref[...].T, s_ref[...].T
    scale = (C_NOPE_DIM + C_ROPE_DIM) ** -0.5 * LOG2E
    cq, sq = c * scale, s * scale

    def rope(y, ct, st):
        return y * ct + pltpu.roll(y, LANES - C_ROPE_DIM, axis=1) * st

    kvd = _dot(hn, wd_ref[:, _DKV:_DEND])
    qr = _dot(hn, wd_ref[:, 0:_DKV])
    ckv = _rms(kvd, kvn_ref[...]).astype(jnp.bfloat16)
    vt_ref[...] = _dot_nt(wuvt_ref[...], ckv).astype(vt_ref.dtype)
    kr = rope(qr[:, _DKR:], c, s)
    n_wide = C_HEADS * LANES // MXU_COLS
    for j in range(n_wide):
        k2 = _dot(ckv, wuk_ref[:, j * MXU_COLS:(j + 1) * MXU_COLS])
        for u in range(MXU_COLS // LANES):
            cols = slice(j * MXU_COLS + u * LANES, j * MXU_COLS + (u + 1) * LANES)
            k_ref[:, cols] = (k2[:, u * LANES:(u + 1) * LANES] + kr).astype(k_ref.dtype)

    qa = _rms(qr[:, :_DKR], qn_ref[...]).astype(jnp.bfloat16)
    up = lambda j: _dot(qa, wuq_ref[:, j * MXU_COLS:(j + 1) * MXU_COLS])
    nxt = up(0)
    for j in range(n_wide):
        q2 = nxt
        if j + 1 < n_wide:
            nxt = up(j + 1)
        for u in range(MXU_COLS // LANES):
            cols = slice(j * MXU_COLS + u * LANES, j * MXU_COLS + (u + 1) * LANES)
            q_ref[:, cols] = rope(q2[:, u * LANES:(u + 1) * LANES], cq, sq).astype(q_ref.dtype)


def _mla_in_proj(x, gain, w_down, q_norm, kv_norm, w_uq, w_uk, w_uvt, tabs):
    n, d = x.shape
    tm = MLA_ROW_TILE
    row = lambda width: pl.BlockSpec((tm, width), lambda i: (i, 0))
    full = lambda a: pl.BlockSpec(a.shape, lambda i: (0,) * a.ndim)
    tab = pl.BlockSpec((LANES, tm), lambda i: (0, i))
    bf = jnp.bfloat16
    return pl.pallas_call(
        _mla_in_kernel,
        grid=(n // tm,),
        in_specs=[row(d), full(gain), full(w_down), full(q_norm), full(kv_norm),
                  full(w_uq), full(w_uk), full(w_uvt), tab, tab],
        out_specs=(row(C_HEADS * LANES), row(C_HEADS * LANES),
                   pl.BlockSpec((C_HEADS * C_V_DIM, tm), lambda i: (0, i))),
        out_shape=(jax.ShapeDtypeStruct((n, C_HEADS * LANES), bf),
                   jax.ShapeDtypeStruct((n, C_HEADS * LANES), bf),
                   jax.ShapeDtypeStruct((C_HEADS * C_V_DIM, n), bf)),
        compiler_params=_params("arbitrary"),
        name="mla_in_proj",
    )(x, gain, w_down, q_norm, kv_norm, w_uq, w_uk, w_uvt, *tabs)


def _mla_attn_kernel(q_ref, qn_ref, k_ref, vt_ref, o_ref, m_ref, acc_ref, s0_ref, s1_ref):
    i = pl.program_id(2)
    tq, kc, group = MLA_Q_TILE, MLA_KEY_CHUNK, MLA_HEAD_GROUP // 2
    n_full = (i * tq) // kc
    _softmax_init(m_ref, acc_ref)

    def issue_qk(c, g, s_ref, width):
        base = pl.multiple_of(c * kc, kc)
        for u in range(group):
            cols = slice((g * group + u) * LANES, (g * group + u + 1) * LANES)
            s_ref[u, 0:width, :] = _dot_nt(k_ref[pl.ds(base, width), cols], q_ref[:, cols])

    def consume(c, g, s_ref, last, width):
        base = pl.multiple_of(c * kc, kc)
        if last:
            q_pos = i * tq + lax.broadcasted_iota(jnp.int32, (width, tq), 1)
            k_pos = base + lax.broadcasted_iota(jnp.int32, (width, tq), 0)
            bias_t = jnp.where(k_pos <= q_pos, 0.0, NEG_BIG)
        for u in range(group):
            h = g * group + u
            s_t = s_ref[u, 0:width, :]
            if last:
                s_t = s_t + bias_t
            v_t = vt_ref[h * C_V_DIM:(h + 1) * C_V_DIM, pl.ds(base, width)]
            _softmax_step_t(s_t, v_t, m_ref, acc_ref, h)

    def issue_next_tile(s_ref):
        for u in range(group):
            cols = slice(u * LANES, (u + 1) * LANES)
            s_ref[u] = _dot_nt(k_ref[0:kc, cols], qn_ref[:, cols])

    half_last = (i + 1) * tq - n_full * kc <= kc // 2
    _attention_pipeline(n_full, half_last, i == 0, kc, MLA_UNROLL, issue_qk, consume,
                        issue_next_tile, (s0_ref, s1_ref))
    _softmax_finish(acc_ref, o_ref)


def _mla_attention(q, k, vt, batch, seq):
    n = q.shape[0]
    tq, hg = MLA_Q_TILE, MLA_HEAD_GROUP
    nq = seq // tq
    return pl.pallas_call(
        _mla_attn_kernel,
        grid=(batch, C_HEADS // hg, nq),
        in_specs=[pl.BlockSpec((tq, hg * LANES), lambda b, g, i: (b * nq + i, g)),
                  pl.BlockSpec((tq, hg * LANES),
                               lambda b, g, i: (b * nq + jnp.minimum(i + 1, nq - 1), g)),
                  pl.BlockSpec((seq, hg * LANES), lambda b, g, i: (b, g)),
                  pl.BlockSpec((hg * C_V_DIM, seq), lambda b, g, i: (g, b))],
        out_specs=pl.BlockSpec((hg * C_V_DIM, tq), lambda b, g, i: (g, b * nq + i)),
        out_shape=jax.ShapeDtypeStruct((C_HEADS * C_V_DIM, n), jnp.bfloat16),
        scratch_shapes=[pltpu.VMEM((hg, 1, tq), jnp.float32),
                        pltpu.VMEM((hg, C_V_DIM + SUM_ROWS, tq), jnp.float32),
                        pltpu.VMEM((hg // 2, MLA_KEY_CHUNK, tq), jnp.float32),
                        pltpu.VMEM((hg // 2, MLA_KEY_CHUNK, tq), jnp.float32)],
        compiler_params=_params("arbitrary", "arbitrary", "arbitrary"),
        name="mla_attention",
    )(q, q, k, vt)


def _rope_tables(positions, rot_dim, lead, period):
    inv_freq = ROPE_THETA ** (-jnp.arange(0, rot_dim, 2, dtype=jnp.float32) / rot_dim)
    ang = inv_freq[:, None] * positions.astype(jnp.float32).reshape(1, -1)
    cos, sin = jnp.cos(ang), jnp.sin(ang)
    half, n = cos.shape
    ones = lambda w: jnp.ones((w, n), jnp.float32)
    zeros = lambda w: jnp.zeros((w, n), jnp.float32)
    tail = period - lead - rot_dim
    c = jnp.concatenate([ones(lead), cos, cos, ones(tail)], axis=0)
    s_next = jnp.concatenate([zeros(lead), -sin, zeros(half + tail)], axis=0)
    s_prev = jnp.concatenate([zeros(lead + half), sin, zeros(tail)], axis=0)
    reps = LANES // period
    return tuple(jnp.tile(t, (reps, 1)) for t in (c, s_next, s_prev))


def _pack_even_w_in(w):
    d = w.shape[0]
    q, k, v, qi, ki, wi, gb, gc, xi = jnp.split(
        w, [512, 1024, 1536, 2560, 2624, 2640, 3152, 3664], axis=1)
    pad = jnp.zeros((d, LANES - IDX_HEADS), w.dtype)
    packed = jnp.concatenate([q, k, qi, ki, ki, wi, pad, gb, gc, xi], axis=1)
    return packed.astype(jnp.bfloat16), v.T.astype(jnp.bfloat16)


def _pad_heads(w, heads, width):
    r = w.shape[0]
    w = w.reshape(r, heads, width)
    return jnp.pad(w, ((0, 0), (0, 0), (0, LANES - width))).reshape(r, heads * LANES)


def _rope_swapped(w_rope):
    half = C_ROPE_DIM // 2
    return jnp.concatenate([w_rope, w_rope[..., half:], w_rope[..., :half]], axis=-1)


def _mla_rope_tables(positions):
    inv_freq = ROPE_THETA ** (-jnp.arange(0, C_ROPE_DIM, 2, dtype=jnp.float32) / C_ROPE_DIM)
    ang = inv_freq[:, None] * positions.astype(jnp.float32).reshape(1, -1)
    cos, sin = jnp.cos(ang), jnp.sin(ang)
    n = cos.shape[1]
    tail = jnp.zeros((LANES - C_NOPE_DIM - C_ROPE_DIM, n), jnp.float32)
    c = jnp.concatenate([jnp.ones((C_NOPE_DIM, n), jnp.float32), cos, cos, tail], axis=0)
    s = jnp.concatenate([jnp.zeros((C_NOPE_DIM, n), jnp.float32), -sin, sin, tail], axis=0)
    return c, s


def kernel(x, positions, norm_mix_pre, norm_mix_post, norm_ffn_pre, norm_ffn_post, even_w_in, even_conv_w, even_w_out, odd_w_dq, odd_q_norm, odd_w_uq, odd_w_dkv, odd_kv_norm, odd_w_ukv, odd_w_o, mlp_w1, mlp_w2):
    batch, seq, d = x.shape
    depth = norm_mix_pre.shape[0]
    bf = jnp.bfloat16
    tabs_a = _rope_tables(positions, A_ROT_DIM, 0, A_HEAD_DIM)
    tabs_c = _mla_rope_tables(positions)
    h = x.reshape(batch * seq, d)
    row = lambda a: a.reshape(1, -1)
    for layer in range(depth):
        j = layer // 2
        g_pre, g_post = row(norm_mix_pre[layer]), row(norm_mix_post[layer])
        if layer % 2 == 0:
            w_packed, w_vt = _pack_even_w_in(even_w_in[j])
            q, k, vt, qi, ki, wi, conv = _even_in_proj(
                h, g_pre, w_packed, w_vt, tabs_a, even_conv_w[j], seq)
            attn = _dsa_attention(q, qi, wi, ki, k, vt, batch, seq)
            w_out = even_w_out[j].astype(bf)
            acts, w_outs = [attn, conv], [w_out[:A_WIDTH], w_out[A_WIDTH:]]
        else:
            w_dkv = odd_w_dkv[j]
            w_down = jnp.concatenate(
                [odd_w_dq[j], jnp.zeros((d, C_NOPE_DIM), w_dkv.dtype),
                 _rope_swapped(w_dkv[:, C_KV_RANK:]), w_dkv[:, :C_KV_RANK]], axis=1)
            w_ukv = odd_w_ukv[j].reshape(C_KV_RANK, C_HEADS, C_NOPE_DIM + C_V_DIM)
            w_uk = _pad_heads(w_ukv[:, :, :C_NOPE_DIM].reshape(C_KV_RANK, -1), C_HEADS, C_NOPE_DIM)
            w_uv = w_ukv[:, :, C_NOPE_DIM:].reshape(C_KV_RANK, -1)
            w_uq = odd_w_uq[j].reshape(C_Q_RANK, C_HEADS, C_NOPE_DIM + C_ROPE_DIM)
            w_uq = jnp.concatenate(
                [w_uq[:, :, :C_NOPE_DIM], _rope_swapped(w_uq[:, :, C_NOPE_DIM:])],
                axis=-1).reshape(C_Q_RANK, C_HEADS * LANES)
            q, k, vt = _mla_in_proj(h, g_pre, w_down.astype(bf), row(odd_q_norm[j]),
                                    row(odd_kv_norm[j]), w_uq.astype(bf), w_uk.astype(bf),
                                    w_uv.T.astype(bf), tabs_c)
            acts, w_outs = [_mla_attention(q, k, vt, batch, seq)], [odd_w_o[j].astype(bf)]
        h = _mix_mlp(acts, w_outs, h, g_post, row(norm_ffn_pre[layer]),
                     row(norm_ffn_post[layer]), mlp_w1[layer].astype(bf),
                     mlp_w2[layer].astype(bf))
    return h.reshape(batch, seq, d)
```

```python
import functools

import jax
import jax.numpy as jnp
from jax import lax
from jax.experimental import pallas as pl
from jax.experimental.pallas import tpu as pltpu

ROPE_THETA = 500000.0
NORM_EPS = 1e-6
LANES = 128
MXU_COLS = 256
NEG_BIG = -1e30
LOG2E = 1.4426950408889634
SUM_ROWS = 16

A_HEADS = 8
A_HEAD_DIM = 64
A_WIDTH = A_HEADS * A_HEAD_DIM
A_ROT_DIM = A_HEAD_DIM // 4
IDX_HEADS = 16
IDX_DIM = 64
TOPK_MAX = 256
B_WIDTH = 512
CONV_WIDTH = 3
C_HEADS = 16
C_NOPE_DIM = 64
C_ROPE_DIM = 32
C_V_DIM = 64
C_Q_RANK = 384
C_KV_RANK = 256

KEY_LOWEST_FINITE = -2139095040
COARSE_SLACK = 0x8000

ROW_TILE = 512
MLA_ROW_TILE = 512
MLP_ROW_TILE = 512
MLP_FF_TILE = 2048
DSA_Q_TILE = 256
DSA_KEY_CHUNK = 512
DSA_COUNT_CHUNK = 256
DSA_COUNT_ROWS = 64
DSA_UNROLL = 4
DSA_SCORE_UNROLL = 4
MLA_UNROLL = 4
MLA_Q_TILE = 256
MLA_KEY_CHUNK = 512
MLA_HEAD_GROUP = 8
VMEM_LIMIT = 56 * 1024 * 1024


def _dot(a, b):
    return jnp.dot(a, b, preferred_element_type=jnp.float32)


def _dot_nt(a, b):
    return lax.dot_general(a, b, (((1,), (1,)), ((), ())), preferred_element_type=jnp.float32)


def _rms(x, gain):
    ms = jnp.mean(x * x, axis=-1, keepdims=True)
    return x * lax.rsqrt(ms + NORM_EPS) * gain


def _rope_tile(x, c, s_next, s_prev, half):
    return (x * c + pltpu.roll(x, LANES - half, axis=1) * s_next
            + pltpu.roll(x, half, axis=1) * s_prev)


def _params(*semantics):
    return pltpu.CompilerParams(dimension_semantics=semantics, vmem_limit_bytes=VMEM_LIMIT)


_EQ, _EK, _EQI, _EKI, _EWI, _EGB, _EGC, _EXI, _EEND = (
    0, 512, 1024, 2048, 2176, 2304, 2816, 3328, 3840)


def _even_in_kernel(seq_tiles, x_ref, g_ref, w_ref, wvt_ref, c_ref, sn_ref, sp_ref, cw_ref,
                    q_ref, k_ref, vt_ref, qi_ref, ki_ref, wi_ref, conv_ref, carry_ref):
    i = pl.program_id(0)
    tm = x_ref.shape[0]
    hn = _rms(x_ref[...], g_ref[...]).astype(jnp.bfloat16)
    c, sn, sp = c_ref[...].T, sn_ref[...].T, sp_ref[...].T
    half = A_ROT_DIM // 2

    def rope_section(start, width, out_ref, scale=None):
        for t in range(width // MXU_COLS):
            y2 = _dot(hn, w_ref[:, start + t * MXU_COLS:start + (t + 1) * MXU_COLS])
            for u in range(MXU_COLS // LANES):
                y = _rope_tile(y2[:, u * LANES:(u + 1) * LANES], c, sn, sp, half)
                if scale is not None:
                    y = y * scale
                lo = t * MXU_COLS + u * LANES
                out_ref[:, lo:lo + LANES] = y.astype(out_ref.dtype)

    gate_b = _dot(hn, w_ref[:, _EGB:_EGC])
    u = _dot(hn, w_ref[:, _EGC:_EXI]) * _dot(hn, w_ref[:, _EXI:_EEND])
    seq_start = (i % seq_tiles) == 0
    prev = jnp.where(seq_start, 0.0, carry_ref[...])
    carry_ref[...] = u[tm - 8:, :]
    row = lax.broadcasted_iota(jnp.int32, u.shape, 0)
    u1 = jnp.where(row == 0, prev[7:8, :], pltpu.roll(u, 1, axis=0))
    u2 = pltpu.roll(u, 2, axis=0)
    u2 = jnp.where(row == 0, prev[6:7, :], jnp.where(row == 1, prev[7:8, :], u2))
    y = cw_ref[0:1, :] * u2 + cw_ref[1:2, :] * u1 + cw_ref[2:3, :] * u
    conv_ref[...] = (gate_b * y).astype(conv_ref.dtype)

    rope_section(_EQ, A_WIDTH, q_ref, A_HEAD_DIM ** -0.5 * LOG2E)
    rope_section(_EK, A_WIDTH, k_ref)
    rope_section(_EQI, IDX_HEADS * IDX_DIM, qi_ref)
    kw = _dot(hn, w_ref[:, _EKI:_EGB])
    ki_ref[...] = _rope_tile(kw[:, :LANES], c, sn, sp, half).astype(ki_ref.dtype)
    wi_ref[...] = kw[:, LANES:] * (IDX_HEADS ** -0.5 * IDX_DIM ** -0.5)
    vt_ref[...] = _dot_nt(wvt_ref[...], hn).astype(vt_ref.dtype)


def _even_in_proj(x, gain, w_packed, w_vt, tabs, conv_w, seq):
    n, d = x.shape
    tm = ROW_TILE
    row = lambda width: pl.BlockSpec((tm, width), lambda i: (i, 0))
    full = lambda a: pl.BlockSpec(a.shape, lambda i: (0,) * a.ndim)
    tab = pl.BlockSpec((LANES, tm), lambda i: (0, i))
    bf = jnp.bfloat16
    out_shape = (
        jax.ShapeDtypeStruct((n, A_WIDTH), bf), jax.ShapeDtypeStruct((n, A_WIDTH), bf),
        jax.ShapeDtypeStruct((A_WIDTH, n), bf), jax.ShapeDtypeStruct((n, IDX_HEADS * IDX_DIM), bf),
        jax.ShapeDtypeStruct((n, LANES), bf), jax.ShapeDtypeStruct((n, LANES), jnp.float32),
        jax.ShapeDtypeStruct((n, B_WIDTH), bf))
    return pl.pallas_call(
        functools.partial(_even_in_kernel, seq // tm),
        grid=(n // tm,),
        in_specs=[row(d), full(gain), full(w_packed), full(w_vt), tab, tab, tab,
                  full(conv_w)],
        out_specs=(row(A_WIDTH), row(A_WIDTH), pl.BlockSpec((A_WIDTH, tm), lambda i: (0, i)),
                   row(IDX_HEADS * IDX_DIM), row(LANES), row(LANES), row(B_WIDTH)),
        out_shape=out_shape,
        scratch_shapes=[pltpu.VMEM((8, B_WIDTH), jnp.float32)],
        compiler_params=_params("arbitrary"),
        name="even_in_proj",
    )(x, gain, w_packed, w_vt, *tabs, conv_w)


def _key_to_float(key):
    bits = key ^ ((key >> 31) & jnp.int32(0x7FFFFFFF))
    return lax.bitcast_convert_type(bits, jnp.float32)


def _softmax_step_t(s_t, v_t, m_ref, acc_ref, h):
    kc = s_t.shape[0]
    m_prev = m_ref[h]
    m_new = jnp.maximum(m_prev, jnp.max(s_t, axis=0, keepdims=True))
    alpha = jnp.exp2(m_prev - m_new)
    p_t = jnp.exp2(s_t - m_new).astype(jnp.bfloat16)
    v_aug = jnp.concatenate([v_t, jnp.ones((SUM_ROWS, kc), jnp.bfloat16)], axis=0)
    acc_ref[h] = alpha * acc_ref[h] + _dot(v_aug, p_t)
    m_ref[h] = m_new


def _softmax_init(m_ref, acc_ref):
    m_ref[...] = jnp.full(m_ref.shape, NEG_BIG, jnp.float32)
    acc_ref[...] = jnp.zeros(acc_ref.shape, jnp.float32)


def _softmax_finish(acc_ref, o_ref):
    heads, rows, _ = acc_ref.shape
    dv = rows - SUM_ROWS
    o_t = jnp.concatenate(
        [acc_ref[h, 0:dv, :] / acc_ref[h, dv:dv + 1, :] for h in range(heads)], axis=0)
    o_ref[...] = o_t.astype(o_ref.dtype)


def _unrolled_loop(n, unroll, step):
    def steps(first, count):
        for t in range(count):
            step(first + t)

    def unrolled(p, carry):
        steps(unroll * p, unroll)
        return carry

    lax.fori_loop(0, n // unroll, unrolled, 0)
    done = (n // unroll) * unroll
    size = unroll // 2
    while size >= 1:
        has = (n & size) != 0
        pl.when(has)(functools.partial(steps, done, size))
        done = done + jnp.where(has, size, 0)
        size //= 2


def _attention_pipeline(n_last, half_last, first_tile, kc, unroll, issue_qk, consume,
                        issue_next_tile, s_refs):
    s0, s1 = s_refs
    pl.when(first_tile)(lambda: issue_qk(0, 0, s0, kc))

    def step(c):
        issue_qk(c, 1, s1, kc)
        consume(c, 0, s0, False, kc)
        issue_qk(c + 1, 0, s0, kc)
        consume(c, 1, s1, False, kc)

    _unrolled_loop(n_last, unroll, step)

    def last_chunk(width):
        issue_qk(n_last, 1, s1, width)
        consume(n_last, 0, s0, True, width)
        issue_next_tile(s0)
        consume(n_last, 1, s1, True, width)

    pl.when(half_last)(lambda: last_chunk(kc // 2))
    pl.when(jnp.logical_not(half_last))(lambda: last_chunk(kc))


def _dsa_kernel(top_k, q_ref, qn_ref, qi_ref, wi_ref, ki_ref, k_ref, vt_ref, o_ref,
                sc_ref, hb_ref, qm_ref, qp_ref, m_ref, acc_ref, s0_ref, s1_ref, bias_ref):
    i = pl.program_id(1)
    tq, kc, ks = DSA_Q_TILE, DSA_KEY_CHUNK, DSA_COUNT_CHUNK
    n_chunks = (i * tq) // kc + 1
    n_count = (i * tq) // ks + 1
    low = lax.broadcasted_iota(jnp.int32, (tq, LANES), 1) < IDX_DIM

    def head_halves(tile):
        t = tile.astype(jnp.float32)
        return (jnp.where(low, t, 0.0).astype(jnp.bfloat16),
                jnp.where(low, 0.0, t).astype(jnp.bfloat16))

    top = lax.broadcasted_iota(jnp.int32, (LANES, tq), 0) < IDX_DIM

    def head_halves_t(tile):
        t = tile.astype(jnp.float32).T
        return (jnp.where(top, t, 0.0).astype(jnp.bfloat16),
                jnp.where(top, 0.0, t).astype(jnp.bfloat16))

    for j in range(IDX_HEADS // 2):
        qm_ref[j, :, 0:tq], qm_ref[j, :, tq:2 * tq] = head_halves_t(
            qi_ref[:, j * LANES:(j + 1) * LANES])
    for j in range(A_HEADS // 2):
        qp_ref[2 * j], qp_ref[2 * j + 1] = head_halves_t(q_ref[:, j * LANES:(j + 1) * LANES])
    w_t = wi_ref[...].T

    q_pos = i * tq + lax.broadcasted_iota(jnp.int32, (kc, tq), 1)
    k_off = lax.broadcasted_iota(jnp.int32, (kc, tq), 0)

    def score_chunk(c):
        base = pl.multiple_of(c * kc, kc)
        kch = ki_ref[pl.ds(base, kc), :]
        acc = jnp.zeros((kc, tq), jnp.float32)
        for j in range(IDX_HEADS // 2):
            r = _dot(kch, qm_ref[j])
            acc = acc + jnp.maximum(r[:, :tq], 0.0) * w_t[2 * j:2 * j + 1, :]
            acc = acc + jnp.maximum(r[:, tq:], 0.0) * w_t[2 * j + 1:2 * j + 2, :]
        masked = jnp.where(base + k_off <= q_pos, acc, -jnp.inf)
        sc_ref[pl.ds(base, kc), :] = masked
        hb_ref[pl.ds(base, kc), :] = masked.astype(jnp.bfloat16)

    _unrolled_loop(n_chunks, DSA_SCORE_UNROLL, score_chunk)

    part = DSA_COUNT_ROWS

    def count(src_ref, pred):
        one = jnp.ones((), src_ref.dtype)

        def count_chunk(c, cnt):
            base = pl.multiple_of(c * ks, ks)
            for r in range(ks // part):
                row0 = base + r * part
                cnt = jnp.where(pred(src_ref[pl.ds(row0, part), :], row0), cnt + one, cnt)
            return cnt

        cnt = lax.fori_loop(0, n_count, count_chunk, jnp.zeros((part, tq), src_ref.dtype))
        return jnp.sum(cnt.astype(jnp.float32), axis=0, keepdims=True)

    def coarse(step, lo):
        cand = lo + jnp.left_shift(jnp.int32(1), 31 - step)
        cand_b = jnp.broadcast_to(_key_to_float(cand).astype(jnp.bfloat16), (part, tq))
        total = count(hb_ref, lambda blk, row0: blk >= cand_b)
        return jnp.where(total >= top_k, cand, lo)

    k_coarse = lax.fori_loop(
        0, 16, coarse, jnp.full((1, tq), jnp.iinfo(jnp.int32).min, jnp.int32))

    lo0 = jnp.maximum(k_coarse, KEY_LOWEST_FINITE + COARSE_SLACK) - COARSE_SLACK
    hi0 = jnp.maximum(k_coarse + 2 * COARSE_SLACK, lo0)

    def fine(step, carry):
        lo, hi = carry
        mid = lo + jnp.right_shift(hi - lo + 1, 1)
        mid_f = _key_to_float(mid)
        take = count(sc_ref, lambda blk, row0: blk >= mid_f) >= top_k
        return jnp.where(take, mid, lo), jnp.where(take, hi, mid - 1)

    lo, _ = lax.fori_loop(0, (3 * COARSE_SLACK).bit_length(), fine, (lo0, hi0))
    thr = _key_to_float(lo)
    n_ge = count(sc_ref, lambda blk, row0: blk >= thr)

    @pl.when(jnp.max(n_ge) > top_k)
    def _():
        n_gt = count(sc_ref, lambda blk, row0: blk > thr)
        need = top_k - n_gt
        row = lax.broadcasted_iota(jnp.int32, (part, tq), 0)
        index_bits = (sc_ref.shape[0] - 1).bit_length()

        def index_search(step, cut):
            cand = cut + jnp.left_shift(jnp.int32(1), index_bits - 1 - step)
            below = count(sc_ref, lambda blk, row0: (blk == thr) & (row0 + row < cand))
            return jnp.where(below < need, cand, cut)

        cut = lax.fori_loop(0, index_bits, index_search, jnp.zeros((1, tq), jnp.int32))

        def drop_chunk(c, carry):
            base = pl.multiple_of(c * ks, ks)
            blk = sc_ref[pl.ds(base, ks), :]
            pos = base + lax.broadcasted_iota(jnp.int32, (ks, tq), 0)
            sc_ref[pl.ds(base, ks), :] = jnp.where((blk == thr) & (pos > cut), -jnp.inf, blk)
            return carry

        lax.fori_loop(0, n_count, drop_chunk, 0)

    _softmax_init(m_ref, acc_ref)

    group = A_HEADS // 2

    def issue_qk(c, g, s_ref, width):
        base = pl.multiple_of(c * kc, kc)
        for u in range(group):
            h = g * group + u
            cols = slice((h // 2) * LANES, (h // 2 + 1) * LANES)
            s_ref[u, 0:width, :] = _dot(k_ref[pl.ds(base, width), cols], qp_ref[h])

    def consume(c, g, s_ref, last, width):
        base = pl.multiple_of(c * kc, kc)
        if g == 0:
            bias_ref[0:width, :] = jnp.where(
                sc_ref[pl.ds(base, width), :] >= thr, 0.0, NEG_BIG)
        for u in range(group):
            h = g * group + u
            v_t = vt_ref[h * A_HEAD_DIM:(h + 1) * A_HEAD_DIM, pl.ds(base, width)]
            s_t = s_ref[u, 0:width, :] + bias_ref[0:width, :]
            _softmax_step_t(s_t, v_t, m_ref, acc_ref, h)

    def issue_next_tile(s_ref):
        for j in range(group // 2):
            halves = head_halves(qn_ref[:, j * LANES:(j + 1) * LANES])
            for u in range(2):
                s_ref[2 * j + u] = _dot_nt(k_ref[0:kc, j * LANES:(j + 1) * LANES], halves[u])

    half_last = (i + 1) * tq - (n_chunks - 1) * kc <= kc // 2
    _attention_pipeline(n_chunks - 1, half_last, i == 0, kc, DSA_UNROLL, issue_qk, consume,
                        issue_next_tile, (s0_ref, s1_ref))
    _softmax_finish(acc_ref, o_ref)


def _dsa_attention(q, qi, wi, ki, k, vt, batch, seq):
    n = q.shape[0]
    tq = DSA_Q_TILE
    nq = seq // tq
    top_k = min(TOPK_MAX, seq // 4)
    qrow = lambda width: pl.BlockSpec((tq, width), lambda b, i: (b * nq + i, 0))
    q_next = pl.BlockSpec((tq, A_WIDTH), lambda b, i: (b * nq + jnp.minimum(i + 1, nq - 1), 0))
    per_batch = lambda width: pl.BlockSpec((seq, width), lambda b, i: (b, 0))
    return pl.pallas_call(
        functools.partial(_dsa_kernel, top_k),
        grid=(batch, nq),
        in_specs=[qrow(A_WIDTH), q_next, qrow(IDX_HEADS * IDX_DIM), qrow(LANES),
                  per_batch(LANES), per_batch(A_WIDTH),
                  pl.BlockSpec((A_WIDTH, seq), lambda b, i: (0, b))],
        out_specs=pl.BlockSpec((A_WIDTH, tq), lambda b, i: (0, b * nq + i)),
        out_shape=jax.ShapeDtypeStruct((A_WIDTH, n), jnp.bfloat16),
        scratch_shapes=[
            pltpu.VMEM((seq, tq), jnp.float32),
            pltpu.VMEM((seq, tq), jnp.bfloat16),
            pltpu.VMEM((IDX_HEADS // 2, LANES, 2 * tq), jnp.bfloat16),
            pltpu.VMEM((A_HEADS, LANES, tq), jnp.bfloat16),
            pltpu.VMEM((A_HEADS, 1, tq), jnp.float32),
            pltpu.VMEM((A_HEADS, A_HEAD_DIM + SUM_ROWS, tq), jnp.float32),
            pltpu.VMEM((A_HEADS // 2, DSA_KEY_CHUNK, tq), jnp.float32),
            pltpu.VMEM((A_HEADS // 2, DSA_KEY_CHUNK, tq), jnp.float32),
            pltpu.VMEM((DSA_KEY_CHUNK, tq), jnp.float32),
        ],
        compiler_params=_params("arbitrary", "arbitrary"),
        name="dsa_attention",
    )(q, q, qi, wi, ki, k, vt)


def _dot_tn(a_t, b):
    return lax.dot_general(a_t, b, (((0,), (0,)), ((), ())), preferred_element_type=jnp.float32)


def _mix_mlp_kernel(n_in, *refs):
    a_refs, wo_refs = refs[:n_in], refs[n_in:2 * n_in]
    (x_ref, gmix_ref, gpre_ref, gpost_ref, w1_ref, w2_ref, o_ref,
     x1_ref, hn_ref, acc_ref) = refs[2 * n_in:]
    f = pl.program_id(1)

    @pl.when(f == 0)
    def _():
        y = _dot_tn(a_refs[0][...], wo_refs[0][...])
        for a_ref, wo_ref in zip(a_refs[1:], wo_refs[1:]):
            y = y + _dot(a_ref[...], wo_ref[...])
        x1 = x_ref[...] + _rms(y, gmix_ref[...])
        x1_ref[...] = x1
        hn_ref[...] = _rms(x1, gpre_ref[...]).astype(jnp.bfloat16)
        acc_ref[...] = jnp.zeros(acc_ref.shape, jnp.float32)

    h = jnp.maximum(_dot(hn_ref[...], w1_ref[...]), 0.0)
    acc_ref[...] += _dot((h * h).astype(jnp.bfloat16), w2_ref[...])

    @pl.when(f == pl.num_programs(1) - 1)
    def _():
        o_ref[...] = x1_ref[...] + _rms(acc_ref[...], gpost_ref[...])


def _mix_mlp(acts, w_outs, x, g_mix, gpre, gpost, w1, w2):
    n, d = x.shape
    ff = w1.shape[1]
    tm, tf = MLP_ROW_TILE, MLP_FF_TILE
    row = lambda a: pl.BlockSpec((tm, a.shape[1]), lambda i, f: (i, 0))
    col = lambda a: pl.BlockSpec((a.shape[0], tm), lambda i, f: (0, i))
    full = lambda a: pl.BlockSpec(a.shape, lambda i, f: (0,) * a.ndim)
    return pl.pallas_call(
        functools.partial(_mix_mlp_kernel, len(acts)),
        grid=(n // tm, ff // tf),
        in_specs=[col(acts[0])] + [row(a) for a in acts[1:]] + [full(w) for w in w_outs]
        + [row(x), full(g_mix), full(gpre), full(gpost),
           pl.BlockSpec((d, tf), lambda i, f: (0, f)),
           pl.BlockSpec((tf, d), lambda i, f: (f, 0))],
        out_specs=row(x),
        out_shape=jax.ShapeDtypeStruct((n, d), jnp.float32),
        scratch_shapes=[pltpu.VMEM((tm, d), jnp.float32), pltpu.VMEM((tm, d), jnp.bfloat16),
                        pltpu.VMEM((tm, d), jnp.float32)],
        compiler_params=_params("arbitrary", "arbitrary"),
        name="mix_mlp",
    )(*acts, *w_outs, x, g_mix, gpre, gpost, w1, w2)


_DKR, _DKV, _DEND = C_Q_RANK, C_Q_RANK + LANES, C_Q_RANK + LANES + C_KV_RANK


def _mla_in_kernel(x_ref, g_ref, wd_ref, qn_ref, kvn_ref, wuq_ref, wuk_ref, wuvt_ref,
                   c_ref, s_ref, q_ref, k_ref, vt_ref):
    hn = _rms(x_ref[...], g_ref[...]).astype(jnp.bfloat16)
    c, s = c_ref[...].T, s_ref[...].T
    scale = (C_NOPE_DIM + C_ROPE_DIM) ** -0.5 * LOG2E
    cq, sq = c * scale, s * scale

    def rope(y, ct, st):
        return y * ct + pltpu.roll(y, LANES - C_ROPE_DIM, axis=1) * st

    kvd = _dot(hn, wd_ref[:, _DKV:_DEND])
    qr = _dot(hn, wd_ref[:, 0:_DKV])
    ckv = _rms(kvd, kvn_ref[...]).astype(jnp.bfloat16)
    vt_ref[...] = _dot_nt(wuvt_ref[...], ckv).astype(vt_ref.dtype)
    kr = rope(qr[:, _DKR:], c, s)
    n_wide = C_HEADS * LANES // MXU_COLS
    for j in range(n_wide):
        k2 = _dot(ckv, wuk_ref[:, j * MXU_COLS:(j + 1) * MXU_COLS])
        for u in range(MXU_COLS // LANES):
            cols = slice(j * MXU_COLS + u * LANES, j * MXU_COLS + (u + 1) * LANES)
            k_ref[:, cols] = (k2[:, u * LANES:(u + 1) * LANES] + kr).astype(k_ref.dtype)

    qa = _rms(qr[:, :_DKR], qn_ref[...]).astype(jnp.bfloat16)
    up = lambda j: _dot(qa, wuq_ref[:, j * MXU_COLS:(j + 1) * MXU_COLS])
    nxt = up(0)
    for j in range(n_wide):
        q2 = nxt
        if j + 1 < n_wide:
            nxt = up(j + 1)
        for u in range(MXU_COLS // LANES):
            cols = slice(j * MXU_COLS + u * LANES, j * MXU_COLS + (u + 1) * LANES)
            q_ref[:, cols] = rope(q2[:, u * LANES:(u + 1) * LANES], cq, sq).astype(q_ref.dtype)


def _mla_in_proj(x, gain, w_down, q_norm, kv_norm, w_uq, w_uk, w_uvt, tabs):
    n, d = x.shape
    tm = MLA_ROW_TILE
    row = lambda width: pl.BlockSpec((tm, width), lambda i: (i, 0))
    full = lambda a: pl.BlockSpec(a.shape, lambda i: (0,) * a.ndim)
    tab = pl.BlockSpec((LANES, tm), lambda i: (0, i))
    bf = jnp.bfloat16
    return pl.pallas_call(
        _mla_in_kernel,
        grid=(n // tm,),
        in_specs=[row(d), full(gain), full(w_down), full(q_norm), full(kv_norm),
                  full(w_uq), full(w_uk), full(w_uvt), tab, tab],
        out_specs=(row(C_HEADS * LANES), row(C_HEADS * LANES),
                   pl.BlockSpec((C_HEADS * C_V_DIM, tm), lambda i: (0, i))),
        out_shape=(jax.ShapeDtypeStruct((n, C_HEADS * LANES), bf),
                   jax.ShapeDtypeStruct((n, C_HEADS * LANES), bf),
                   jax.ShapeDtypeStruct((C_HEADS * C_V_DIM, n), bf)),
        compiler_params=_params("arbitrary"),
        name="mla_in_proj",
    )(x, gain, w_down, q_norm, kv_norm, w_uq, w_uk, w_uvt, *tabs)


def _mla_attn_kernel(q_ref, qn_ref, k_ref, vt_ref, o_ref,
                     qt_ref, m_ref, acc_ref, s0_ref, s1_ref):
    i = pl.program_id(2)
    tq, kc, group = MLA_Q_TILE, MLA_KEY_CHUNK, MLA_HEAD_GROUP // 2
    n_full = (i * tq) // kc
    _softmax_init(m_ref, acc_ref)
    for h in range(MLA_HEAD_GROUP):
        qt_ref[h] = q_ref[:, h * LANES:(h + 1) * LANES].astype(jnp.float32).T.astype(
            jnp.bfloat16)

    def issue_qk(c, g, s_ref, width):
        base = pl.multiple_of(c * kc, kc)
        for u in range(group):
            h = g * group + u
            s_ref[u, 0:width, :] = _dot(
                k_ref[pl.ds(base, width), h * LANES:(h + 1) * LANES], qt_ref[h])

    def consume(c, g, s_ref, last, width):
        base = pl.multiple_of(c * kc, kc)
        if last:
            q_pos = i * tq + lax.broadcasted_iota(jnp.int32, (width, tq), 1)
            k_pos = base + lax.broadcasted_iota(jnp.int32, (width, tq), 0)
            bias_t = jnp.where(k_pos <= q_pos, 0.0, NEG_BIG)
        for u in range(group):
            h = g * group + u
            s_t = s_ref[u, 0:width, :]
            if last:
                s_t = s_t + bias_t
            v_t = vt_ref[h * C_V_DIM:(h + 1) * C_V_DIM, pl.ds(base, width)]
            _softmax_step_t(s_t, v_t, m_ref, acc_ref, h)

    def issue_next_tile(s_ref):
        for u in range(group):
            cols = slice(u * LANES, (u + 1) * LANES)
            s_ref[u] = _dot_nt(k_ref[0:kc, cols], qn_ref[:, cols])

    half_last = (i + 1) * tq - n_full * kc <= kc // 2
    _attention_pipeline(n_full, half_last, i == 0, kc, MLA_UNROLL, issue_qk, consume,
                        issue_next_tile, (s0_ref, s1_ref))
    _softmax_finish(acc_ref, o_ref)


def _mla_attention(q, k, vt, batch, seq):
    n = q.shape[0]
    tq, hg = MLA_Q_TILE, MLA_HEAD_GROUP
    nq = seq // tq
    return pl.pallas_call(
        _mla_attn_kernel,
        grid=(batch, C_HEADS // hg, nq),
        in_specs=[pl.BlockSpec((tq, hg * LANES), lambda b, g, i: (b * nq + i, g)),
                  pl.BlockSpec((tq, hg * LANES),
                               lambda b, g, i: (b * nq + jnp.minimum(i + 1, nq - 1), g)),
                  pl.BlockSpec((seq, hg * LANES), lambda b, g, i: (b, g)),
                  pl.BlockSpec((hg * C_V_DIM, seq), lambda b, g, i: (g, b))],
        out_specs=pl.BlockSpec((hg * C_V_DIM, tq), lambda b, g, i: (g, b * nq + i)),
        out_shape=jax.ShapeDtypeStruct((C_HEADS * C_V_DIM, n), jnp.bfloat16),
        scratch_shapes=[pltpu.VMEM((hg, LANES, tq), jnp.bfloat16),
                        pltpu.VMEM((hg, 1, tq), jnp.float32),
                        pltpu.VMEM((hg, C_V_DIM + SUM_ROWS, tq), jnp.float32),
                        pltpu.VMEM((hg // 2, MLA_KEY_CHUNK, tq), jnp.float32),
                        pltpu.VMEM((hg // 2, MLA_KEY_CHUNK, tq), jnp.float32)],
        compiler_params=_params("arbitrary", "arbitrary", "arbitrary"),
        name="mla_attention",
    )(q, q, k, vt)


def _rope_tables(positions, rot_dim, lead, period):
    inv_freq = ROPE_THETA ** (-jnp.arange(0, rot_dim, 2, dtype=jnp.float32) / rot_dim)
    ang = inv_freq[:, None] * positions.astype(jnp.float32).reshape(1, -1)
    cos, sin = jnp.cos(ang), jnp.sin(ang)
    half, n = cos.shape
    ones = lambda w: jnp.ones((w, n), jnp.float32)
    zeros = lambda w: jnp.zeros((w, n), jnp.float32)
    tail = period - lead - rot_dim
    c = jnp.concatenate([ones(lead), cos, cos, ones(tail)], axis=0)
    s_next = jnp.concatenate([zeros(lead), -sin, zeros(half + tail)], axis=0)
    s_prev = jnp.concatenate([zeros(lead + half), sin, zeros(tail)], axis=0)
    reps = LANES // period
    return tuple(jnp.tile(t, (reps, 1)) for t in (c, s_next, s_prev))


def _pack_even_w_in(w):
    d = w.shape[0]
    q, k, v, qi, ki, wi, gb, gc, xi = jnp.split(
        w, [512, 1024, 1536, 2560, 2624, 2640, 3152, 3664], axis=1)
    pad = jnp.zeros((d, LANES - IDX_HEADS), w.dtype)
    packed = jnp.concatenate([q, k, qi, ki, ki, wi, pad, gb, gc, xi], axis=1)
    return packed.astype(jnp.bfloat16), v.T.astype(jnp.bfloat16)


def _pad_heads(w, heads, width):
    r = w.shape[0]
    w = w.reshape(r, heads, width)
    return jnp.pad(w, ((0, 0), (0, 0), (0, LANES - width))).reshape(r, heads * LANES)


def _rope_swapped(w_rope):
    half = C_ROPE_DIM // 2
    return jnp.concatenate([w_rope, w_rope[..., half:], w_rope[..., :half]], axis=-1)


def _mla_rope_tables(positions):
    inv_freq = ROPE_THETA ** (-jnp.arange(0, C_ROPE_DIM, 2, dtype=jnp.float32) / C_ROPE_DIM)
    ang = inv_freq[:, None] * positions.astype(jnp.float32).reshape(1, -1)
    cos, sin = jnp.cos(ang), jnp.sin(ang)
    n = cos.shape[1]
    tail = jnp.zeros((LANES - C_NOPE_DIM - C_ROPE_DIM, n), jnp.float32)
    c = jnp.concatenate([jnp.ones((C_NOPE_DIM, n), jnp.float32), cos, cos, tail], axis=0)
    s = jnp.concatenate([jnp.zeros((C_NOPE_DIM, n), jnp.float32), -sin, sin, tail], axis=0)
    return c, s


def kernel(x, positions, norm_mix_pre, norm_mix_post, norm_ffn_pre, norm_ffn_post, even_w_in, even_conv_w, even_w_out, odd_w_dq, odd_q_norm, odd_w_uq, odd_w_dkv, odd_kv_norm, odd_w_ukv, odd_w_o, mlp_w1, mlp_w2):
    batch, seq, d = x.shape
    depth = norm_mix_pre.shape[0]
    bf = jnp.bfloat16
    tabs_a = _rope_tables(positions, A_ROT_DIM, 0, A_HEAD_DIM)
    tabs_c = _mla_rope_tables(positions)
    h = x.reshape(batch * seq, d)
    row = lambda a: a.reshape(1, -1)
    for layer in range(depth):
        j = layer // 2
        g_pre, g_post = row(norm_mix_pre[layer]), row(norm_mix_post[layer])
        if layer % 2 == 0:
            w_packed, w_vt = _pack_even_w_in(even_w_in[j])
            q, k, vt, qi, ki, wi, conv = _even_in_proj(
                h, g_pre, w_packed, w_vt, tabs_a, even_conv_w[j], seq)
            attn = _dsa_attention(q, qi, wi, ki, k, vt, batch, seq)
            w_out = even_w_out[j].astype(bf)
            acts, w_outs = [attn, conv], [w_out[:A_WIDTH], w_out[A_WIDTH:]]
        else:
            w_dkv = odd_w_dkv[j]
            w_down = jnp.concatenate(
                [odd_w_dq[j], jnp.zeros((d, C_NOPE_DIM), w_dkv.dtype),
                 _rope_swapped(w_dkv[:, C_KV_RANK:]), w_dkv[:, :C_KV_RANK]], axis=1)
            w_ukv = odd_w_ukv[j].reshape(C_KV_RANK, C_HEADS, C_NOPE_DIM + C_V_DIM)
            w_uk = _pad_heads(w_ukv[:, :, :C_NOPE_DIM].reshape(C_KV_RANK, -1), C_HEADS, C_NOPE_DIM)
            w_uv = w_ukv[:, :, C_NOPE_DIM:].reshape(C_KV_RANK, -1)
            w_uq = odd_w_uq[j].reshape(C_Q_RANK, C_HEADS, C_NOPE_DIM + C_ROPE_DIM)
            w_uq = jnp.concatenate(
                [w_uq[:, :, :C_NOPE_DIM], _rope_swapped(w_uq[:, :, C_NOPE_DIM:])],
                axis=-1).reshape(C_Q_RANK, C_HEADS * LANES)
            q, k, vt = _mla_in_proj(h, g_pre, w_down.astype(bf), row(odd_q_norm[j]),
                                    row(odd_kv_norm[j]), w_uq.astype(bf), w_uk.astype(bf),
                                    w_uv.T.astype(bf), tabs_c)
            acts, w_outs = [_mla_attention(q, k, vt, batch, seq)], [odd_w_o[j].astype(bf)]
        h = _mix_mlp(acts, w_outs, h, g_post, row(norm_ffn_pre[layer]),
                     row(norm_ffn_post[layer]), mlp_w1[layer].astype(bf),
                     mlp_w2[layer].astype(bf))
    return h.reshape(batch, seq, d)
```

```python
import functools

import jax
import jax.numpy as jnp
from jax import lax
from jax.experimental import pallas as pl
from jax.experimental.pallas import tpu as pltpu

ROPE_THETA = 500000.0
NORM_EPS = 1e-6
LANES = 128
MXU_COLS = 256
NEG_BIG = -1e30
LOG2E = 1.4426950408889634
SUM_ROWS = 16

A_HEADS = 8
A_HEAD_DIM = 64
A_WIDTH = A_HEADS * A_HEAD_DIM
A_ROT_DIM = A_HEAD_DIM // 4
IDX_HEADS = 16
IDX_DIM = 64
TOPK_MAX = 256
B_WIDTH = 512
CONV_WIDTH = 3
C_HEADS = 16
C_NOPE_DIM = 64
C_ROPE_DIM = 32
C_V_DIM = 64
C_Q_RANK = 384
C_KV_RANK = 256

KEY_LOWEST_FINITE = -2139095040
COARSE_SLACK = 0x8000

ROW_TILE = 512
MLA_ROW_TILE = 512
MLP_ROW_TILE = 512
MLP_FF_TILE = 2048
DSA_Q_TILE = 256
DSA_KEY_CHUNK = 512
DSA_COUNT_CHUNK = 256
DSA_COUNT_ROWS = 64
DSA_UNROLL = 4
DSA_SCORE_UNROLL = 4
MLA_UNROLL = 4
MLA_Q_TILE = 256
MLA_KEY_CHUNK = 512
MLA_HEAD_GROUP = 8
VMEM_LIMIT = 56 * 1024 * 1024


def _dot(a, b):
    return jnp.dot(a, b, preferred_element_type=jnp.float32)


def _dot_nt(a, b):
    return lax.dot_general(a, b, (((1,), (1,)), ((), ())), preferred_element_type=jnp.float32)


def _rms(x, gain):
    ms = jnp.mean(x * x, axis=-1, keepdims=True)
    return x * lax.rsqrt(ms + NORM_EPS) * gain


def _rope_tile(x, c, s_next, s_prev, half):
    return (x * c + pltpu.roll(x, LANES - half, axis=1) * s_next
            + pltpu.roll(x, half, axis=1) * s_prev)


def _params(*semantics):
    return pltpu.CompilerParams(dimension_semantics=semantics, vmem_limit_bytes=VMEM_LIMIT)


_EQ, _EK, _EQI, _EKI, _EWI, _EGB, _EGC, _EXI, _EEND = (
    0, 512, 1024, 2048, 2176, 2304, 2816, 3328, 3840)


def _even_in_kernel(seq_tiles, x_ref, g_ref, w_ref, wvt_ref, c_ref, sn_ref, sp_ref, cw_ref,
                    q_ref, k_ref, vt_ref, qi_ref, ki_ref, wi_ref, conv_ref, carry_ref):
    i = pl.program_id(0)
    tm = x_ref.shape[0]
    hn = _rms(x_ref[...], g_ref[...]).astype(jnp.bfloat16)
    c, sn, sp = c_ref[...].T, sn_ref[...].T, sp_ref[...].T
    half = A_ROT_DIM // 2

    def rope_section(start, width, out_ref, scale=None):
        for t in range(width // MXU_COLS):
            y2 = _dot(hn, w_ref[:, start + t * MXU_COLS:start + (t + 1) * MXU_COLS])
            for u in range(MXU_COLS // LANES):
                y = _rope_tile(y2[:, u * LANES:(u + 1) * LANES], c, sn, sp, half)
                if scale is not None:
                    y = y * scale
                lo = t * MXU_COLS + u * LANES
                out_ref[:, lo:lo + LANES] = y.astype(out_ref.dtype)

    gate_b = _dot(hn, w_ref[:, _EGB:_EGC])
    u = _dot(hn, w_ref[:, _EGC:_EXI]) * _dot(hn, w_ref[:, _EXI:_EEND])
    seq_start = (i % seq_tiles) == 0
    prev = jnp.where(seq_start, 0.0, carry_ref[...])
    carry_ref[...] = u[tm - 8:, :]
    row = lax.broadcasted_iota(jnp.int32, u.shape, 0)
    u1 = jnp.where(row == 0, prev[7:8, :], pltpu.roll(u, 1, axis=0))
    u2 = pltpu.roll(u, 2, axis=0)
    u2 = jnp.where(row == 0, prev[6:7, :], jnp.where(row == 1, prev[7:8, :], u2))
    y = cw_ref[0:1, :] * u2 + cw_ref[1:2, :] * u1 + cw_ref[2:3, :] * u
    conv_ref[...] = (gate_b * y).astype(conv_ref.dtype)

    rope_section(_EQ, A_WIDTH, q_ref, A_HEAD_DIM ** -0.5 * LOG2E)
    rope_section(_EK, A_WIDTH, k_ref)
    rope_section(_EQI, IDX_HEADS * IDX_DIM, qi_ref)
    kw = _dot(hn, w_ref[:, _EKI:_EGB])
    ki_ref[...] = _rope_tile(kw[:, :LANES], c, sn, sp, half).astype(ki_ref.dtype)
    wi_ref[...] = kw[:, LANES:] * (IDX_HEADS ** -0.5 * IDX_DIM ** -0.5)
    vt_ref[...] = _dot_nt(wvt_ref[...], hn).astype(vt_ref.dtype)


def _even_in_proj(x, gain, w_packed, w_vt, tabs, conv_w, seq):
    n, d = x.shape
    tm = ROW_TILE
    row = lambda width: pl.BlockSpec((tm, width), lambda i: (i, 0))
    full = lambda a: pl.BlockSpec(a.shape, lambda i: (0,) * a.ndim)
    tab = pl.BlockSpec((LANES, tm), lambda i: (0, i))
    bf = jnp.bfloat16
    out_shape = (
        jax.ShapeDtypeStruct((n, A_WIDTH), bf), jax.ShapeDtypeStruct((n, A_WIDTH), bf),
        jax.ShapeDtypeStruct((A_WIDTH, n), bf), jax.ShapeDtypeStruct((n, IDX_HEADS * IDX_DIM), bf),
        jax.ShapeDtypeStruct((n, LANES), bf), jax.ShapeDtypeStruct((n, LANES), jnp.float32),
        jax.ShapeDtypeStruct((n, B_WIDTH), bf))
    return pl.pallas_call(
        functools.partial(_even_in_kernel, seq // tm),
        grid=(n // tm,),
        in_specs=[row(d), full(gain), full(w_packed), full(w_vt), tab, tab, tab,
                  full(conv_w)],
        out_specs=(row(A_WIDTH), row(A_WIDTH), pl.BlockSpec((A_WIDTH, tm), lambda i: (0, i)),
                   row(IDX_HEADS * IDX_DIM), row(LANES), row(LANES), row(B_WIDTH)),
        out_shape=out_shape,
        scratch_shapes=[pltpu.VMEM((8, B_WIDTH), jnp.float32)],
        compiler_params=_params("arbitrary"),
        name="even_in_proj",
    )(x, gain, w_packed, w_vt, *tabs, conv_w)


def _key_to_float(key):
    bits = key ^ ((key >> 31) & jnp.int32(0x7FFFFFFF))
    return lax.bitcast_convert_type(bits, jnp.float32)


def _softmax_step_t(s_t, v_t, m_ref, acc_ref, h):
    kc = s_t.shape[0]
    m_prev = m_ref[h]
    m_new = jnp.maximum(m_prev, jnp.max(s_t, axis=0, keepdims=True))
    alpha = jnp.exp2(m_prev - m_new)
    p_t = jnp.exp2(s_t - m_new).astype(jnp.bfloat16)
    v_aug = jnp.concatenate([v_t, jnp.ones((SUM_ROWS, kc), jnp.bfloat16)], axis=0)
    acc_ref[h] = alpha * acc_ref[h] + _dot(v_aug, p_t)
    m_ref[h] = m_new


def _softmax_init(m_ref, acc_ref):
    m_ref[...] = jnp.full(m_ref.shape, NEG_BIG, jnp.float32)
    acc_ref[...] = jnp.zeros(acc_ref.shape, jnp.float32)


def _softmax_finish(acc_ref, o_ref):
    heads, rows, _ = acc_ref.shape
    dv = rows - SUM_ROWS
    o_t = jnp.concatenate(
        [acc_ref[h, 0:dv, :] / acc_ref[h, dv:dv + 1, :] for h in range(heads)], axis=0)
    o_ref[...] = o_t.astype(o_ref.dtype)


def _unrolled_loop(n, unroll, step):
    def steps(first, count):
        for t in range(count):
            step(first + t)

    def unrolled(p, carry):
        steps(unroll * p, unroll)
        return carry

    lax.fori_loop(0, n // unroll, unrolled, 0)
    done = (n // unroll) * unroll
    size = unroll // 2
    while size >= 1:
        has = (n & size) != 0
        pl.when(has)(functools.partial(steps, done, size))
        done = done + jnp.where(has, size, 0)
        size //= 2


def _attention_pipeline(n_last, half_last, first_tile, kc, unroll, issue_qk, consume,
                        issue_next_tile, s_refs):
    s0, s1 = s_refs
    pl.when(first_tile)(lambda: issue_qk(0, 0, s0, kc))

    def step(c):
        issue_qk(c, 1, s1, kc)
        consume(c, 0, s0, False, kc)
        issue_qk(c + 1, 0, s0, kc)
        consume(c, 1, s1, False, kc)

    _unrolled_loop(n_last, unroll, step)

    def last_chunk(width):
        issue_qk(n_last, 1, s1, width)
        consume(n_last, 0, s0, True, width)
        issue_next_tile(s0)
        consume(n_last, 1, s1, True, width)

    pl.when(half_last)(lambda: last_chunk(kc // 2))
    pl.when(jnp.logical_not(half_last))(lambda: last_chunk(kc))


def _dsa_kernel(top_k, q_ref, qn_ref, qi_ref, wi_ref, ki_ref, k_ref, vt_ref, o_ref,
                sc_ref, hb_ref, qm_ref, qp_ref, m_ref, acc_ref, s0_ref, s1_ref, bias_ref):
    i = pl.program_id(1)
    tq, kc, ks = DSA_Q_TILE, DSA_KEY_CHUNK, DSA_COUNT_CHUNK
    n_chunks = (i * tq) // kc + 1
    n_count = (i * tq) // ks + 1
    low = lax.broadcasted_iota(jnp.int32, (tq, LANES), 1) < IDX_DIM

    def head_halves(tile):
        t = tile.astype(jnp.float32)
        return (jnp.where(low, t, 0.0).astype(jnp.bfloat16),
                jnp.where(low, 0.0, t).astype(jnp.bfloat16))

    top = lax.broadcasted_iota(jnp.int32, (LANES, tq), 0) < IDX_DIM

    def head_halves_t(tile):
        t = tile.astype(jnp.float32).T
        return (jnp.where(top, t, 0.0).astype(jnp.bfloat16),
                jnp.where(top, 0.0, t).astype(jnp.bfloat16))

    for j in range(IDX_HEADS // 2):
        qm_ref[j, :, 0:tq], qm_ref[j, :, tq:2 * tq] = head_halves_t(
            qi_ref[:, j * LANES:(j + 1) * LANES])
    for j in range(A_HEADS // 2):
        qp_ref[2 * j], qp_ref[2 * j + 1] = head_halves_t(q_ref[:, j * LANES:(j + 1) * LANES])
    w_t = wi_ref[...].T

    q_pos = i * tq + lax.broadcasted_iota(jnp.int32, (kc, tq), 1)
    k_off = lax.broadcasted_iota(jnp.int32, (kc, tq), 0)

    def score_chunk(c):
        base = pl.multiple_of(c * kc, kc)
        kch = ki_ref[pl.ds(base, kc), :]
        acc = jnp.zeros((kc, tq), jnp.float32)
        for j in range(IDX_HEADS // 2):
            r = _dot(kch, qm_ref[j])
            acc = acc + jnp.maximum(r[:, :tq], 0.0) * w_t[2 * j:2 * j + 1, :]
            acc = acc + jnp.maximum(r[:, tq:], 0.0) * w_t[2 * j + 1:2 * j + 2, :]
        masked = jnp.where(base + k_off <= q_pos, acc, -jnp.inf)
        sc_ref[pl.ds(base, kc), :] = masked
        hb_ref[pl.ds(base, kc), :] = masked.astype(jnp.bfloat16)

    _unrolled_loop(n_chunks, DSA_SCORE_UNROLL, score_chunk)

    part = DSA_COUNT_ROWS

    def count(src_ref, pred):
        one = jnp.ones((), src_ref.dtype)

        def count_chunk(c, cnt):
            base = pl.multiple_of(c * ks, ks)
            for r in range(ks // part):
                row0 = base + r * part
                cnt = jnp.where(pred(src_ref[pl.ds(row0, part), :], row0), cnt + one, cnt)
            return cnt

        cnt = lax.fori_loop(0, n_count, count_chunk, jnp.zeros((part, tq), src_ref.dtype))
        return jnp.sum(cnt.astype(jnp.float32), axis=0, keepdims=True)

    def coarse(step, lo):
        cand = lo + jnp.left_shift(jnp.int32(1), 31 - step)
        cand_b = jnp.broadcast_to(_key_to_float(cand).astype(jnp.bfloat16), (part, tq))
        total = count(hb_ref, lambda blk, row0: blk >= cand_b)
        return jnp.where(total >= top_k, cand, lo)

    k_coarse = lax.fori_loop(
        0, 16, coarse, jnp.full((1, tq), jnp.iinfo(jnp.int32).min, jnp.int32))

    lo0 = jnp.maximum(k_coarse, KEY_LOWEST_FINITE + COARSE_SLACK) - COARSE_SLACK
    hi0 = jnp.maximum(k_coarse + 2 * COARSE_SLACK, lo0)

    def fine(step, carry):
        lo, hi = carry
        mid = lo + jnp.right_shift(hi - lo + 1, 1)
        mid_f = _key_to_float(mid)
        take = count(sc_ref, lambda blk, row0: blk >= mid_f) >= top_k
        return jnp.where(take, mid, lo), jnp.where(take, hi, mid - 1)

    lo, _ = lax.fori_loop(0, (3 * COARSE_SLACK).bit_length(), fine, (lo0, hi0))
    thr = _key_to_float(lo)
    n_ge = count(sc_ref, lambda blk, row0: blk >= thr)

    @pl.when(jnp.max(n_ge) > top_k)
    def _():
        n_gt = count(sc_ref, lambda blk, row0: blk > thr)
        need = top_k - n_gt
        row = lax.broadcasted_iota(jnp.int32, (part, tq), 0)
        index_bits = (sc_ref.shape[0] - 1).bit_length()

        def index_search(step, cut):
            cand = cut + jnp.left_shift(jnp.int32(1), index_bits - 1 - step)
            below = count(sc_ref, lambda blk, row0: (blk == thr) & (row0 + row < cand))
            return jnp.where(below < need, cand, cut)

        cut = lax.fori_loop(0, index_bits, index_search, jnp.zeros((1, tq), jnp.int32))

        def drop_chunk(c, carry):
            base = pl.multiple_of(c * ks, ks)
            blk = sc_ref[pl.ds(base, ks), :]
            pos = base + lax.broadcasted_iota(jnp.int32, (ks, tq), 0)
            sc_ref[pl.ds(base, ks), :] = jnp.where((blk == thr) & (pos > cut), -jnp.inf, blk)
            return carry

        lax.fori_loop(0, n_count, drop_chunk, 0)

    _softmax_init(m_ref, acc_ref)

    group = A_HEADS // 2

    def issue_qk(c, g, s_ref, width):
        base = pl.multiple_of(c * kc, kc)
        for u in range(group):
            h = g * group + u
            cols = slice((h // 2) * LANES, (h // 2 + 1) * LANES)
            s_ref[u, 0:width, :] = _dot(k_ref[pl.ds(base, width), cols], qp_ref[h])

    def consume(c, g, s_ref, last, width):
        base = pl.multiple_of(c * kc, kc)
        if g == 0:
            bias_ref[0:width, :] = jnp.where(
                sc_ref[pl.ds(base, width), :] >= thr, 0.0, NEG_BIG)
        for u in range(group):
            h = g * group + u
            v_t = vt_ref[h * A_HEAD_DIM:(h + 1) * A_HEAD_DIM, pl.ds(base, width)]
            s_t = s_ref[u, 0:width, :] + bias_ref[0:width, :]
            _softmax_step_t(s_t, v_t, m_ref, acc_ref, h)

    def issue_next_tile(s_ref):
        for j in range(group // 2):
            halves = head_halves(qn_ref[:, j * LANES:(j + 1) * LANES])
            for u in range(2):
                s_ref[2 * j + u] = _dot_nt(k_ref[0:kc, j * LANES:(j + 1) * LANES], halves[u])

    half_last = (i + 1) * tq - (n_chunks - 1) * kc <= kc // 2
    _attention_pipeline(n_chunks - 1, half_last, i == 0, kc, DSA_UNROLL, issue_qk, consume,
                        issue_next_tile, (s0_ref, s1_ref))
    _softmax_finish(acc_ref, o_ref)


def _dsa_attention(q, qi, wi, ki, k, vt, batch, seq):
    n = q.shape[0]
    tq = DSA_Q_TILE
    nq = seq // tq
    top_k = min(TOPK_MAX, seq // 4)
    qrow = lambda width: pl.BlockSpec((tq, width), lambda b, i: (b * nq + i, 0))
    q_next = pl.BlockSpec((tq, A_WIDTH), lambda b, i: (b * nq + jnp.minimum(i + 1, nq - 1), 0))
    per_batch = lambda width: pl.BlockSpec((seq, width), lambda b, i: (b, 0))
    return pl.pallas_call(
        functools.partial(_dsa_kernel, top_k),
        grid=(batch, nq),
        in_specs=[qrow(A_WIDTH), q_next, qrow(IDX_HEADS * IDX_DIM), qrow(LANES),
                  per_batch(LANES), per_batch(A_WIDTH),
                  pl.BlockSpec((A_WIDTH, seq), lambda b, i: (0, b))],
        out_specs=pl.BlockSpec((A_WIDTH, tq), lambda b, i: (0, b * nq + i)),
        out_shape=jax.ShapeDtypeStruct((A_WIDTH, n), jnp.bfloat16),
        scratch_shapes=[
            pltpu.VMEM((seq, tq), jnp.float32),
            pltpu.VMEM((seq, tq), jnp.bfloat16),
            pltpu.VMEM((IDX_HEADS // 2, LANES, 2 * tq), jnp.bfloat16),
            pltpu.VMEM((A_HEADS, LANES, tq), jnp.bfloat16),
            pltpu.VMEM((A_HEADS, 1, tq), jnp.float32),
            pltpu.VMEM((A_HEADS, A_HEAD_DIM + SUM_ROWS, tq), jnp.float32),
            pltpu.VMEM((A_HEADS // 2, DSA_KEY_CHUNK, tq), jnp.float32),
            pltpu.VMEM((A_HEADS // 2, DSA_KEY_CHUNK, tq), jnp.float32),
            pltpu.VMEM((DSA_KEY_CHUNK, tq), jnp.float32),
        ],
        compiler_params=_params("arbitrary", "arbitrary"),
        name="dsa_attention",
    )(q, q, qi, wi, ki, k, vt)


def _dot_tn(a_t, b):
    return lax.dot_general(a_t, b, (((0,), (0,)), ((), ())), preferred_element_type=jnp.float32)


def _mix_mlp_kernel(n_in, n_ff, *refs):
    a_refs, wo_refs = refs[:n_in], refs[n_in:2 * n_in]
    (x_ref, gmix_ref, gpre_ref, gpost_ref, w1_ref, w2_ref, o_ref,
     x1_ref, hn_ref, acc_ref) = refs[2 * n_in:]
    f = pl.program_id(1)
    tm = x_ref.shape[0]
    halves = [slice(r * (tm // 2), (r + 1) * (tm // 2)) for r in range(2)]

    def mix(rows):
        y = _dot_tn(a_refs[0][:, rows], wo_refs[0][...])
        for a_ref, wo_ref in zip(a_refs[1:], wo_refs[1:]):
            y = y + _dot(a_ref[rows, :], wo_ref[...])
        x1 = x_ref[rows, :] + _rms(y, gmix_ref[...])
        x1_ref[rows, :] = x1
        hn_ref[rows, :] = _rms(x1, gpre_ref[...]).astype(jnp.bfloat16)

    def mlp(rows, first):
        h = jnp.maximum(_dot(hn_ref[rows, :], w1_ref[...]), 0.0)
        part = _dot((h * h).astype(jnp.bfloat16), w2_ref[...])
        acc_ref[rows, :] = part if first else acc_ref[rows, :] + part

    @pl.when(f == 0)
    def _():
        for rows in halves:
            mix(rows)
            mlp(rows, True)

    @pl.when((f > 0) & (f < n_ff - 1))
    def _():
        for rows in halves:
            mlp(rows, False)

    @pl.when(f == n_ff - 1)
    def _():
        for rows in halves:
            mlp(rows, False)
        for rows in halves:
            o_ref[rows, :] = x1_ref[rows, :] + _rms(acc_ref[rows, :], gpost_ref[...])


def _mix_mlp(acts, w_outs, x, g_mix, gpre, gpost, w1, w2):
    n, d = x.shape
    ff = w1.shape[1]
    tm, tf = MLP_ROW_TILE, MLP_FF_TILE
    row = lambda a: pl.BlockSpec((tm, a.shape[1]), lambda i, f: (i, 0))
    col = lambda a: pl.BlockSpec((a.shape[0], tm), lambda i, f: (0, i))
    full = lambda a: pl.BlockSpec(a.shape, lambda i, f: (0,) * a.ndim)
    return pl.pallas_call(
        functools.partial(_mix_mlp_kernel, len(acts), ff // tf),
        grid=(n // tm, ff // tf),
        in_specs=[col(acts[0])] + [row(a) for a in acts[1:]] + [full(w) for w in w_outs]
        + [row(x), full(g_mix), full(gpre), full(gpost),
           pl.BlockSpec((d, tf), lambda i, f: (0, f)),
           pl.BlockSpec((tf, d), lambda i, f: (f, 0))],
        out_specs=row(x),
        out_shape=jax.ShapeDtypeStruct((n, d), jnp.float32),
        scratch_shapes=[pltpu.VMEM((tm, d), jnp.float32), pltpu.VMEM((tm, d), jnp.bfloat16),
                        pltpu.VMEM((tm, d), jnp.float32)],
        compiler_params=_params("arbitrary", "arbitrary"),
        name="mix_mlp",
    )(*acts, *w_outs, x, g_mix, gpre, gpost, w1, w2)


_DKR, _DKV, _DEND = C_Q_RANK, C_Q_RANK + LANES, C_Q_RANK + LANES + C_KV_RANK


def _mla_in_kernel(x_ref, g_ref, wd_ref, qn_ref, kvn_ref, wuq_ref, wuk_ref, wuvt_ref,
                   c_ref, s_ref, q_ref, k_ref, vt_ref):
    hn = _rms(x_ref[...], g_ref[...]).astype(jnp.bfloat16)
    c, s = c_ref[...].T, s_ref[...].T
    scale = (C_NOPE_DIM + C_ROPE_DIM) ** -0.5 * LOG2E
    cq, sq = c * scale, s * scale

    def rope(y, ct, st):
        return y * ct + pltpu.roll(y, LANES - C_ROPE_DIM, axis=1) * st

    kvd = _dot(hn, wd_ref[:, _DKV:_DEND])
    qr = _dot(hn, wd_ref[:, 0:_DKV])
    ckv = _rms(kvd, kvn_ref[...]).astype(jnp.bfloat16)
    vt_ref[...] = _dot_nt(wuvt_ref[...], ckv).astype(vt_ref.dtype)
    kr = rope(qr[:, _DKR:], c, s)
    n_wide = C_HEADS * LANES // MXU_COLS
    for j in range(n_wide):
        k2 = _dot(ckv, wuk_ref[:, j * MXU_COLS:(j + 1) * MXU_COLS])
        for u in range(MXU_COLS // LANES):
            cols = slice(j * MXU_COLS + u * LANES, j * MXU_COLS + (u + 1) * LANES)
            k_ref[:, cols] = (k2[:, u * LANES:(u + 1) * LANES] + kr).astype(k_ref.dtype)

    qa = _rms(qr[:, :_DKR], qn_ref[...]).astype(jnp.bfloat16)
    up = lambda j: _dot(qa, wuq_ref[:, j * MXU_COLS:(j + 1) * MXU_COLS])
    nxt = up(0)
    for j in range(n_wide):
        q2 = nxt
        if j + 1 < n_wide:
            nxt = up(j + 1)
        for u in range(MXU_COLS // LANES):
            cols = slice(j * MXU_COLS + u * LANES, j * MXU_COLS + (u + 1) * LANES)
            q_ref[:, cols] = rope(q2[:, u * LANES:(u + 1) * LANES], cq, sq).astype(q_ref.dtype)


def _mla_in_proj(x, gain, w_down, q_norm, kv_norm, w_uq, w_uk, w_uvt, tabs):
    n, d = x.shape
    tm = MLA_ROW_TILE
    row = lambda width: pl.BlockSpec((tm, width), lambda i: (i, 0))
    full = lambda a: pl.BlockSpec(a.shape, lambda i: (0,) * a.ndim)
    tab = pl.BlockSpec((LANES, tm), lambda i: (0, i))
    bf = jnp.bfloat16
    return pl.pallas_call(
        _mla_in_kernel,
        grid=(n // tm,),
        in_specs=[row(d), full(gain), full(w_down), full(q_norm), full(kv_norm),
                  full(w_uq), full(w_uk), full(w_uvt), tab, tab],
        out_specs=(row(C_HEADS * LANES), row(C_HEADS * LANES),
                   pl.BlockSpec((C_HEADS * C_V_DIM, tm), lambda i: (0, i))),
        out_shape=(jax.ShapeDtypeStruct((n, C_HEADS * LANES), bf),
                   jax.ShapeDtypeStruct((n, C_HEADS * LANES), bf),
                   jax.ShapeDtypeStruct((C_HEADS * C_V_DIM, n), bf)),
        compiler_params=_params("arbitrary"),
        name="mla_in_proj",
    )(x, gain, w_down, q_norm, kv_norm, w_uq, w_uk, w_uvt, *tabs)


def _mla_attn_kernel(q_ref, qn_ref, k_ref, vt_ref, o_ref,
                     qt_ref, m_ref, acc_ref, s0_ref, s1_ref):
    i = pl.program_id(2)
    tq, kc, group = MLA_Q_TILE, MLA_KEY_CHUNK, MLA_HEAD_GROUP // 2
    n_full = (i * tq) // kc
    _softmax_init(m_ref, acc_ref)
    for h in range(MLA_HEAD_GROUP):
        qt_ref[h] = q_ref[:, h * LANES:(h + 1) * LANES].astype(jnp.float32).T.astype(
            jnp.bfloat16)

    def issue_qk(c, g, s_ref, width):
        base = pl.multiple_of(c * kc, kc)
        for u in range(group):
            h = g * group + u
            s_ref[u, 0:width, :] = _dot(
                k_ref[pl.ds(base, width), h * LANES:(h + 1) * LANES], qt_ref[h])

    def consume(c, g, s_ref, last, width):
        base = pl.multiple_of(c * kc, kc)
        if last:
            q_pos = i * tq + lax.broadcasted_iota(jnp.int32, (width, tq), 1)
            k_pos = base + lax.broadcasted_iota(jnp.int32, (width, tq), 0)
            bias_t = jnp.where(k_pos <= q_pos, 0.0, NEG_BIG)
        for u in range(group):
            h = g * group + u
            s_t = s_ref[u, 0:width, :]
            if last:
                s_t = s_t + bias_t
            v_t = vt_ref[h * C_V_DIM:(h + 1) * C_V_DIM, pl.ds(base, width)]
            _softmax_step_t(s_t, v_t, m_ref, acc_ref, h)

    def issue_next_tile(s_ref):
        for u in range(group):
            cols = slice(u * LANES, (u + 1) * LANES)
            s_ref[u] = _dot_nt(k_ref[0:kc, cols], qn_ref[:, cols])

    half_last = (i + 1) * tq - n_full * kc <= kc // 2
    _attention_pipeline(n_full, half_last, i == 0, kc, MLA_UNROLL, issue_qk, consume,
                        issue_next_tile, (s0_ref, s1_ref))
    _softmax_finish(acc_ref, o_ref)


def _mla_attention(q, k, vt, batch, seq):
    n = q.shape[0]
    tq, hg = MLA_Q_TILE, MLA_HEAD_GROUP
    nq = seq // tq
    return pl.pallas_call(
        _mla_attn_kernel,
        grid=(batch, C_HEADS // hg, nq),
        in_specs=[pl.BlockSpec((tq, hg * LANES), lambda b, g, i: (b * nq + i, g)),
                  pl.BlockSpec((tq, hg * LANES),
                               lambda b, g, i: (b * nq + jnp.minimum(i + 1, nq - 1), g)),
                  pl.BlockSpec((seq, hg * LANES), lambda b, g, i: (b, g)),
                  pl.BlockSpec((hg * C_V_DIM, seq), lambda b, g, i: (g, b))],
        out_specs=pl.BlockSpec((hg * C_V_DIM, tq), lambda b, g, i: (g, b * nq + i)),
        out_shape=jax.ShapeDtypeStruct((C_HEADS * C_V_DIM, n), jnp.bfloat16),
        scratch_shapes=[pltpu.VMEM((hg, LANES, tq), jnp.bfloat16),
                        pltpu.VMEM((hg, 1, tq), jnp.float32),
                        pltpu.VMEM((hg, C_V_DIM + SUM_ROWS, tq), jnp.float32),
                        pltpu.VMEM((hg // 2, MLA_KEY_CHUNK, tq), jnp.float32),
                        pltpu.VMEM((hg // 2, MLA_KEY_CHUNK, tq), jnp.float32)],
        compiler_params=_params("arbitrary", "arbitrary", "arbitrary"),
        name="mla_attention",
    )(q, q, k, vt)


def _rope_tables(positions, rot_dim, lead, period):
    inv_freq = ROPE_THETA ** (-jnp.arange(0, rot_dim, 2, dtype=jnp.float32) / rot_dim)
    ang = inv_freq[:, None] * positions.astype(jnp.float32).reshape(1, -1)
    cos, sin = jnp.cos(ang), jnp.sin(ang)
    half, n = cos.shape
    ones = lambda w: jnp.ones((w, n), jnp.float32)
    zeros = lambda w: jnp.zeros((w, n), jnp.float32)
    tail = period - lead - rot_dim
    c = jnp.concatenate([ones(lead), cos, cos, ones(tail)], axis=0)
    s_next = jnp.concatenate([zeros(lead), -sin, zeros(half + tail)], axis=0)
    s_prev = jnp.concatenate([zeros(lead + half), sin, zeros(tail)], axis=0)
    reps = LANES // period
    return tuple(jnp.tile(t, (reps, 1)) for t in (c, s_next, s_prev))


def _pack_even_w_in(w):
    d = w.shape[0]
    q, k, v, qi, ki, wi, gb, gc, xi = jnp.split(
        w, [512, 1024, 1536, 2560, 2624, 2640, 3152, 3664], axis=1)
    pad = jnp.zeros((d, LANES - IDX_HEADS), w.dtype)
    packed = jnp.concatenate([q, k, qi, ki, ki, wi, pad, gb, gc, xi], axis=1)
    return packed.astype(jnp.bfloat16), v.T.astype(jnp.bfloat16)


def _pad_heads(w, heads, width):
    r = w.shape[0]
    w = w.reshape(r, heads, width)
    return jnp.pad(w, ((0, 0), (0, 0), (0, LANES - width))).reshape(r, heads * LANES)


def _rope_swapped(w_rope):
    half = C_ROPE_DIM // 2
    return jnp.concatenate([w_rope, w_rope[..., half:], w_rope[..., :half]], axis=-1)


def _mla_rope_tables(positions):
    inv_freq = ROPE_THETA ** (-jnp.arange(0, C_ROPE_DIM, 2, dtype=jnp.float32) / C_ROPE_DIM)
    ang = inv_freq[:, None] * positions.astype(jnp.float32).reshape(1, -1)
    cos, sin = jnp.cos(ang), jnp.sin(ang)
    n = cos.shape[1]
    tail = jnp.zeros((LANES - C_NOPE_DIM - C_ROPE_DIM, n), jnp.float32)
    c = jnp.concatenate([jnp.ones((C_NOPE_DIM, n), jnp.float32), cos, cos, tail], axis=0)
    s = jnp.concatenate([jnp.zeros((C_NOPE_DIM, n), jnp.float32), -sin, sin, tail], axis=0)
    return c, s


def kernel(x, positions, norm_mix_pre, norm_mix_post, norm_ffn_pre, norm_ffn_post, even_w_in, even_conv_w, even_w_out, odd_w_dq, odd_q_norm, odd_w_uq, odd_w_dkv, odd_kv_norm, odd_w_ukv, odd_w_o, mlp_w1, mlp_w2):
    batch, seq, d = x.shape
    depth = norm_mix_pre.shape[0]
    bf = jnp.bfloat16
    tabs_a = _rope_tables(positions, A_ROT_DIM, 0, A_HEAD_DIM)
    tabs_c = _mla_rope_tables(positions)
    h = x.reshape(batch * seq, d)
    row = lambda a: a.reshape(1, -1)
    for layer in range(depth):
        j = layer // 2
        g_pre, g_post = row(norm_mix_pre[layer]), row(norm_mix_post[layer])
        if layer % 2 == 0:
            w_packed, w_vt = _pack_even_w_in(even_w_in[j])
            q, k, vt, qi, ki, wi, conv = _even_in_proj(
                h, g_pre, w_packed, w_vt, tabs_a, even_conv_w[j], seq)
            attn = _dsa_attention(q, qi, wi, ki, k, vt, batch, seq)
            w_out = even_w_out[j].astype(bf)
            acts, w_outs = [attn, conv], [w_out[:A_WIDTH], w_out[A_WIDTH:]]
        else:
            w_dkv = odd_w_dkv[j]
            w_down = jnp.concatenate(
                [odd_w_dq[j], jnp.zeros((d, C_NOPE_DIM), w_dkv.dtype),
                 _rope_swapped(w_dkv[:, C_KV_RANK:]), w_dkv[:, :C_KV_RANK]], axis=1)
            w_ukv = odd_w_ukv[j].reshape(C_KV_RANK, C_HEADS, C_NOPE_DIM + C_V_DIM)
            w_uk = _pad_heads(w_ukv[:, :, :C_NOPE_DIM].reshape(C_KV_RANK, -1), C_HEADS, C_NOPE_DIM)
            w_uv = w_ukv[:, :, C_NOPE_DIM:].reshape(C_KV_RANK, -1)
            w_uq = odd_w_uq[j].reshape(C_Q_RANK, C_HEADS, C_NOPE_DIM + C_ROPE_DIM)
            w_uq = jnp.concatenate(
                [w_uq[:, :, :C_NOPE_DIM], _rope_swapped(w_uq[:, :, C_NOPE_DIM:])],
                axis=-1).reshape(C_Q_RANK, C_HEADS * LANES)
            q, k, vt = _mla_in_proj(h, g_pre, w_down.astype(bf), row(odd_q_norm[j]),
                                    row(odd_kv_norm[j]), w_uq.astype(bf), w_uk.astype(bf),
                                    w_uv.T.astype(bf), tabs_c)
            acts, w_outs = [_mla_attention(q, k, vt, batch, seq)], [odd_w_o[j].astype(bf)]
        h = _mix_mlp(acts, w_outs, h, g_post, row(norm_ffn_pre[layer]),
                     row(norm_ffn_post[layer]), mlp_w1[layer].astype(bf),
                     mlp_w2[layer].astype(bf))
    return h.reshape(batch, seq, d)
```

```python
import functools

import jax
import jax.numpy as jnp
from jax import lax
from jax.experimental import pallas as pl
from jax.experimental.pallas import tpu as pltpu

ROPE_THETA = 500000.0
NORM_EPS = 1e-6
LANES = 128
MXU_COLS = 256
NEG_BIG = -1e30
LOG2E = 1.4426950408889634
SUM_ROWS = 16

A_HEADS = 8
A_HEAD_DIM = 64
A_WIDTH = A_HEADS * A_HEAD_DIM
A_ROT_DIM = A_HEAD_DIM // 4
IDX_HEADS = 16
IDX_DIM = 64
TOPK_MAX = 256
B_WIDTH = 512
CONV_WIDTH = 3
C_HEADS = 16
C_NOPE_DIM = 64
C_ROPE_DIM = 32
C_V_DIM = 64
C_Q_RANK = 384
C_KV_RANK = 256

KEY_LOWEST_FINITE = -2139095040
COARSE_SLACK = 0x8000
KEY_BITS = 32
COARSE_BITS = 16

ROW_TILE = 512
MLA_ROW_TILE = 512
MLP_ROW_TILE = 512
MLP_FF_TILE = 2048
DSA_Q_TILE = 256
DSA_KEY_CHUNK = 512
DSA_COUNT_CHUNK = 256
DSA_COUNT_ROWS = 64
DSA_UNROLL = 4
DSA_SCORE_UNROLL = 4
MLA_UNROLL = 4
MLA_Q_TILE = 256
MLA_KEY_CHUNK = 512
MLA_HEAD_GROUP = 8
VMEM_LIMIT = 56 * 1024 * 1024


def _dot(a, b):
    return jnp.dot(a, b, preferred_element_type=jnp.float32)


def _dot_nt(a, b):
    return lax.dot_general(a, b, (((1,), (1,)), ((), ())), preferred_element_type=jnp.float32)


def _rms(x, gain):
    ms = jnp.mean(x * x, axis=-1, keepdims=True)
    return x * lax.rsqrt(ms + NORM_EPS) * gain


def _rope_tile(x, c, s_next, s_prev, half):
    return (x * c + pltpu.roll(x, LANES - half, axis=1) * s_next
            + pltpu.roll(x, half, axis=1) * s_prev)


def _params(*semantics):
    return pltpu.CompilerParams(dimension_semantics=semantics, vmem_limit_bytes=VMEM_LIMIT)


_EQ, _EK, _EQI, _EKI, _EWI, _EGB, _EGC, _EXI, _EEND = (
    0, 512, 1024, 2048, 2176, 2304, 2816, 3328, 3840)


def _even_in_kernel(seq_tiles, x_ref, g_ref, w_ref, wvt_ref, c_ref, sn_ref, sp_ref, cw_ref,
                    q_ref, k_ref, vt_ref, qi_ref, ki_ref, wi_ref, conv_ref, carry_ref):
    i = pl.program_id(0)
    tm = x_ref.shape[0]
    hn = _rms(x_ref[...], g_ref[...]).astype(jnp.bfloat16)
    c, sn, sp = c_ref[...].T, sn_ref[...].T, sp_ref[...].T
    half = A_ROT_DIM // 2

    def rope_section(start, width, out_ref, scale=None):
        for t in range(width // MXU_COLS):
            y2 = _dot(hn, w_ref[:, start + t * MXU_COLS:start + (t + 1) * MXU_COLS])
            for u in range(MXU_COLS // LANES):
                y = _rope_tile(y2[:, u * LANES:(u + 1) * LANES], c, sn, sp, half)
                if scale is not None:
                    y = y * scale
                lo = t * MXU_COLS + u * LANES
                out_ref[:, lo:lo + LANES] = y.astype(out_ref.dtype)

    gate_b = _dot(hn, w_ref[:, _EGB:_EGC])
    u = _dot(hn, w_ref[:, _EGC:_EXI]) * _dot(hn, w_ref[:, _EXI:_EEND])
    seq_start = (i % seq_tiles) == 0
    prev = jnp.where(seq_start, 0.0, carry_ref[...])
    carry_ref[...] = u[tm - 8:, :]
    row = lax.broadcasted_iota(jnp.int32, u.shape, 0)
    u1 = jnp.where(row == 0, prev[7:8, :], pltpu.roll(u, 1, axis=0))
    u2 = pltpu.roll(u, 2, axis=0)
    u2 = jnp.where(row == 0, prev[6:7, :], jnp.where(row == 1, prev[7:8, :], u2))
    y = cw_ref[0:1, :] * u2 + cw_ref[1:2, :] * u1 + cw_ref[2:3, :] * u
    conv_ref[...] = (gate_b * y).astype(conv_ref.dtype)

    rope_section(_EQ, A_WIDTH, q_ref, A_HEAD_DIM ** -0.5 * LOG2E)
    rope_section(_EK, A_WIDTH, k_ref)
    rope_section(_EQI, IDX_HEADS * IDX_DIM, qi_ref)
    kw = _dot(hn, w_ref[:, _EKI:_EGB])
    ki_ref[...] = _rope_tile(kw[:, :LANES], c, sn, sp, half).astype(ki_ref.dtype)
    wi_ref[...] = kw[:, LANES:] * (IDX_HEADS ** -0.5 * IDX_DIM ** -0.5)
    vt_ref[...] = _dot_nt(wvt_ref[...], hn).astype(vt_ref.dtype)


def _even_in_proj(x, gain, w_packed, w_vt, tabs, conv_w, seq):
    n, d = x.shape
    tm = ROW_TILE
    row = lambda width: pl.BlockSpec((tm, width), lambda i: (i, 0))
    full = lambda a: pl.BlockSpec(a.shape, lambda i: (0,) * a.ndim)
    tab = pl.BlockSpec((LANES, tm), lambda i: (0, i))
    bf = jnp.bfloat16
    out_shape = (
        jax.ShapeDtypeStruct((n, A_WIDTH), bf), jax.ShapeDtypeStruct((n, A_WIDTH), bf),
        jax.ShapeDtypeStruct((A_WIDTH, n), bf), jax.ShapeDtypeStruct((n, IDX_HEADS * IDX_DIM), bf),
        jax.ShapeDtypeStruct((n, LANES), bf), jax.ShapeDtypeStruct((n, LANES), jnp.float32),
        jax.ShapeDtypeStruct((n, B_WIDTH), bf))
    return pl.pallas_call(
        functools.partial(_even_in_kernel, seq // tm),
        grid=(n // tm,),
        in_specs=[row(d), full(gain), full(w_packed), full(w_vt), tab, tab, tab,
                  full(conv_w)],
        out_specs=(row(A_WIDTH), row(A_WIDTH), pl.BlockSpec((A_WIDTH, tm), lambda i: (0, i)),
                   row(IDX_HEADS * IDX_DIM), row(LANES), row(LANES), row(B_WIDTH)),
        out_shape=out_shape,
        scratch_shapes=[pltpu.VMEM((8, B_WIDTH), jnp.float32)],
        compiler_params=_params("arbitrary"),
        name="even_in_proj",
    )(x, gain, w_packed, w_vt, *tabs, conv_w)


def _key_to_float(key):
    bits = key ^ ((key >> 31) & jnp.int32(0x7FFFFFFF))
    return lax.bitcast_convert_type(bits, jnp.float32)


def _softmax_step_t(s_t, v_t, m_ref, acc_ref, h):
    kc = s_t.shape[0]
    m_prev = m_ref[h]
    m_new = jnp.maximum(m_prev, jnp.max(s_t, axis=0, keepdims=True))
    alpha = jnp.exp2(m_prev - m_new)
    p_t = jnp.exp2(s_t - m_new).astype(jnp.bfloat16)
    v_aug = jnp.concatenate([v_t, jnp.ones((SUM_ROWS, kc), jnp.bfloat16)], axis=0)
    acc_ref[h] = alpha * acc_ref[h] + _dot(v_aug, p_t)
    m_ref[h] = m_new


def _softmax_init(m_ref, acc_ref):
    m_ref[...] = jnp.full(m_ref.shape, NEG_BIG, jnp.float32)
    acc_ref[...] = jnp.zeros(acc_ref.shape, jnp.float32)


def _softmax_finish(acc_ref, o_ref):
    heads, rows, _ = acc_ref.shape
    dv = rows - SUM_ROWS
    o_t = jnp.concatenate(
        [acc_ref[h, 0:dv, :] * (1.0 / acc_ref[h, dv:dv + 1, :]) for h in range(heads)], axis=0)
    o_ref[...] = o_t.astype(o_ref.dtype)


def _unrolled_loop(n, unroll, step):
    def steps(first, count):
        for t in range(count):
            step(first + t)

    def unrolled(p, carry):
        steps(unroll * p, unroll)
        return carry

    lax.fori_loop(0, n // unroll, unrolled, 0)
    done = (n // unroll) * unroll
    size = unroll // 2
    while size >= 1:
        has = (n & size) != 0
        pl.when(has)(functools.partial(steps, done, size))
        done = done + jnp.where(has, size, 0)
        size //= 2


def _attention_pipeline(n_last, half_last, first_tile, kc, unroll, issue_qk, consume,
                        issue_next_tile, s_refs):
    s0, s1 = s_refs
    pl.when(first_tile)(lambda: issue_qk(0, 0, s0, kc))

    def step(c):
        issue_qk(c, 1, s1, kc)
        consume(c, 0, s0, False, kc)
        issue_qk(c + 1, 0, s0, kc)
        consume(c, 1, s1, False, kc)

    _unrolled_loop(n_last, unroll, step)

    def last_chunk(width):
        issue_qk(n_last, 1, s1, width)
        consume(n_last, 0, s0, True, width)
        issue_next_tile(s0)
        consume(n_last, 1, s1, True, width)

    pl.when(half_last)(lambda: last_chunk(kc // 2))
    pl.when(jnp.logical_not(half_last))(lambda: last_chunk(kc))


def _dsa_kernel(top_k, q_ref, qn_ref, qi_ref, wi_ref, ki_ref, k_ref, vt_ref, o_ref,
                sc_ref, hb_ref, qm_ref, qp_ref, m_ref, acc_ref, s0_ref, s1_ref, bias_ref):
    i = pl.program_id(1)
    tq, kc, ks = DSA_Q_TILE, DSA_KEY_CHUNK, DSA_COUNT_CHUNK
    n_chunks = (i * tq) // kc + 1
    n_count = (i * tq) // ks + 1
    low = lax.broadcasted_iota(jnp.int32, (tq, LANES), 1) < IDX_DIM

    def head_halves(tile):
        t = tile.astype(jnp.float32)
        return (jnp.where(low, t, 0.0).astype(jnp.bfloat16),
                jnp.where(low, 0.0, t).astype(jnp.bfloat16))

    top = lax.broadcasted_iota(jnp.int32, (LANES, tq), 0) < IDX_DIM

    def head_halves_t(tile):
        t = tile.astype(jnp.float32).T
        return (jnp.where(top, t, 0.0).astype(jnp.bfloat16),
                jnp.where(top, 0.0, t).astype(jnp.bfloat16))

    for j in range(IDX_HEADS // 2):
        qm_ref[j, :, 0:tq], qm_ref[j, :, tq:2 * tq] = head_halves_t(
            qi_ref[:, j * LANES:(j + 1) * LANES])
    for j in range(A_HEADS // 2):
        qp_ref[2 * j], qp_ref[2 * j + 1] = head_halves_t(q_ref[:, j * LANES:(j + 1) * LANES])
    w_t = wi_ref[...].T

    q_pos = i * tq + lax.broadcasted_iota(jnp.int32, (kc, tq), 1)
    k_off = lax.broadcasted_iota(jnp.int32, (kc, tq), 0)

    def score_chunk(c):
        base = pl.multiple_of(c * kc, kc)
        kch = ki_ref[pl.ds(base, kc), :]
        acc = jnp.zeros((kc, tq), jnp.float32)
        for j in range(IDX_HEADS // 2):
            r = _dot(kch, qm_ref[j])
            acc = acc + jnp.maximum(r[:, :tq], 0.0) * w_t[2 * j:2 * j + 1, :]
            acc = acc + jnp.maximum(r[:, tq:], 0.0) * w_t[2 * j + 1:2 * j + 2, :]
        masked = jnp.where(base + k_off <= q_pos, acc, -jnp.inf)
        sc_ref[pl.ds(base, kc), :] = masked
        hb_ref[pl.ds(base, kc), :] = masked.astype(jnp.bfloat16)

    _unrolled_loop(n_chunks, DSA_SCORE_UNROLL, score_chunk)

    part = DSA_COUNT_ROWS

    def count(src_ref, pred):
        one = jnp.ones((), src_ref.dtype)

        def count_chunk(c, cnt):
            base = pl.multiple_of(c * ks, ks)
            for r in range(ks // part):
                row0 = base + r * part
                cnt = jnp.where(pred(src_ref[pl.ds(row0, part), :], row0), cnt + one, cnt)
            return cnt

        cnt = lax.fori_loop(0, n_count, count_chunk, jnp.zeros((part, tq), src_ref.dtype))
        return jnp.sum(cnt.astype(jnp.float32), axis=0, keepdims=True)

    def coarse(step, lo):
        cand = lo + jnp.left_shift(jnp.int32(1), KEY_BITS - 1 - step)
        cand_b = jnp.broadcast_to(_key_to_float(cand).astype(jnp.bfloat16), (part, tq))
        total = count(hb_ref, lambda blk, row0: blk >= cand_b)
        return jnp.where(total >= top_k, cand, lo)

    k_coarse = lax.fori_loop(
        0, COARSE_BITS, coarse, jnp.full((1, tq), jnp.iinfo(jnp.int32).min, jnp.int32))

    lo0 = jnp.maximum(k_coarse, KEY_LOWEST_FINITE + COARSE_SLACK) - COARSE_SLACK
    hi0 = jnp.maximum(k_coarse + 2 * COARSE_SLACK, lo0)

    def fine(step, carry):
        lo, hi, n_lo = carry
        mid = lo + jnp.right_shift(hi - lo + 1, 1)
        mid_f = _key_to_float(mid)
        total = count(sc_ref, lambda blk, row0: blk >= mid_f)
        take = total >= top_k
        return jnp.where(take, mid, lo), jnp.where(take, hi, mid - 1), jnp.where(take, total, n_lo)

    unknown = jnp.full((1, tq), -1.0, jnp.float32)
    lo, _, n_ge = lax.fori_loop(0, (3 * COARSE_SLACK).bit_length(), fine, (lo0, hi0, unknown))
    thr = _key_to_float(lo)
    n_ge = lax.cond(jnp.min(n_ge) < 0.0,
                    lambda: count(sc_ref, lambda blk, row0: blk >= thr), lambda: n_ge)

    @pl.when(jnp.max(n_ge) > top_k)
    def _():
        n_gt = count(sc_ref, lambda blk, row0: blk > thr)
        need = top_k - n_gt
        row = lax.broadcasted_iota(jnp.int32, (part, tq), 0)
        index_bits = (sc_ref.shape[0] - 1).bit_length()

        def index_search(step, cut):
            cand = cut + jnp.left_shift(jnp.int32(1), index_bits - 1 - step)
            below = count(sc_ref, lambda blk, row0: (blk == thr) & (row0 + row < cand))
            return jnp.where(below < need, cand, cut)

        cut = lax.fori_loop(0, index_bits, index_search, jnp.zeros((1, tq), jnp.int32))

        def drop_chunk(c, carry):
            base = pl.multiple_of(c * ks, ks)
            blk = sc_ref[pl.ds(base, ks), :]
            pos = base + lax.broadcasted_iota(jnp.int32, (ks, tq), 0)
            sc_ref[pl.ds(base, ks), :] = jnp.where((blk == thr) & (pos > cut), -jnp.inf, blk)
            return carry

        lax.fori_loop(0, n_count, drop_chunk, 0)

    _softmax_init(m_ref, acc_ref)

    group = A_HEADS // 2

    def issue_qk(c, g, s_ref, width):
        base = pl.multiple_of(c * kc, kc)
        for u in range(group):
            h = g * group + u
            cols = slice((h // 2) * LANES, (h // 2 + 1) * LANES)
            s_ref[u, 0:width, :] = _dot(k_ref[pl.ds(base, width), cols], qp_ref[h])

    def consume(c, g, s_ref, last, width):
        base = pl.multiple_of(c * kc, kc)
        if g == 0:
            bias_ref[0:width, :] = jnp.where(
                sc_ref[pl.ds(base, width), :] >= thr, 0.0, NEG_BIG)
        for u in range(group):
            h = g * group + u
            v_t = vt_ref[h * A_HEAD_DIM:(h + 1) * A_HEAD_DIM, pl.ds(base, width)]
            s_t = s_ref[u, 0:width, :] + bias_ref[0:width, :]
            _softmax_step_t(s_t, v_t, m_ref, acc_ref, h)

    def issue_next_tile(s_ref):
        for j in range(group // 2):
            halves = head_halves(qn_ref[:, j * LANES:(j + 1) * LANES])
            for u in range(2):
                s_ref[2 * j + u] = _dot_nt(k_ref[0:kc, j * LANES:(j + 1) * LANES], halves[u])

    half_last = (i + 1) * tq - (n_chunks - 1) * kc <= kc // 2
    _attention_pipeline(n_chunks - 1, half_last, i == 0, kc, DSA_UNROLL, issue_qk, consume,
                        issue_next_tile, (s0_ref, s1_ref))
    _softmax_finish(acc_ref, o_ref)


def _dsa_attention(q, qi, wi, ki, k, vt, batch, seq):
    n = q.shape[0]
    tq = DSA_Q_TILE
    nq = seq // tq
    top_k = min(TOPK_MAX, seq // 4)
    assert A_HEAD_DIM == IDX_DIM == LANES // 2
    qrow = lambda width: pl.BlockSpec((tq, width), lambda b, i: (b * nq + i, 0))
    q_next = pl.BlockSpec((tq, A_WIDTH), lambda b, i: (b * nq + jnp.minimum(i + 1, nq - 1), 0))
    per_batch = lambda width: pl.BlockSpec((seq, width), lambda b, i: (b, 0))
    return pl.pallas_call(
        functools.partial(_dsa_kernel, top_k),
        grid=(batch, nq),
        in_specs=[qrow(A_WIDTH), q_next, qrow(IDX_HEADS * IDX_DIM), qrow(LANES),
                  per_batch(LANES), per_batch(A_WIDTH),
                  pl.BlockSpec((A_WIDTH, seq), lambda b, i: (0, b))],
        out_specs=pl.BlockSpec((A_WIDTH, tq), lambda b, i: (0, b * nq + i)),
        out_shape=jax.ShapeDtypeStruct((A_WIDTH, n), jnp.bfloat16),
        scratch_shapes=[
            pltpu.VMEM((seq, tq), jnp.float32),
            pltpu.VMEM((seq, tq), jnp.bfloat16),
            pltpu.VMEM((IDX_HEADS // 2, LANES, 2 * tq), jnp.bfloat16),
            pltpu.VMEM((A_HEADS, LANES, tq), jnp.bfloat16),
            pltpu.VMEM((A_HEADS, 1, tq), jnp.float32),
            pltpu.VMEM((A_HEADS, A_HEAD_DIM + SUM_ROWS, tq), jnp.float32),
            pltpu.VMEM((A_HEADS // 2, DSA_KEY_CHUNK, tq), jnp.float32),
            pltpu.VMEM((A_HEADS // 2, DSA_KEY_CHUNK, tq), jnp.float32),
            pltpu.VMEM((DSA_KEY_CHUNK, tq), jnp.float32),
        ],
        compiler_params=_params("arbitrary", "arbitrary"),
        name="dsa_attention",
    )(q, q, qi, wi, ki, k, vt)


def _dot_tn(a_t, b):
    return lax.dot_general(a_t, b, (((0,), (0,)), ((), ())), preferred_element_type=jnp.float32)


def _mix_mlp_kernel(n_in, *refs):
    a_refs, wo_refs = refs[:n_in], refs[n_in:2 * n_in]
    (x_ref, gmix_ref, gpre_ref, gpost_ref, w1_ref, w2_ref, o_ref,
     x1_ref, hn_ref, acc_ref) = refs[2 * n_in:]
    f = pl.program_id(1)

    @pl.when(f == 0)
    def _():
        y = _dot_tn(a_refs[0][...], wo_refs[0][...])
        for a_ref, wo_ref in zip(a_refs[1:], wo_refs[1:]):
            y = y + _dot(a_ref[...], wo_ref[...])
        x1 = x_ref[...] + _rms(y, gmix_ref[...])
        x1_ref[...] = x1
        hn_ref[...] = _rms(x1, gpre_ref[...]).astype(jnp.bfloat16)
        acc_ref[...] = jnp.zeros(acc_ref.shape, jnp.float32)

    h = jnp.maximum(_dot(hn_ref[...], w1_ref[...]), 0.0)
    acc_ref[...] += _dot((h * h).astype(jnp.bfloat16), w2_ref[...])

    @pl.when(f == pl.num_programs(1) - 1)
    def _():
        o_ref[...] = x1_ref[...] + _rms(acc_ref[...], gpost_ref[...])


def _mix_mlp(acts, w_outs, x, g_mix, gpre, gpost, w1, w2):
    n, d = x.shape
    ff = w1.shape[1]
    tm, tf = MLP_ROW_TILE, MLP_FF_TILE
    row = lambda a: pl.BlockSpec((tm, a.shape[1]), lambda i, f: (i, 0))
    col = lambda a: pl.BlockSpec((a.shape[0], tm), lambda i, f: (0, i))
    full = lambda a: pl.BlockSpec(a.shape, lambda i, f: (0,) * a.ndim)
    return pl.pallas_call(
        functools.partial(_mix_mlp_kernel, len(acts)),
        grid=(n // tm, ff // tf),
        in_specs=[col(acts[0])] + [row(a) for a in acts[1:]] + [full(w) for w in w_outs]
        + [row(x), full(g_mix), full(gpre), full(gpost),
           pl.BlockSpec((d, tf), lambda i, f: (0, f)),
           pl.BlockSpec((tf, d), lambda i, f: (f, 0))],
        out_specs=row(x),
        out_shape=jax.ShapeDtypeStruct((n, d), jnp.float32),
        scratch_shapes=[pltpu.VMEM((tm, d), jnp.float32), pltpu.VMEM((tm, d), jnp.bfloat16),
                        pltpu.VMEM((tm, d), jnp.float32)],
        compiler_params=_params("arbitrary", "arbitrary"),
        name="mix_mlp",
    )(*acts, *w_outs, x, g_mix, gpre, gpost, w1, w2)


_DKR, _DKV, _DEND = C_Q_RANK, C_Q_RANK + LANES, C_Q_RANK + LANES + C_KV_RANK


def _mla_in_kernel(x_ref, g_ref, wd_ref, qn_ref, kvn_ref, wuq_ref, wuk_ref, wuvt_ref,
                   c_ref, s_ref, q_ref, k_ref, vt_ref):
    hn = _rms(x_ref[...], g_ref[...]).astype(jnp.bfloat16)
    c, s = c_ref[...].T, s_ref[...].T
    scale = (C_NOPE_DIM + C_ROPE_DIM) ** -0.5 * LOG2E
    cq, sq = c * scale, s * scale

    def rope(y, ct, st):
        return y * ct + pltpu.roll(y, LANES - C_ROPE_DIM, axis=1) * st

    kvd = _dot(hn, wd_ref[:, _DKV:_DEND])
    qr = _dot(hn, wd_ref[:, 0:_DKV])
    ckv = _rms(kvd, kvn_ref[...]).astype(jnp.bfloat16)
    vt_ref[...] = _dot_nt(wuvt_ref[...], ckv).astype(vt_ref.dtype)
    kr = rope(qr[:, _DKR:], c, s)
    n_wide = C_HEADS * LANES // MXU_COLS
    for j in range(n_wide):
        k2 = _dot(ckv, wuk_ref[:, j * MXU_COLS:(j + 1) * MXU_COLS])
        for u in range(MXU_COLS // LANES):
            cols = slice(j * MXU_COLS + u * LANES, j * MXU_COLS + (u + 1) * LANES)
            k_ref[:, cols] = (k2[:, u * LANES:(u + 1) * LANES] + kr).astype(k_ref.dtype)

    qa = _rms(qr[:, :_DKR], qn_ref[...]).astype(jnp.bfloat16)
    up = lambda j: _dot(qa, wuq_ref[:, j * MXU_COLS:(j + 1) * MXU_COLS])
    nxt = up(0)
    for j in range(n_wide):
        q2 = nxt
        if j + 1 < n_wide:
            nxt = up(j + 1)
        for u in range(MXU_COLS // LANES):
            cols = slice(j * MXU_COLS + u * LANES, j * MXU_COLS + (u + 1) * LANES)
            q_ref[:, cols] = rope(q2[:, u * LANES:(u + 1) * LANES], cq, sq).astype(q_ref.dtype)


def _mla_in_proj(x, gain, w_down, q_norm, kv_norm, w_uq, w_uk, w_uvt, tabs):
    n, d = x.shape
    tm = MLA_ROW_TILE
    row = lambda width: pl.BlockSpec((tm, width), lambda i: (i, 0))
    full = lambda a: pl.BlockSpec(a.shape, lambda i: (0,) * a.ndim)
    tab = pl.BlockSpec((LANES, tm), lambda i: (0, i))
    bf = jnp.bfloat16
    return pl.pallas_call(
        _mla_in_kernel,
        grid=(n // tm,),
        in_specs=[row(d), full(gain), full(w_down), full(q_norm), full(kv_norm),
                  full(w_uq), full(w_uk), full(w_uvt), tab, tab],
        out_specs=(row(C_HEADS * LANES), row(C_HEADS * LANES),
                   pl.BlockSpec((C_HEADS * C_V_DIM, tm), lambda i: (0, i))),
        out_shape=(jax.ShapeDtypeStruct((n, C_HEADS * LANES), bf),
                   jax.ShapeDtypeStruct((n, C_HEADS * LANES), bf),
                   jax.ShapeDtypeStruct((C_HEADS * C_V_DIM, n), bf)),
        compiler_params=_params("arbitrary"),
        name="mla_in_proj",
    )(x, gain, w_down, q_norm, kv_norm, w_uq, w_uk, w_uvt, *tabs)


def _mla_attn_kernel(q_ref, qn_ref, k_ref, vt_ref, o_ref,
                     qt_ref, m_ref, acc_ref, s0_ref, s1_ref):
    i = pl.program_id(2)
    tq, kc, group = MLA_Q_TILE, MLA_KEY_CHUNK, MLA_HEAD_GROUP // 2
    n_full = (i * tq) // kc
    _softmax_init(m_ref, acc_ref)
    for h in range(MLA_HEAD_GROUP):
        qt_ref[h] = q_ref[:, h * LANES:(h + 1) * LANES].astype(jnp.float32).T.astype(
            jnp.bfloat16)

    def issue_qk(c, g, s_ref, width):
        base = pl.multiple_of(c * kc, kc)
        for u in range(group):
            h = g * group + u
            s_ref[u, 0:width, :] = _dot(
                k_ref[pl.ds(base, width), h * LANES:(h + 1) * LANES], qt_ref[h])

    def consume(c, g, s_ref, last, width):
        base = pl.multiple_of(c * kc, kc)
        if last:
            q_pos = i * tq + lax.broadcasted_iota(jnp.int32, (width, tq), 1)
            k_pos = base + lax.broadcasted_iota(jnp.int32, (width, tq), 0)
            bias_t = jnp.where(k_pos <= q_pos, 0.0, NEG_BIG)
        for u in range(group):
            h = g * group + u
            s_t = s_ref[u, 0:width, :]
            if last:
                s_t = s_t + bias_t
            v_t = vt_ref[h * C_V_DIM:(h + 1) * C_V_DIM, pl.ds(base, width)]
            _softmax_step_t(s_t, v_t, m_ref, acc_ref, h)

    def issue_next_tile(s_ref):
        for u in range(group):
            cols = slice(u * LANES, (u + 1) * LANES)
            s_ref[u] = _dot_nt(k_ref[0:kc, cols], qn_ref[:, cols])

    half_last = (i + 1) * tq - n_full * kc <= kc // 2
    _attention_pipeline(n_full, half_last, i == 0, kc, MLA_UNROLL, issue_qk, consume,
                        issue_next_tile, (s0_ref, s1_ref))
    _softmax_finish(acc_ref, o_ref)


def _mla_attention(q, k, vt, batch, seq):
    n = q.shape[0]
    tq, hg = MLA_Q_TILE, MLA_HEAD_GROUP
    nq = seq // tq
    return pl.pallas_call(
        _mla_attn_kernel,
        grid=(batch, C_HEADS // hg, nq),
        in_specs=[pl.BlockSpec((tq, hg * LANES), lambda b, g, i: (b * nq + i, g)),
                  pl.BlockSpec((tq, hg * LANES),
                               lambda b, g, i: (b * nq + jnp.minimum(i + 1, nq - 1), g)),
                  pl.BlockSpec((seq, hg * LANES), lambda b, g, i: (b, g)),
                  pl.BlockSpec((hg * C_V_DIM, seq), lambda b, g, i: (g, b))],
        out_specs=pl.BlockSpec((hg * C_V_DIM, tq), lambda b, g, i: (g, b * nq + i)),
        out_shape=jax.ShapeDtypeStruct((C_HEADS * C_V_DIM, n), jnp.bfloat16),
        scratch_shapes=[pltpu.VMEM((hg, LANES, tq), jnp.bfloat16),
                        pltpu.VMEM((hg, 1, tq), jnp.float32),
                        pltpu.VMEM((hg, C_V_DIM + SUM_ROWS, tq), jnp.float32),
                        pltpu.VMEM((hg // 2, MLA_KEY_CHUNK, tq), jnp.float32),
                        pltpu.VMEM((hg // 2, MLA_KEY_CHUNK, tq), jnp.float32)],
        compiler_params=_params("arbitrary", "arbitrary", "arbitrary"),
        name="mla_attention",
    )(q, q, k, vt)


def _rope_tables(positions, rot_dim, lead, period):
    inv_freq = ROPE_THETA ** (-jnp.arange(0, rot_dim, 2, dtype=jnp.float32) / rot_dim)
    ang = inv_freq[:, None] * positions.astype(jnp.float32).reshape(1, -1)
    cos, sin = jnp.cos(ang), jnp.sin(ang)
    half, n = cos.shape
    ones = lambda w: jnp.ones((w, n), jnp.float32)
    zeros = lambda w: jnp.zeros((w, n), jnp.float32)
    tail = period - lead - rot_dim
    c = jnp.concatenate([ones(lead), cos, cos, ones(tail)], axis=0)
    s_next = jnp.concatenate([zeros(lead), -sin, zeros(half + tail)], axis=0)
    s_prev = jnp.concatenate([zeros(lead + half), sin, zeros(tail)], axis=0)
    reps = LANES // period
    return tuple(jnp.tile(t, (reps, 1)) for t in (c, s_next, s_prev))


def _pack_even_w_in(w):
    d = w.shape[0]
    q, k, v, qi, ki, wi, gb, gc, xi = jnp.split(
        w, [512, 1024, 1536, 2560, 2624, 2640, 3152, 3664], axis=1)
    pad = jnp.zeros((d, LANES - IDX_HEADS), w.dtype)
    packed = jnp.concatenate([q, k, qi, ki, ki, wi, pad, gb, gc, xi], axis=1)
    return packed.astype(jnp.bfloat16), v.T.astype(jnp.bfloat16)


def _pad_heads(w, heads, width):
    r = w.shape[0]
    w = w.reshape(r, heads, width)
    return jnp.pad(w, ((0, 0), (0, 0), (0, LANES - width))).reshape(r, heads * LANES)


def _rope_swapped(w_rope):
    half = C_ROPE_DIM // 2
    return jnp.concatenate([w_rope, w_rope[..., half:], w_rope[..., :half]], axis=-1)


def _mla_rope_tables(positions):
    inv_freq = ROPE_THETA ** (-jnp.arange(0, C_ROPE_DIM, 2, dtype=jnp.float32) / C_ROPE_DIM)
    ang = inv_freq[:, None] * positions.astype(jnp.float32).reshape(1, -1)
    cos, sin = jnp.cos(ang), jnp.sin(ang)
    n = cos.shape[1]
    tail = jnp.zeros((LANES - C_NOPE_DIM - C_ROPE_DIM, n), jnp.float32)
    c = jnp.concatenate([jnp.ones((C_NOPE_DIM, n), jnp.float32), cos, cos, tail], axis=0)
    s = jnp.concatenate([jnp.zeros((C_NOPE_DIM, n), jnp.float32), -sin, sin, tail], axis=0)
    return c, s


def kernel(x, positions, norm_mix_pre, norm_mix_post, norm_ffn_pre, norm_ffn_post, even_w_in, even_conv_w, even_w_out, odd_w_dq, odd_q_norm, odd_w_uq, odd_w_dkv, odd_kv_norm, odd_w_ukv, odd_w_o, mlp_w1, mlp_w2):
    batch, seq, d = x.shape
    depth = norm_mix_pre.shape[0]
    bf = jnp.bfloat16
    tabs_a = _rope_tables(positions, A_ROT_DIM, 0, A_HEAD_DIM)
    tabs_c = _mla_rope_tables(positions)
    h = x.reshape(batch * seq, d)
    row = lambda a: a.reshape(1, -1)
    for layer in range(depth):
        j = layer // 2
        g_pre, g_post = row(norm_mix_pre[layer]), row(norm_mix_post[layer])
        if layer % 2 == 0:
            w_packed, w_vt = _pack_even_w_in(even_w_in[j])
            q, k, vt, qi, ki, wi, conv = _even_in_proj(
                h, g_pre, w_packed, w_vt, tabs_a, even_conv_w[j], seq)
            attn = _dsa_attention(q, qi, wi, ki, k, vt, batch, seq)
            w_out = even_w_out[j].astype(bf)
            acts, w_outs = [attn, conv], [w_out[:A_WIDTH], w_out[A_WIDTH:]]
        else:
            w_dkv = odd_w_dkv[j]
            w_down = jnp.concatenate(
                [odd_w_dq[j], jnp.zeros((d, C_NOPE_DIM), w_dkv.dtype),
                 _rope_swapped(w_dkv[:, C_KV_RANK:]), w_dkv[:, :C_KV_RANK]], axis=1)
            w_ukv = odd_w_ukv[j].reshape(C_KV_RANK, C_HEADS, C_NOPE_DIM + C_V_DIM)
            w_uk = _pad_heads(w_ukv[:, :, :C_NOPE_DIM].reshape(C_KV_RANK, -1), C_HEADS, C_NOPE_DIM)
            w_uv = w_ukv[:, :, C_NOPE_DIM:].reshape(C_KV_RANK, -1)
            w_uq = odd_w_uq[j].reshape(C_Q_RANK, C_HEADS, C_NOPE_DIM + C_ROPE_DIM)
            w_uq = jnp.concatenate(
                [w_uq[:, :, :C_NOPE_DIM], _rope_swapped(w_uq[:, :, C_NOPE_DIM:])],
                axis=-1).reshape(C_Q_RANK, C_HEADS * LANES)
            q, k, vt = _mla_in_proj(h, g_pre, w_down.astype(bf), row(odd_q_norm[j]),
                                    row(odd_kv_norm[j]), w_uq.astype(bf), w_uk.astype(bf),
                                    w_uv.T.astype(bf), tabs_c)
            acts, w_outs = [_mla_attention(q, k, vt, batch, seq)], [odd_w_o[j].astype(bf)]
        h = _mix_mlp(acts, w_outs, h, g_post, row(norm_ffn_pre[layer]),
                     row(norm_ffn_post[layer]), mlp_w1[layer].astype(bf),
                     mlp_w2[layer].astype(bf))
    return h.reshape(batch, seq, d)
```

```python
import functools

import jax
import jax.numpy as jnp
from jax import lax
from jax.experimental import pallas as pl
from jax.experimental.pallas import tpu as pltpu

ROPE_THETA = 500000.0
NORM_EPS = 1e-6
LANES = 128
MXU_COLS = 256
NEG_BIG = -1e30
LOG2E = 1.4426950408889634
SUM_ROWS = 16

A_HEADS = 8
A_HEAD_DIM = 64
A_WIDTH = A_HEADS * A_HEAD_DIM
A_ROT_DIM = A_HEAD_DIM // 4
IDX_HEADS = 16
IDX_DIM = 64
TOPK_MAX = 256
B_WIDTH = 512
CONV_WIDTH = 3
C_HEADS = 16
C_NOPE_DIM = 64
C_ROPE_DIM = 32
C_V_DIM = 64
C_Q_RANK = 384
C_KV_RANK = 256

KEY_LOWEST_FINITE = -2139095040
COARSE_SLACK = 0x8000
KEY_BITS = 32
COARSE_BITS = 16

ROW_TILE = 512
MLA_ROW_TILE = 512
MLP_ROW_TILE = 512
MLP_FF_TILE = 2048
DSA_Q_TILE = 256
DSA_KEY_CHUNK = 512
DSA_COUNT_CHUNK = 256
DSA_COUNT_ROWS = 64
DSA_UNROLL = 4
DSA_SCORE_UNROLL = 4
MLA_UNROLL = 4
MLA_Q_TILE = 256
MLA_KEY_CHUNK = 512
MLA_HEAD_GROUP = 8
VMEM_LIMIT = 56 * 1024 * 1024


def _dot(a, b):
    return jnp.dot(a, b, preferred_element_type=jnp.float32)


def _dot_nt(a, b):
    return lax.dot_general(a, b, (((1,), (1,)), ((), ())), preferred_element_type=jnp.float32)


def _rms(x, gain):
    ms = jnp.mean(x * x, axis=-1, keepdims=True)
    return x * lax.rsqrt(ms + NORM_EPS) * gain


def _rope_tile(x, c, s_next, s_prev, half):
    return (x * c + pltpu.roll(x, LANES - half, axis=1) * s_next
            + pltpu.roll(x, half, axis=1) * s_prev)


def _params(*semantics):
    return pltpu.CompilerParams(dimension_semantics=semantics, vmem_limit_bytes=VMEM_LIMIT)


_EQ, _EK, _EQI, _EKI, _EWI, _EGB, _EGC, _EXI, _EEND = (
    0, 512, 1024, 2048, 2176, 2304, 2816, 3328, 3840)


def _even_in_kernel(seq_tiles, x_ref, g_ref, w_ref, wvt_ref, c_ref, sn_ref, sp_ref, cw_ref,
                    q_ref, k_ref, vt_ref, qi_ref, ki_ref, wi_ref, conv_ref, carry_ref):
    i = pl.program_id(0)
    tm = x_ref.shape[0]
    hn = _rms(x_ref[...], g_ref[...]).astype(jnp.bfloat16)
    c, sn, sp = c_ref[...].T, sn_ref[...].T, sp_ref[...].T
    half = A_ROT_DIM // 2

    def rope_section(start, width, out_ref, scale=None):
        for t in range(width // MXU_COLS):
            y2 = _dot(hn, w_ref[:, start + t * MXU_COLS:start + (t + 1) * MXU_COLS])
            for u in range(MXU_COLS // LANES):
                y = _rope_tile(y2[:, u * LANES:(u + 1) * LANES], c, sn, sp, half)
                if scale is not None:
                    y = y * scale
                lo = t * MXU_COLS + u * LANES
                out_ref[:, lo:lo + LANES] = y.astype(out_ref.dtype)

    gate_b = _dot(hn, w_ref[:, _EGB:_EGC])
    u = _dot(hn, w_ref[:, _EGC:_EXI]) * _dot(hn, w_ref[:, _EXI:_EEND])
    seq_start = (i % seq_tiles) == 0
    prev = jnp.where(seq_start, 0.0, carry_ref[...])
    carry_ref[...] = u[tm - 8:, :]
    row = lax.broadcasted_iota(jnp.int32, u.shape, 0)
    u1 = jnp.where(row == 0, prev[7:8, :], pltpu.roll(u, 1, axis=0))
    u2 = pltpu.roll(u, 2, axis=0)
    u2 = jnp.where(row == 0, prev[6:7, :], jnp.where(row == 1, prev[7:8, :], u2))
    y = cw_ref[0:1, :] * u2 + cw_ref[1:2, :] * u1 + cw_ref[2:3, :] * u
    conv_ref[...] = (gate_b * y).astype(conv_ref.dtype)

    rope_section(_EQ, A_WIDTH, q_ref, A_HEAD_DIM ** -0.5 * LOG2E)
    rope_section(_EK, A_WIDTH, k_ref)
    rope_section(_EQI, IDX_HEADS * IDX_DIM, qi_ref)
    kw = _dot(hn, w_ref[:, _EKI:_EGB])
    ki_ref[...] = _rope_tile(kw[:, :LANES], c, sn, sp, half).astype(ki_ref.dtype)
    wi_ref[...] = kw[:, LANES:] * (IDX_HEADS ** -0.5 * IDX_DIM ** -0.5)
    vt_ref[...] = _dot_nt(wvt_ref[...], hn).astype(vt_ref.dtype)


def _even_in_proj(x, gain, w_packed, w_vt, tabs, conv_w, seq):
    n, d = x.shape
    tm = ROW_TILE
    row = lambda width: pl.BlockSpec((tm, width), lambda i: (i, 0))
    full = lambda a: pl.BlockSpec(a.shape, lambda i: (0,) * a.ndim)
    tab = pl.BlockSpec((LANES, tm), lambda i: (0, i))
    bf = jnp.bfloat16
    out_shape = (
        jax.ShapeDtypeStruct((n, A_WIDTH), bf), jax.ShapeDtypeStruct((n, A_WIDTH), bf),
        jax.ShapeDtypeStruct((A_WIDTH, n), bf), jax.ShapeDtypeStruct((n, IDX_HEADS * IDX_DIM), bf),
        jax.ShapeDtypeStruct((n, LANES), bf), jax.ShapeDtypeStruct((n, LANES), jnp.float32),
        jax.ShapeDtypeStruct((n, B_WIDTH), bf))
    return pl.pallas_call(
        functools.partial(_even_in_kernel, seq // tm),
        grid=(n // tm,),
        in_specs=[row(d), full(gain), full(w_packed), full(w_vt), tab, tab, tab,
                  full(conv_w)],
        out_specs=(row(A_WIDTH), row(A_WIDTH), pl.BlockSpec((A_WIDTH, tm), lambda i: (0, i)),
                   row(IDX_HEADS * IDX_DIM), row(LANES), row(LANES), row(B_WIDTH)),
        out_shape=out_shape,
        scratch_shapes=[pltpu.VMEM((8, B_WIDTH), jnp.float32)],
        compiler_params=_params("arbitrary"),
        name="even_in_proj",
    )(x, gain, w_packed, w_vt, *tabs, conv_w)


def _key_to_float(key):
    bits = key ^ ((key >> 31) & jnp.int32(0x7FFFFFFF))
    return lax.bitcast_convert_type(bits, jnp.float32)


def _softmax_step_t(s_t, v_t, m_ref, acc_ref, h):
    kc = s_t.shape[0]
    m_prev = m_ref[h]
    m_new = jnp.maximum(m_prev, jnp.max(s_t, axis=0, keepdims=True))
    alpha = jnp.exp2(m_prev - m_new)
    p_t = jnp.exp2(s_t - m_new).astype(jnp.bfloat16)
    v_aug = jnp.concatenate([v_t, jnp.ones((SUM_ROWS, kc), jnp.bfloat16)], axis=0)
    acc_ref[h] = alpha * acc_ref[h] + _dot(v_aug, p_t)
    m_ref[h] = m_new


def _softmax_init(m_ref, acc_ref):
    m_ref[...] = jnp.full(m_ref.shape, NEG_BIG, jnp.float32)
    acc_ref[...] = jnp.zeros(acc_ref.shape, jnp.float32)


def _softmax_finish(acc_ref, o_ref):
    heads, rows, _ = acc_ref.shape
    dv = rows - SUM_ROWS
    o_t = jnp.concatenate(
        [acc_ref[h, 0:dv, :] * (1.0 / acc_ref[h, dv:dv + 1, :]) for h in range(heads)], axis=0)
    o_ref[...] = o_t.astype(o_ref.dtype)


def _unrolled_loop(n, unroll, step):
    def steps(first, count):
        for t in range(count):
            step(first + t)

    def unrolled(p, carry):
        steps(unroll * p, unroll)
        return carry

    lax.fori_loop(0, n // unroll, unrolled, 0)
    done = (n // unroll) * unroll
    size = unroll // 2
    while size >= 1:
        has = (n & size) != 0
        pl.when(has)(functools.partial(steps, done, size))
        done = done + jnp.where(has, size, 0)
        size //= 2


def _attention_pipeline(n_last, half_last, first_tile, kc, unroll, issue_qk, consume,
                        issue_next_tile, s_refs):
    s0, s1 = s_refs
    pl.when(first_tile)(lambda: issue_qk(0, 0, s0, kc))

    def step(c):
        issue_qk(c, 1, s1, kc)
        consume(c, 0, s0, False, kc)
        issue_qk(c + 1, 0, s0, kc)
        consume(c, 1, s1, False, kc)

    _unrolled_loop(n_last, unroll, step)

    def last_chunk(width):
        issue_qk(n_last, 1, s1, width)
        consume(n_last, 0, s0, True, width)
        issue_next_tile(s0)
        consume(n_last, 1, s1, True, width)

    pl.when(half_last)(lambda: last_chunk(kc // 2))
    pl.when(jnp.logical_not(half_last))(lambda: last_chunk(kc))


def _dsa_kernel(top_k, q_ref, qn_ref, qi_ref, wi_ref, ki_ref, k_ref, vt_ref, o_ref,
                sc_ref, hb_ref, qm_ref, qp_ref, m_ref, acc_ref, s0_ref, s1_ref, bias_ref):
    i = pl.program_id(1)
    tq, kc, ks = DSA_Q_TILE, DSA_KEY_CHUNK, DSA_COUNT_CHUNK
    n_chunks = (i * tq) // kc + 1
    n_count = (i * tq) // ks + 1
    low = lax.broadcasted_iota(jnp.int32, (tq, LANES), 1) < IDX_DIM

    def head_halves(tile):
        t = tile.astype(jnp.float32)
        return (jnp.where(low, t, 0.0).astype(jnp.bfloat16),
                jnp.where(low, 0.0, t).astype(jnp.bfloat16))

    top = lax.broadcasted_iota(jnp.int32, (LANES, tq), 0) < IDX_DIM

    def head_halves_t(tile):
        t = tile.astype(jnp.float32).T
        return (jnp.where(top, t, 0.0).astype(jnp.bfloat16),
                jnp.where(top, 0.0, t).astype(jnp.bfloat16))

    for j in range(IDX_HEADS // 2):
        qm_ref[j, :, 0:tq], qm_ref[j, :, tq:2 * tq] = head_halves_t(
            qi_ref[:, j * LANES:(j + 1) * LANES])
    for j in range(A_HEADS // 2):
        qp_ref[2 * j], qp_ref[2 * j + 1] = head_halves_t(q_ref[:, j * LANES:(j + 1) * LANES])
    w_t = wi_ref[...].T

    half_last = (i + 1) * tq - (n_chunks - 1) * kc <= kc // 2

    def score_chunk(c, width=kc):
        base = pl.multiple_of(c * kc, kc)
        kch = ki_ref[pl.ds(base, width), :]
        acc = jnp.zeros((width, tq), jnp.float32)
        for j in range(IDX_HEADS // 2):
            r = _dot(kch, qm_ref[j])
            acc = acc + jnp.maximum(r[:, :tq], 0.0) * w_t[2 * j:2 * j + 1, :]
            acc = acc + jnp.maximum(r[:, tq:], 0.0) * w_t[2 * j + 1:2 * j + 2, :]
        q_pos = i * tq + lax.broadcasted_iota(jnp.int32, (width, tq), 1)
        k_pos = base + lax.broadcasted_iota(jnp.int32, (width, tq), 0)
        masked = jnp.where(k_pos <= q_pos, acc, -jnp.inf)
        sc_ref[pl.ds(base, width), :] = masked
        hb_ref[pl.ds(base, width), :] = masked.astype(jnp.bfloat16)

    _unrolled_loop(n_chunks - 1, DSA_SCORE_UNROLL, score_chunk)
    pl.when(half_last)(lambda: score_chunk(n_chunks - 1, kc // 2))
    pl.when(jnp.logical_not(half_last))(lambda: score_chunk(n_chunks - 1, kc))

    part = DSA_COUNT_ROWS

    def count(src_ref, pred):
        one = jnp.ones((), src_ref.dtype)

        def count_chunk(c, cnt):
            base = pl.multiple_of(c * ks, ks)
            for r in range(ks // part):
                row0 = base + r * part
                cnt = jnp.where(pred(src_ref[pl.ds(row0, part), :], row0), cnt + one, cnt)
            return cnt

        cnt = lax.fori_loop(0, n_count, count_chunk, jnp.zeros((part, tq), src_ref.dtype))
        return jnp.sum(cnt.astype(jnp.float32), axis=0, keepdims=True)

    def coarse(step, lo):
        cand = lo + jnp.left_shift(jnp.int32(1), KEY_BITS - 1 - step)
        cand_b = jnp.broadcast_to(_key_to_float(cand).astype(jnp.bfloat16), (part, tq))
        total = count(hb_ref, lambda blk, row0: blk >= cand_b)
        return jnp.where(total >= top_k, cand, lo)

    k_coarse = lax.fori_loop(
        0, COARSE_BITS, coarse, jnp.full((1, tq), jnp.iinfo(jnp.int32).min, jnp.int32))

    lo0 = jnp.maximum(k_coarse, KEY_LOWEST_FINITE + COARSE_SLACK) - COARSE_SLACK
    hi0 = jnp.maximum(k_coarse + 2 * COARSE_SLACK, lo0)

    def fine(step, carry):
        lo, hi, n_lo = carry
        mid = lo + jnp.right_shift(hi - lo + 1, 1)
        mid_f = _key_to_float(mid)
        total = count(sc_ref, lambda blk, row0: blk >= mid_f)
        take = total >= top_k
        return jnp.where(take, mid, lo), jnp.where(take, hi, mid - 1), jnp.where(take, total, n_lo)

    unknown = jnp.full((1, tq), -1.0, jnp.float32)
    lo, _, n_ge = lax.fori_loop(0, (3 * COARSE_SLACK).bit_length(), fine, (lo0, hi0, unknown))
    thr = _key_to_float(lo)
    n_ge = lax.cond(jnp.min(n_ge) < 0.0,
                    lambda: count(sc_ref, lambda blk, row0: blk >= thr), lambda: n_ge)

    @pl.when(jnp.max(n_ge) > top_k)
    def _():
        n_gt = count(sc_ref, lambda blk, row0: blk > thr)
        need = top_k - n_gt
        row = lax.broadcasted_iota(jnp.int32, (part, tq), 0)
        index_bits = (sc_ref.shape[0] - 1).bit_length()

        def index_search(step, cut):
            cand = cut + jnp.left_shift(jnp.int32(1), index_bits - 1 - step)
            below = count(sc_ref, lambda blk, row0: (blk == thr) & (row0 + row < cand))
            return jnp.where(below < need, cand, cut)

        cut = lax.fori_loop(0, index_bits, index_search, jnp.zeros((1, tq), jnp.int32))

        def drop_chunk(c, carry):
            base = pl.multiple_of(c * ks, ks)
            blk = sc_ref[pl.ds(base, ks), :]
            pos = base + lax.broadcasted_iota(jnp.int32, (ks, tq), 0)
            sc_ref[pl.ds(base, ks), :] = jnp.where((blk == thr) & (pos > cut), -jnp.inf, blk)
            return carry

        lax.fori_loop(0, n_count, drop_chunk, 0)

    _softmax_init(m_ref, acc_ref)

    group = A_HEADS // 2

    def issue_qk(c, g, s_ref, width):
        base = pl.multiple_of(c * kc, kc)
        for u in range(group):
            h = g * group + u
            cols = slice((h // 2) * LANES, (h // 2 + 1) * LANES)
            s_ref[u, 0:width, :] = _dot(k_ref[pl.ds(base, width), cols], qp_ref[h])

    def consume(c, g, s_ref, last, width):
        base = pl.multiple_of(c * kc, kc)
        if g == 0:
            bias_ref[0:width, :] = jnp.where(
                sc_ref[pl.ds(base, width), :] >= thr, 0.0, NEG_BIG)
        for u in range(group):
            h = g * group + u
            v_t = vt_ref[h * A_HEAD_DIM:(h + 1) * A_HEAD_DIM, pl.ds(base, width)]
            s_t = s_ref[u, 0:width, :] + bias_ref[0:width, :]
            _softmax_step_t(s_t, v_t, m_ref, acc_ref, h)

    def issue_next_tile(s_ref):
        for j in range(group // 2):
            halves = head_halves(qn_ref[:, j * LANES:(j + 1) * LANES])
            for u in range(2):
                s_ref[2 * j + u] = _dot_nt(k_ref[0:kc, j * LANES:(j + 1) * LANES], halves[u])

    _attention_pipeline(n_chunks - 1, half_last, i == 0, kc, DSA_UNROLL, issue_qk, consume,
                        issue_next_tile, (s0_ref, s1_ref))
    _softmax_finish(acc_ref, o_ref)


def _dsa_attention(q, qi, wi, ki, k, vt, batch, seq):
    n = q.shape[0]
    tq = DSA_Q_TILE
    nq = seq // tq
    top_k = min(TOPK_MAX, seq // 4)
    assert A_HEAD_DIM == IDX_DIM == LANES // 2
    qrow = lambda width: pl.BlockSpec((tq, width), lambda b, i: (b * nq + i, 0))
    q_next = pl.BlockSpec((tq, A_WIDTH), lambda b, i: (b * nq + jnp.minimum(i + 1, nq - 1), 0))
    per_batch = lambda width: pl.BlockSpec((seq, width), lambda b, i: (b, 0))
    return pl.pallas_call(
        functools.partial(_dsa_kernel, top_k),
        grid=(batch, nq),
        in_specs=[qrow(A_WIDTH), q_next, qrow(IDX_HEADS * IDX_DIM), qrow(LANES),
                  per_batch(LANES), per_batch(A_WIDTH),
                  pl.BlockSpec((A_WIDTH, seq), lambda b, i: (0, b))],
        out_specs=pl.BlockSpec((A_WIDTH, tq), lambda b, i: (0, b * nq + i)),
        out_shape=jax.ShapeDtypeStruct((A_WIDTH, n), jnp.bfloat16),
        scratch_shapes=[
            pltpu.VMEM((seq, tq), jnp.float32),
            pltpu.VMEM((seq, tq), jnp.bfloat16),
            pltpu.VMEM((IDX_HEADS // 2, LANES, 2 * tq), jnp.bfloat16),
            pltpu.VMEM((A_HEADS, LANES, tq), jnp.bfloat16),
            pltpu.VMEM((A_HEADS, 1, tq), jnp.float32),
            pltpu.VMEM((A_HEADS, A_HEAD_DIM + SUM_ROWS, tq), jnp.float32),
            pltpu.VMEM((A_HEADS // 2, DSA_KEY_CHUNK, tq), jnp.float32),
            pltpu.VMEM((A_HEADS // 2, DSA_KEY_CHUNK, tq), jnp.float32),
            pltpu.VMEM((DSA_KEY_CHUNK, tq), jnp.float32),
        ],
        compiler_params=_params("arbitrary", "arbitrary"),
        name="dsa_attention",
    )(q, q, qi, wi, ki, k, vt)


def _dot_tn(a_t, b):
    return lax.dot_general(a_t, b, (((0,), (0,)), ((), ())), preferred_element_type=jnp.float32)


def _mix_mlp_kernel(n_in, *refs):
    a_refs, wo_refs = refs[:n_in], refs[n_in:2 * n_in]
    (x_ref, gmix_ref, gpre_ref, gpost_ref, w1_ref, w2_ref, o_ref,
     x1_ref, hn_ref, acc_ref) = refs[2 * n_in:]
    f = pl.program_id(1)

    @pl.when(f == 0)
    def _():
        y = _dot_tn(a_refs[0][...], wo_refs[0][...])
        for a_ref, wo_ref in zip(a_refs[1:], wo_refs[1:]):
            y = y + _dot(a_ref[...], wo_ref[...])
        x1 = x_ref[...] + _rms(y, gmix_ref[...])
        x1_ref[...] = x1
        hn_ref[...] = _rms(x1, gpre_ref[...]).astype(jnp.bfloat16)
        acc_ref[...] = jnp.zeros(acc_ref.shape, jnp.float32)

    h = jnp.maximum(_dot(hn_ref[...], w1_ref[...]), 0.0)
    acc_ref[...] += _dot((h * h).astype(jnp.bfloat16), w2_ref[...])

    @pl.when(f == pl.num_programs(1) - 1)
    def _():
        o_ref[...] = x1_ref[...] + _rms(acc_ref[...], gpost_ref[...])


def _mix_mlp(acts, w_outs, x, g_mix, gpre, gpost, w1, w2):
    n, d = x.shape
    ff = w1.shape[1]
    tm, tf = MLP_ROW_TILE, MLP_FF_TILE
    row = lambda a: pl.BlockSpec((tm, a.shape[1]), lambda i, f: (i, 0))
    col = lambda a: pl.BlockSpec((a.shape[0], tm), lambda i, f: (0, i))
    full = lambda a: pl.BlockSpec(a.shape, lambda i, f: (0,) * a.ndim)
    return pl.pallas_call(
        functools.partial(_mix_mlp_kernel, len(acts)),
        grid=(n // tm, ff // tf),
        in_specs=[col(acts[0])] + [row(a) for a in acts[1:]] + [full(w) for w in w_outs]
        + [row(x), full(g_mix), full(gpre), full(gpost),
           pl.BlockSpec((d, tf), lambda i, f: (0, f)),
           pl.BlockSpec((tf, d), lambda i, f: (f, 0))],
        out_specs=row(x),
        out_shape=jax.ShapeDtypeStruct((n, d), jnp.float32),
        scratch_shapes=[pltpu.VMEM((tm, d), jnp.float32), pltpu.VMEM((tm, d), jnp.bfloat16),
                        pltpu.VMEM((tm, d), jnp.float32)],
        compiler_params=_params("arbitrary", "arbitrary"),
        name="mix_mlp",
    )(*acts, *w_outs, x, g_mix, gpre, gpost, w1, w2)


_DKR, _DKV, _DEND = C_Q_RANK, C_Q_RANK + LANES, C_Q_RANK + LANES + C_KV_RANK


def _mla_in_kernel(x_ref, g_ref, wd_ref, qn_ref, kvn_ref, wuq_ref, wuk_ref, wuvt_ref,
                   c_ref, s_ref, q_ref, k_ref, vt_ref):
    hn = _rms(x_ref[...], g_ref[...]).astype(jnp.bfloat16)
    c, s = c_ref[...].T, s_ref[...].T
    scale = (C_NOPE_DIM + C_ROPE_DIM) ** -0.5 * LOG2E
    cq, sq = c * scale, s * scale

    def rope(y, ct, st):
        return y * ct + pltpu.roll(y, LANES - C_ROPE_DIM, axis=1) * st

    kvd = _dot(hn, wd_ref[:, _DKV:_DEND])
    qr = _dot(hn, wd_ref[:, 0:_DKV])
    ckv = _rms(kvd, kvn_ref[...]).astype(jnp.bfloat16)
    vt_ref[...] = _dot_nt(wuvt_ref[...], ckv).astype(vt_ref.dtype)
    kr = rope(qr[:, _DKR:], c, s)
    n_wide = C_HEADS * LANES // MXU_COLS
    for j in range(n_wide):
        k2 = _dot(ckv, wuk_ref[:, j * MXU_COLS:(j + 1) * MXU_COLS])
        for u in range(MXU_COLS // LANES):
            cols = slice(j * MXU_COLS + u * LANES, j * MXU_COLS + (u + 1) * LANES)
            k_ref[:, cols] = (k2[:, u * LANES:(u + 1) * LANES] + kr).astype(k_ref.dtype)

    qa = _rms(qr[:, :_DKR], qn_ref[...]).astype(jnp.bfloat16)
    up = lambda j: _dot(qa, wuq_ref[:, j * MXU_COLS:(j + 1) * MXU_COLS])
    nxt = up(0)
    for j in range(n_wide):
        q2 = nxt
        if j + 1 < n_wide:
            nxt = up(j + 1)
        for u in range(MXU_COLS // LANES):
            cols = slice(j * MXU_COLS + u * LANES, j * MXU_COLS + (u + 1) * LANES)
            q_ref[:, cols] = rope(q2[:, u * LANES:(u + 1) * LANES], cq, sq).astype(q_ref.dtype)


def _mla_in_proj(x, gain, w_down, q_norm, kv_norm, w_uq, w_uk, w_uvt, tabs):
    n, d = x.shape
    tm = MLA_ROW_TILE
    row = lambda width: pl.BlockSpec((tm, width), lambda i: (i, 0))
    full = lambda a: pl.BlockSpec(a.shape, lambda i: (0,) * a.ndim)
    tab = pl.BlockSpec((LANES, tm), lambda i: (0, i))
    bf = jnp.bfloat16
    return pl.pallas_call(
        _mla_in_kernel,
        grid=(n // tm,),
        in_specs=[row(d), full(gain), full(w_down), full(q_norm), full(kv_norm),
                  full(w_uq), full(w_uk), full(w_uvt), tab, tab],
        out_specs=(row(C_HEADS * LANES), row(C_HEADS * LANES),
                   pl.BlockSpec((C_HEADS * C_V_DIM, tm), lambda i: (0, i))),
        out_shape=(jax.ShapeDtypeStruct((n, C_HEADS * LANES), bf),
                   jax.ShapeDtypeStruct((n, C_HEADS * LANES), bf),
                   jax.ShapeDtypeStruct((C_HEADS * C_V_DIM, n), bf)),
        compiler_params=_params("arbitrary"),
        name="mla_in_proj",
    )(x, gain, w_down, q_norm, kv_norm, w_uq, w_uk, w_uvt, *tabs)


def _mla_attn_kernel(q_ref, qn_ref, k_ref, vt_ref, o_ref,
                     qt_ref, m_ref, acc_ref, s0_ref, s1_ref):
    i = pl.program_id(2)
    tq, kc, group = MLA_Q_TILE, MLA_KEY_CHUNK, MLA_HEAD_GROUP // 2
    n_full = (i * tq) // kc
    _softmax_init(m_ref, acc_ref)
    for h in range(MLA_HEAD_GROUP):
        qt_ref[h] = q_ref[:, h * LANES:(h + 1) * LANES].astype(jnp.float32).T.astype(
            jnp.bfloat16)

    def issue_qk(c, g, s_ref, width):
        base = pl.multiple_of(c * kc, kc)
        for u in range(group):
            h = g * group + u
            s_ref[u, 0:width, :] = _dot(
                k_ref[pl.ds(base, width), h * LANES:(h + 1) * LANES], qt_ref[h])

    def consume(c, g, s_ref, last, width):
        base = pl.multiple_of(c * kc, kc)
        if last:
            q_pos = i * tq + lax.broadcasted_iota(jnp.int32, (width, tq), 1)
            k_pos = base + lax.broadcasted_iota(jnp.int32, (width, tq), 0)
            bias_t = jnp.where(k_pos <= q_pos, 0.0, NEG_BIG)
        for u in range(group):
            h = g * group + u
            s_t = s_ref[u, 0:width, :]
            if last:
                s_t = s_t + bias_t
            v_t = vt_ref[h * C_V_DIM:(h + 1) * C_V_DIM, pl.ds(base, width)]
            _softmax_step_t(s_t, v_t, m_ref, acc_ref, h)

    def issue_next_tile(s_ref):
        for u in range(group):
            cols = slice(u * LANES, (u + 1) * LANES)
            s_ref[u] = _dot_nt(k_ref[0:kc, cols], qn_ref[:, cols])

    half_last = (i + 1) * tq - n_full * kc <= kc // 2
    _attention_pipeline(n_full, half_last, i == 0, kc, MLA_UNROLL, issue_qk, consume,
                        issue_next_tile, (s0_ref, s1_ref))
    _softmax_finish(acc_ref, o_ref)


def _mla_attention(q, k, vt, batch, seq):
    n = q.shape[0]
    tq, hg = MLA_Q_TILE, MLA_HEAD_GROUP
    nq = seq // tq
    return pl.pallas_call(
        _mla_attn_kernel,
        grid=(batch, C_HEADS // hg, nq),
        in_specs=[pl.BlockSpec((tq, hg * LANES), lambda b, g, i: (b * nq + i, g)),
                  pl.BlockSpec((tq, hg * LANES),
                               lambda b, g, i: (b * nq + jnp.minimum(i + 1, nq - 1), g)),
                  pl.BlockSpec((seq, hg * LANES), lambda b, g, i: (b, g)),
                  pl.BlockSpec((hg * C_V_DIM, seq), lambda b, g, i: (g, b))],
        out_specs=pl.BlockSpec((hg * C_V_DIM, tq), lambda b, g, i: (g, b * nq + i)),
        out_shape=jax.ShapeDtypeStruct((C_HEADS * C_V_DIM, n), jnp.bfloat16),
        scratch_shapes=[pltpu.VMEM((hg, LANES, tq), jnp.bfloat16),
                        pltpu.VMEM((hg, 1, tq), jnp.float32),
                        pltpu.VMEM((hg, C_V_DIM + SUM_ROWS, tq), jnp.float32),
                        pltpu.VMEM((hg // 2, MLA_KEY_CHUNK, tq), jnp.float32),
                        pltpu.VMEM((hg // 2, MLA_KEY_CHUNK, tq), jnp.float32)],
        compiler_params=_params("arbitrary", "arbitrary", "arbitrary"),
        name="mla_attention",
    )(q, q, k, vt)


def _rope_tables(positions, rot_dim, lead, period):
    inv_freq = ROPE_THETA ** (-jnp.arange(0, rot_dim, 2, dtype=jnp.float32) / rot_dim)
    ang = inv_freq[:, None] * positions.astype(jnp.float32).reshape(1, -1)
    cos, sin = jnp.cos(ang), jnp.sin(ang)
    half, n = cos.shape
    ones = lambda w: jnp.ones((w, n), jnp.float32)
    zeros = lambda w: jnp.zeros((w, n), jnp.float32)
    tail = period - lead - rot_dim
    c = jnp.concatenate([ones(lead), cos, cos, ones(tail)], axis=0)
    s_next = jnp.concatenate([zeros(lead), -sin, zeros(half + tail)], axis=0)
    s_prev = jnp.concatenate([zeros(lead + half), sin, zeros(tail)], axis=0)
    reps = LANES // period
    return tuple(jnp.tile(t, (reps, 1)) for t in (c, s_next, s_prev))


def _pack_even_w_in(w):
    d = w.shape[0]
    w = w.astype(jnp.bfloat16)
    q, k, v, qi, ki, wi, gb, gc, xi = jnp.split(
        w, [512, 1024, 1536, 2560, 2624, 2640, 3152, 3664], axis=1)
    pad = jnp.zeros((d, LANES - IDX_HEADS), w.dtype)
    return jnp.concatenate([q, k, qi, ki, ki, wi, pad, gb, gc, xi], axis=1), v.T


def _pad_heads(w, heads, width):
    r = w.shape[0]
    w = w.reshape(r, heads, width)
    return jnp.pad(w, ((0, 0), (0, 0), (0, LANES - width))).reshape(r, heads * LANES)


def _rope_swapped(w_rope):
    half = C_ROPE_DIM // 2
    return jnp.concatenate([w_rope, w_rope[..., half:], w_rope[..., :half]], axis=-1)


def _mla_rope_tables(positions):
    inv_freq = ROPE_THETA ** (-jnp.arange(0, C_ROPE_DIM, 2, dtype=jnp.float32) / C_ROPE_DIM)
    ang = inv_freq[:, None] * positions.astype(jnp.float32).reshape(1, -1)
    cos, sin = jnp.cos(ang), jnp.sin(ang)
    n = cos.shape[1]
    tail = jnp.zeros((LANES - C_NOPE_DIM - C_ROPE_DIM, n), jnp.float32)
    c = jnp.concatenate([jnp.ones((C_NOPE_DIM, n), jnp.float32), cos, cos, tail], axis=0)
    s = jnp.concatenate([jnp.zeros((C_NOPE_DIM, n), jnp.float32), -sin, sin, tail], axis=0)
    return c, s


def kernel(x, positions, norm_mix_pre, norm_mix_post, norm_ffn_pre, norm_ffn_post, even_w_in, even_conv_w, even_w_out, odd_w_dq, odd_q_norm, odd_w_uq, odd_w_dkv, odd_kv_norm, odd_w_ukv, odd_w_o, mlp_w1, mlp_w2):
    batch, seq, d = x.shape
    depth = norm_mix_pre.shape[0]
    bf = jnp.bfloat16
    tabs_a = _rope_tables(positions, A_ROT_DIM, 0, A_HEAD_DIM)
    tabs_c = _mla_rope_tables(positions)
    h = x.reshape(batch * seq, d)
    row = lambda a: a.reshape(1, -1)
    for layer in range(depth):
        j = layer // 2
        g_pre, g_post = row(norm_mix_pre[layer]), row(norm_mix_post[layer])
        if layer % 2 == 0:
            w_packed, w_vt = _pack_even_w_in(even_w_in[j])
            q, k, vt, qi, ki, wi, conv = _even_in_proj(
                h, g_pre, w_packed, w_vt, tabs_a, even_conv_w[j], seq)
            attn = _dsa_attention(q, qi, wi, ki, k, vt, batch, seq)
            w_out = even_w_out[j].astype(bf)
            acts, w_outs = [attn, conv], [w_out[:A_WIDTH], w_out[A_WIDTH:]]
        else:
            w_dkv = odd_w_dkv[j].astype(bf)
            w_down = jnp.concatenate(
                [odd_w_dq[j].astype(bf), jnp.zeros((d, C_NOPE_DIM), bf),
                 _rope_swapped(w_dkv[:, C_KV_RANK:]), w_dkv[:, :C_KV_RANK]], axis=1)
            w_ukv = odd_w_ukv[j].astype(bf).reshape(C_KV_RANK, C_HEADS, C_NOPE_DIM + C_V_DIM)
            w_uk = _pad_heads(w_ukv[:, :, :C_NOPE_DIM].reshape(C_KV_RANK, -1), C_HEADS, C_NOPE_DIM)
            w_uv = w_ukv[:, :, C_NOPE_DIM:].reshape(C_KV_RANK, -1)
            w_uq = odd_w_uq[j].astype(bf).reshape(C_Q_RANK, C_HEADS, C_NOPE_DIM + C_ROPE_DIM)
            w_uq = jnp.concatenate(
                [w_uq[:, :, :C_NOPE_DIM], _rope_swapped(w_uq[:, :, C_NOPE_DIM:])],
                axis=-1).reshape(C_Q_RANK, C_HEADS * LANES)
            q, k, vt = _mla_in_proj(h, g_pre, w_down, row(odd_q_norm[j]), row(odd_kv_norm[j]),
                                    w_uq, w_uk, w_uv.T, tabs_c)
            acts, w_outs = [_mla_attention(q, k, vt, batch, seq)], [odd_w_o[j].astype(bf)]
        h = _mix_mlp(acts, w_outs, h, g_post, row(norm_ffn_pre[layer]),
                     row(norm_ffn_post[layer]), mlp_w1[layer].astype(bf),
                     mlp_w2[layer].astype(bf))
    return h.reshape(batch, seq, d)
```

```python
import functools

import jax
import jax.numpy as jnp
from jax import lax
from jax.experimental import pallas as pl
from jax.experimental.pallas import tpu as pltpu

ROPE_THETA = 500000.0
NORM_EPS = 1e-6
LANES = 128
MXU_COLS = 256
NEG_BIG = -1e30
LOG2E = 1.4426950408889634
SUM_ROWS = 16

A_HEADS = 8
A_HEAD_DIM = 64
A_WIDTH = A_HEADS * A_HEAD_DIM
A_ROT_DIM = A_HEAD_DIM // 4
IDX_HEADS = 16
IDX_DIM = 64
TOPK_MAX = 256
B_WIDTH = 512
CONV_WIDTH = 3
C_HEADS = 16
C_NOPE_DIM = 64
C_ROPE_DIM = 32
C_V_DIM = 64
C_Q_RANK = 384
C_KV_RANK = 256

KEY_LOWEST_FINITE = -2139095040
COARSE_SLACK = 0x8000
KEY_BITS = 32
COARSE_BITS = 16

ROW_TILE = 512
MLA_ROW_TILE = 512
MLP_ROW_TILE = 512
MLP_FF_TILE = 2048
DSA_Q_TILE = 256
DSA_KEY_CHUNK = 512
DSA_COUNT_CHUNK = 256
DSA_COUNT_ROWS = 64
DSA_UNROLL = 4
DSA_SCORE_UNROLL = 4
MLA_UNROLL = 4
MLA_Q_TILE = 256
MLA_KEY_CHUNK = 512
MLA_HEAD_GROUP = 8
VMEM_LIMIT = 56 * 1024 * 1024


def _dot(a, b):
    return jnp.dot(a, b, preferred_element_type=jnp.float32)


def _dot_nt(a, b):
    return lax.dot_general(a, b, (((1,), (1,)), ((), ())), preferred_element_type=jnp.float32)


def _rms(x, gain):
    ms = jnp.mean(x * x, axis=-1, keepdims=True)
    return x * lax.rsqrt(ms + NORM_EPS) * gain


def _rope_tile(x, c, s_next, s_prev, half):
    return (x * c + pltpu.roll(x, LANES - half, axis=1) * s_next
            + pltpu.roll(x, half, axis=1) * s_prev)


def _params(*semantics):
    return pltpu.CompilerParams(dimension_semantics=semantics, vmem_limit_bytes=VMEM_LIMIT)


_EQ, _EK, _EQI, _EKI, _EWI, _EGB, _EGC, _EXI, _EEND = (
    0, 512, 1024, 2048, 2176, 2304, 2816, 3328, 3840)


def _even_in_kernel(seq_tiles, x_ref, g_ref, w_ref, wvt_ref, c_ref, sn_ref, sp_ref, cw_ref,
                    q_ref, k_ref, vt_ref, qi_ref, ki_ref, wi_ref, conv_ref, carry_ref):
    i = pl.program_id(0)
    tm = x_ref.shape[0]
    hn = _rms(x_ref[...], g_ref[...]).astype(jnp.bfloat16)
    half = A_ROT_DIM // 2

    def lanes_of(rows, fill):
        rest = jnp.full((A_HEAD_DIM - A_ROT_DIM, tm), fill, jnp.float32)
        return jnp.concatenate([rows, rest, rows, rest], axis=0).T

    c, sn, sp = lanes_of(c_ref[...], 1.0), lanes_of(sn_ref[...], 0.0), lanes_of(sp_ref[...], 0.0)

    def rope_section(start, width, out_ref, scale=None):
        for t in range(width // MXU_COLS):
            y2 = _dot(hn, w_ref[:, start + t * MXU_COLS:start + (t + 1) * MXU_COLS])
            for u in range(MXU_COLS // LANES):
                y = _rope_tile(y2[:, u * LANES:(u + 1) * LANES], c, sn, sp, half)
                if scale is not None:
                    y = y * scale
                lo = t * MXU_COLS + u * LANES
                out_ref[:, lo:lo + LANES] = y.astype(out_ref.dtype)

    gate_b = _dot(hn, w_ref[:, _EGB:_EGC])
    u = _dot(hn, w_ref[:, _EGC:_EXI]) * _dot(hn, w_ref[:, _EXI:_EEND])
    seq_start = (i % seq_tiles) == 0
    prev = jnp.where(seq_start, 0.0, carry_ref[...])
    carry_ref[...] = u[tm - 8:, :]
    row = lax.broadcasted_iota(jnp.int32, u.shape, 0)
    u1 = jnp.where(row == 0, prev[7:8, :], pltpu.roll(u, 1, axis=0))
    u2 = pltpu.roll(u, 2, axis=0)
    u2 = jnp.where(row == 0, prev[6:7, :], jnp.where(row == 1, prev[7:8, :], u2))
    y = cw_ref[0:1, :] * u2 + cw_ref[1:2, :] * u1 + cw_ref[2:3, :] * u
    conv_ref[...] = (gate_b * y).astype(conv_ref.dtype)

    rope_section(_EQ, A_WIDTH, q_ref, A_HEAD_DIM ** -0.5 * LOG2E)
    rope_section(_EK, A_WIDTH, k_ref)
    rope_section(_EQI, IDX_HEADS * IDX_DIM, qi_ref)
    kw = _dot(hn, w_ref[:, _EKI:_EGB])
    ki_ref[...] = _rope_tile(kw[:, :LANES], c, sn, sp, half).astype(ki_ref.dtype)
    wi_ref[...] = kw[:, LANES:] * (IDX_HEADS ** -0.5 * IDX_DIM ** -0.5)
    vt_ref[...] = _dot_nt(wvt_ref[...], hn).astype(vt_ref.dtype)


def _even_in_proj(x, gain, w_packed, w_vt, tabs, conv_w, seq):
    n, d = x.shape
    tm = ROW_TILE
    row = lambda width: pl.BlockSpec((tm, width), lambda i: (i, 0))
    full = lambda a: pl.BlockSpec(a.shape, lambda i: (0,) * a.ndim)
    tab = pl.BlockSpec((A_ROT_DIM, tm), lambda i: (0, i))
    bf = jnp.bfloat16
    out_shape = (
        jax.ShapeDtypeStruct((n, A_WIDTH), bf), jax.ShapeDtypeStruct((n, A_WIDTH), bf),
        jax.ShapeDtypeStruct((A_WIDTH, n), bf), jax.ShapeDtypeStruct((n, IDX_HEADS * IDX_DIM), bf),
        jax.ShapeDtypeStruct((n, LANES), bf), jax.ShapeDtypeStruct((n, LANES), jnp.float32),
        jax.ShapeDtypeStruct((n, B_WIDTH), bf))
    return pl.pallas_call(
        functools.partial(_even_in_kernel, seq // tm),
        grid=(n // tm,),
        in_specs=[row(d), full(gain), full(w_packed), full(w_vt), tab, tab, tab,
                  full(conv_w)],
        out_specs=(row(A_WIDTH), row(A_WIDTH), pl.BlockSpec((A_WIDTH, tm), lambda i: (0, i)),
                   row(IDX_HEADS * IDX_DIM), row(LANES), row(LANES), row(B_WIDTH)),
        out_shape=out_shape,
        scratch_shapes=[pltpu.VMEM((8, B_WIDTH), jnp.float32)],
        compiler_params=_params("arbitrary"),
        name="even_in_proj",
    )(x, gain, w_packed, w_vt, *tabs, conv_w)


def _key_to_float(key):
    bits = key ^ ((key >> 31) & jnp.int32(0x7FFFFFFF))
    return lax.bitcast_convert_type(bits, jnp.float32)


def _softmax_step_t(s_t, v_t, m_ref, acc_ref, h):
    kc = s_t.shape[0]
    m_prev = m_ref[h]
    m_new = jnp.maximum(m_prev, jnp.max(s_t, axis=0, keepdims=True))
    alpha = jnp.exp2(m_prev - m_new)
    p_t = jnp.exp2(s_t - m_new).astype(jnp.bfloat16)
    v_aug = jnp.concatenate([v_t, jnp.ones((SUM_ROWS, kc), jnp.bfloat16)], axis=0)
    acc_ref[h] = alpha * acc_ref[h] + _dot(v_aug, p_t)
    m_ref[h] = m_new


def _softmax_init(m_ref, acc_ref):
    m_ref[...] = jnp.full(m_ref.shape, NEG_BIG, jnp.float32)
    acc_ref[...] = jnp.zeros(acc_ref.shape, jnp.float32)


def _softmax_finish(acc_ref, o_ref):
    heads, rows, _ = acc_ref.shape
    dv = rows - SUM_ROWS
    o_t = jnp.concatenate(
        [acc_ref[h, 0:dv, :] * (1.0 / acc_ref[h, dv:dv + 1, :]) for h in range(heads)], axis=0)
    o_ref[...] = o_t.astype(o_ref.dtype)


def _unrolled_loop(n, unroll, step):
    def steps(first, count):
        for t in range(count):
            step(first + t)

    def unrolled(p, carry):
        steps(unroll * p, unroll)
        return carry

    lax.fori_loop(0, n // unroll, unrolled, 0)
    done = (n // unroll) * unroll
    size = unroll // 2
    while size >= 1:
        has = (n & size) != 0
        pl.when(has)(functools.partial(steps, done, size))
        done = done + jnp.where(has, size, 0)
        size //= 2


def _attention_pipeline(n_last, half_last, first_tile, kc, unroll, issue_qk, consume,
                        issue_next_tile, s_refs):
    s0, s1 = s_refs
    pl.when(first_tile)(lambda: issue_qk(0, 0, s0, kc))

    def step(c):
        issue_qk(c, 1, s1, kc)
        consume(c, 0, s0, False, kc)
        issue_qk(c + 1, 0, s0, kc)
        consume(c, 1, s1, False, kc)

    _unrolled_loop(n_last, unroll, step)

    def last_chunk(width):
        issue_qk(n_last, 1, s1, width)
        consume(n_last, 0, s0, True, width)
        issue_next_tile(s0)
        consume(n_last, 1, s1, True, width)

    pl.when(half_last)(lambda: last_chunk(kc // 2))
    pl.when(jnp.logical_not(half_last))(lambda: last_chunk(kc))


def _dsa_kernel(top_k, q_ref, qn_ref, qi_ref, wi_ref, ki_ref, k_ref, vt_ref, o_ref,
                sc_ref, hb_ref, qm_ref, qp_ref, m_ref, acc_ref, s0_ref, s1_ref, bias_ref):
    i = pl.program_id(1)
    tq, kc, ks = DSA_Q_TILE, DSA_KEY_CHUNK, DSA_COUNT_CHUNK
    n_chunks = (i * tq) // kc + 1
    n_count = (i * tq) // ks + 1
    low = lax.broadcasted_iota(jnp.int32, (tq, LANES), 1) < IDX_DIM

    def head_halves(tile):
        t = tile.astype(jnp.float32)
        return (jnp.where(low, t, 0.0).astype(jnp.bfloat16),
                jnp.where(low, 0.0, t).astype(jnp.bfloat16))

    top = lax.broadcasted_iota(jnp.int32, (LANES, tq), 0) < IDX_DIM

    def head_halves_t(tile):
        t = tile.astype(jnp.float32).T
        return (jnp.where(top, t, 0.0).astype(jnp.bfloat16),
                jnp.where(top, 0.0, t).astype(jnp.bfloat16))

    for j in range(IDX_HEADS // 2):
        qm_ref[j, :, 0:tq], qm_ref[j, :, tq:2 * tq] = head_halves_t(
            qi_ref[:, j * LANES:(j + 1) * LANES])
    for j in range(A_HEADS // 2):
        qp_ref[2 * j], qp_ref[2 * j + 1] = head_halves_t(q_ref[:, j * LANES:(j + 1) * LANES])
    w_t = wi_ref[...].T

    half_last = (i + 1) * tq - (n_chunks - 1) * kc <= kc // 2

    def score_chunk(c, width=kc):
        base = pl.multiple_of(c * kc, kc)
        kch = ki_ref[pl.ds(base, width), :]
        acc = jnp.zeros((width, tq), jnp.float32)
        for j in range(IDX_HEADS // 2):
            r = _dot(kch, qm_ref[j])
            acc = acc + jnp.maximum(r[:, :tq], 0.0) * w_t[2 * j:2 * j + 1, :]
            acc = acc + jnp.maximum(r[:, tq:], 0.0) * w_t[2 * j + 1:2 * j + 2, :]
        q_pos = i * tq + lax.broadcasted_iota(jnp.int32, (width, tq), 1)
        k_pos = base + lax.broadcasted_iota(jnp.int32, (width, tq), 0)
        masked = jnp.where(k_pos <= q_pos, acc, -jnp.inf)
        sc_ref[pl.ds(base, width), :] = masked
        hb_ref[pl.ds(base, width), :] = masked.astype(jnp.bfloat16)

    _unrolled_loop(n_chunks - 1, DSA_SCORE_UNROLL, score_chunk)
    pl.when(half_last)(lambda: score_chunk(n_chunks - 1, kc // 2))
    pl.when(jnp.logical_not(half_last))(lambda: score_chunk(n_chunks - 1, kc))

    part = DSA_COUNT_ROWS

    def count(src_ref, pred):
        one = jnp.ones((), src_ref.dtype)

        def count_chunk(c, cnt):
            base = pl.multiple_of(c * ks, ks)
            for r in range(ks // part):
                row0 = base + r * part
                cnt = jnp.where(pred(src_ref[pl.ds(row0, part), :], row0), cnt + one, cnt)
            return cnt

        cnt = lax.fori_loop(0, n_count, count_chunk, jnp.zeros((part, tq), src_ref.dtype))
        return jnp.sum(cnt.astype(jnp.float32), axis=0, keepdims=True)

    def coarse(step, lo):
        cand = lo + jnp.left_shift(jnp.int32(1), KEY_BITS - 1 - step)
        cand_b = jnp.broadcast_to(_key_to_float(cand).astype(jnp.bfloat16), (part, tq))
        total = count(hb_ref, lambda blk, row0: blk >= cand_b)
        return jnp.where(total >= top_k, cand, lo)

    k_coarse = lax.fori_loop(
        0, COARSE_BITS, coarse, jnp.full((1, tq), jnp.iinfo(jnp.int32).min, jnp.int32))

    lo0 = jnp.maximum(k_coarse, KEY_LOWEST_FINITE + COARSE_SLACK) - COARSE_SLACK
    hi0 = jnp.maximum(k_coarse + 2 * COARSE_SLACK, lo0)

    def fine(step, carry):
        lo, hi, n_lo = carry
        mid = lo + jnp.right_shift(hi - lo + 1, 1)
        mid_f = _key_to_float(mid)
        total = count(sc_ref, lambda blk, row0: blk >= mid_f)
        take = total >= top_k
        return jnp.where(take, mid, lo), jnp.where(take, hi, mid - 1), jnp.where(take, total, n_lo)

    unknown = jnp.full((1, tq), -1.0, jnp.float32)
    lo, _, n_ge = lax.fori_loop(0, (3 * COARSE_SLACK).bit_length(), fine, (lo0, hi0, unknown))
    thr = _key_to_float(lo)
    n_ge = lax.cond(jnp.min(n_ge) < 0.0,
                    lambda: count(sc_ref, lambda blk, row0: blk >= thr), lambda: n_ge)

    @pl.when(jnp.max(n_ge) > top_k)
    def _():
        n_gt = count(sc_ref, lambda blk, row0: blk > thr)
        need = top_k - n_gt
        row = lax.broadcasted_iota(jnp.int32, (part, tq), 0)
        index_bits = (sc_ref.shape[0] - 1).bit_length()

        def index_search(step, cut):
            cand = cut + jnp.left_shift(jnp.int32(1), index_bits - 1 - step)
            below = count(sc_ref, lambda blk, row0: (blk == thr) & (row0 + row < cand))
            return jnp.where(below < need, cand, cut)

        cut = lax.fori_loop(0, index_bits, index_search, jnp.zeros((1, tq), jnp.int32))

        def drop_chunk(c, carry):
            base = pl.multiple_of(c * ks, ks)
            blk = sc_ref[pl.ds(base, ks), :]
            pos = base + lax.broadcasted_iota(jnp.int32, (ks, tq), 0)
            sc_ref[pl.ds(base, ks), :] = jnp.where((blk == thr) & (pos > cut), -jnp.inf, blk)
            return carry

        lax.fori_loop(0, n_count, drop_chunk, 0)

    _softmax_init(m_ref, acc_ref)

    group = A_HEADS // 2

    def issue_qk(c, g, s_ref, width):
        base = pl.multiple_of(c * kc, kc)
        for u in range(group):
            h = g * group + u
            cols = slice((h // 2) * LANES, (h // 2 + 1) * LANES)
            s_ref[u, 0:width, :] = _dot(k_ref[pl.ds(base, width), cols], qp_ref[h])

    def consume(c, g, s_ref, last, width):
        base = pl.multiple_of(c * kc, kc)
        if g == 0:
            bias_ref[0:width, :] = jnp.where(
                sc_ref[pl.ds(base, width), :] >= thr, 0.0, NEG_BIG)
        for u in range(group):
            h = g * group + u
            v_t = vt_ref[h * A_HEAD_DIM:(h + 1) * A_HEAD_DIM, pl.ds(base, width)]
            s_t = s_ref[u, 0:width, :] + bias_ref[0:width, :]
            _softmax_step_t(s_t, v_t, m_ref, acc_ref, h)

    def issue_next_tile(s_ref):
        for j in range(group // 2):
            halves = head_halves(qn_ref[:, j * LANES:(j + 1) * LANES])
            for u in range(2):
                s_ref[2 * j + u] = _dot_nt(k_ref[0:kc, j * LANES:(j + 1) * LANES], halves[u])

    _attention_pipeline(n_chunks - 1, half_last, i == 0, kc, DSA_UNROLL, issue_qk, consume,
                        issue_next_tile, (s0_ref, s1_ref))
    _softmax_finish(acc_ref, o_ref)


def _dsa_attention(q, qi, wi, ki, k, vt, batch, seq):
    n = q.shape[0]
    tq = DSA_Q_TILE
    nq = seq // tq
    top_k = min(TOPK_MAX, seq // 4)
    assert A_HEAD_DIM == IDX_DIM == LANES // 2
    qrow = lambda width: pl.BlockSpec((tq, width), lambda b, i: (b * nq + i, 0))
    q_next = pl.BlockSpec((tq, A_WIDTH), lambda b, i: (b * nq + jnp.minimum(i + 1, nq - 1), 0))
    per_batch = lambda width: pl.BlockSpec((seq, width), lambda b, i: (b, 0))
    return pl.pallas_call(
        functools.partial(_dsa_kernel, top_k),
        grid=(batch, nq),
        in_specs=[qrow(A_WIDTH), q_next, qrow(IDX_HEADS * IDX_DIM), qrow(LANES),
                  per_batch(LANES), per_batch(A_WIDTH),
                  pl.BlockSpec((A_WIDTH, seq), lambda b, i: (0, b))],
        out_specs=pl.BlockSpec((A_WIDTH, tq), lambda b, i: (0, b * nq + i)),
        out_shape=jax.ShapeDtypeStruct((A_WIDTH, n), jnp.bfloat16),
        scratch_shapes=[
            pltpu.VMEM((seq, tq), jnp.float32),
            pltpu.VMEM((seq, tq), jnp.bfloat16),
            pltpu.VMEM((IDX_HEADS // 2, LANES, 2 * tq), jnp.bfloat16),
            pltpu.VMEM((A_HEADS, LANES, tq), jnp.bfloat16),
            pltpu.VMEM((A_HEADS, 1, tq), jnp.float32),
            pltpu.VMEM((A_HEADS, A_HEAD_DIM + SUM_ROWS, tq), jnp.float32),
            pltpu.VMEM((A_HEADS // 2, DSA_KEY_CHUNK, tq), jnp.float32),
            pltpu.VMEM((A_HEADS // 2, DSA_KEY_CHUNK, tq), jnp.float32),
            pltpu.VMEM((DSA_KEY_CHUNK, tq), jnp.float32),
        ],
        compiler_params=_params("arbitrary", "arbitrary"),
        name="dsa_attention",
    )(q, q, qi, wi, ki, k, vt)


def _dot_tn(a_t, b):
    return lax.dot_general(a_t, b, (((0,), (0,)), ((), ())), preferred_element_type=jnp.float32)


def _mix_mlp_kernel(n_in, *refs):
    a_refs, wo_refs = refs[:n_in], refs[n_in:2 * n_in]
    (x_ref, gmix_ref, gpre_ref, gpost_ref, w1_ref, w2_ref, o_ref,
     x1_ref, hn_ref, acc_ref) = refs[2 * n_in:]
    f = pl.program_id(1)

    @pl.when(f == 0)
    def _():
        y = _dot_tn(a_refs[0][...], wo_refs[0][...])
        for a_ref, wo_ref in zip(a_refs[1:], wo_refs[1:]):
            y = y + _dot(a_ref[...], wo_ref[...])
        x1 = x_ref[...] + _rms(y, gmix_ref[...])
        x1_ref[...] = x1
        hn_ref[...] = _rms(x1, gpre_ref[...]).astype(jnp.bfloat16)
        acc_ref[...] = jnp.zeros(acc_ref.shape, jnp.float32)

    h = jnp.maximum(_dot(hn_ref[...], w1_ref[...]), 0.0)
    acc_ref[...] += _dot((h * h).astype(jnp.bfloat16), w2_ref[...])

    @pl.when(f == pl.num_programs(1) - 1)
    def _():
        o_ref[...] = x1_ref[...] + _rms(acc_ref[...], gpost_ref[...])


def _mix_mlp(acts, w_outs, x, g_mix, gpre, gpost, w1, w2):
    n, d = x.shape
    ff = w1.shape[1]
    tm, tf = MLP_ROW_TILE, MLP_FF_TILE
    row = lambda a: pl.BlockSpec((tm, a.shape[1]), lambda i, f: (i, 0))
    col = lambda a: pl.BlockSpec((a.shape[0], tm), lambda i, f: (0, i))
    full = lambda a: pl.BlockSpec(a.shape, lambda i, f: (0,) * a.ndim)
    return pl.pallas_call(
        functools.partial(_mix_mlp_kernel, len(acts)),
        grid=(n // tm, ff // tf),
        in_specs=[col(acts[0])] + [row(a) for a in acts[1:]] + [full(w) for w in w_outs]
        + [row(x), full(g_mix), full(gpre), full(gpost),
           pl.BlockSpec((d, tf), lambda i, f: (0, f)),
           pl.BlockSpec((tf, d), lambda i, f: (f, 0))],
        out_specs=row(x),
        out_shape=jax.ShapeDtypeStruct((n, d), jnp.float32),
        scratch_shapes=[pltpu.VMEM((tm, d), jnp.float32), pltpu.VMEM((tm, d), jnp.bfloat16),
                        pltpu.VMEM((tm, d), jnp.float32)],
        compiler_params=_params("arbitrary", "arbitrary"),
        name="mix_mlp",
    )(*acts, *w_outs, x, g_mix, gpre, gpost, w1, w2)


_DKR, _DKV, _DEND = C_Q_RANK, C_Q_RANK + LANES, C_Q_RANK + LANES + C_KV_RANK


def _mla_in_kernel(x_ref, g_ref, wd_ref, qn_ref, kvn_ref, wuq_ref, wuk_ref, wuvt_ref,
                   c_ref, s_ref, q_ref, k_ref, vt_ref):
    hn = _rms(x_ref[...], g_ref[...]).astype(jnp.bfloat16)
    tm = x_ref.shape[0]
    pad = jnp.zeros((LANES - C_NOPE_DIM - C_ROPE_DIM, tm), jnp.float32)
    c = jnp.concatenate([jnp.ones((C_NOPE_DIM, tm), jnp.float32), c_ref[...], pad], axis=0).T
    s = jnp.concatenate([jnp.zeros((C_NOPE_DIM, tm), jnp.float32), s_ref[...], pad], axis=0).T
    scale = (C_NOPE_DIM + C_ROPE_DIM) ** -0.5 * LOG2E
    cq, sq = c * scale, s * scale

    def rope(y, ct, st):
        return y * ct + pltpu.roll(y, LANES - C_ROPE_DIM, axis=1) * st

    kvd = _dot(hn, wd_ref[:, _DKV:_DEND])
    qr = _dot(hn, wd_ref[:, 0:_DKV])
    ckv = _rms(kvd, kvn_ref[...]).astype(jnp.bfloat16)
    vt_ref[...] = _dot_nt(wuvt_ref[...], ckv).astype(vt_ref.dtype)
    kr = rope(qr[:, _DKR:], c, s)
    n_wide = C_HEADS * LANES // MXU_COLS
    for j in range(n_wide):
        k2 = _dot(ckv, wuk_ref[:, j * MXU_COLS:(j + 1) * MXU_COLS])
        for u in range(MXU_COLS // LANES):
            cols = slice(j * MXU_COLS + u * LANES, j * MXU_COLS + (u + 1) * LANES)
            k_ref[:, cols] = (k2[:, u * LANES:(u + 1) * LANES] + kr).astype(k_ref.dtype)

    qa = _rms(qr[:, :_DKR], qn_ref[...]).astype(jnp.bfloat16)
    up = lambda j: _dot(qa, wuq_ref[:, j * MXU_COLS:(j + 1) * MXU_COLS])
    nxt = up(0)
    for j in range(n_wide):
        q2 = nxt
        if j + 1 < n_wide:
            nxt = up(j + 1)
        for u in range(MXU_COLS // LANES):
            cols = slice(j * MXU_COLS + u * LANES, j * MXU_COLS + (u + 1) * LANES)
            q_ref[:, cols] = rope(q2[:, u * LANES:(u + 1) * LANES], cq, sq).astype(q_ref.dtype)


def _mla_in_proj(x, gain, w_down, q_norm, kv_norm, w_uq, w_uk, w_uvt, tabs):
    n, d = x.shape
    tm = MLA_ROW_TILE
    row = lambda width: pl.BlockSpec((tm, width), lambda i: (i, 0))
    full = lambda a: pl.BlockSpec(a.shape, lambda i: (0,) * a.ndim)
    tab = pl.BlockSpec((C_ROPE_DIM, tm), lambda i: (0, i))
    bf = jnp.bfloat16
    return pl.pallas_call(
        _mla_in_kernel,
        grid=(n // tm,),
        in_specs=[row(d), full(gain), full(w_down), full(q_norm), full(kv_norm),
                  full(w_uq), full(w_uk), full(w_uvt), tab, tab],
        out_specs=(row(C_HEADS * LANES), row(C_HEADS * LANES),
                   pl.BlockSpec((C_HEADS * C_V_DIM, tm), lambda i: (0, i))),
        out_shape=(jax.ShapeDtypeStruct((n, C_HEADS * LANES), bf),
                   jax.ShapeDtypeStruct((n, C_HEADS * LANES), bf),
                   jax.ShapeDtypeStruct((C_HEADS * C_V_DIM, n), bf)),
        compiler_params=_params("arbitrary"),
        name="mla_in_proj",
    )(x, gain, w_down, q_norm, kv_norm, w_uq, w_uk, w_uvt, *tabs)


def _mla_attn_kernel(q_ref, qn_ref, k_ref, vt_ref, o_ref,
                     qt_ref, m_ref, acc_ref, s0_ref, s1_ref):
    i = pl.program_id(2)
    tq, kc, group = MLA_Q_TILE, MLA_KEY_CHUNK, MLA_HEAD_GROUP // 2
    n_full = (i * tq) // kc
    _softmax_init(m_ref, acc_ref)
    for h in range(MLA_HEAD_GROUP):
        qt_ref[h] = q_ref[:, h * LANES:(h + 1) * LANES].astype(jnp.float32).T.astype(
            jnp.bfloat16)

    def issue_qk(c, g, s_ref, width):
        base = pl.multiple_of(c * kc, kc)
        for u in range(group):
            h = g * group + u
            s_ref[u, 0:width, :] = _dot(
                k_ref[pl.ds(base, width), h * LANES:(h + 1) * LANES], qt_ref[h])

    def consume(c, g, s_ref, last, width):
        base = pl.multiple_of(c * kc, kc)
        if last:
            q_pos = i * tq + lax.broadcasted_iota(jnp.int32, (width, tq), 1)
            k_pos = base + lax.broadcasted_iota(jnp.int32, (width, tq), 0)
            bias_t = jnp.where(k_pos <= q_pos, 0.0, NEG_BIG)
        for u in range(group):
            h = g * group + u
            s_t = s_ref[u, 0:width, :]
            if last:
                s_t = s_t + bias_t
            v_t = vt_ref[h * C_V_DIM:(h + 1) * C_V_DIM, pl.ds(base, width)]
            _softmax_step_t(s_t, v_t, m_ref, acc_ref, h)

    def issue_next_tile(s_ref):
        for u in range(group):
            cols = slice(u * LANES, (u + 1) * LANES)
            s_ref[u] = _dot_nt(k_ref[0:kc, cols], qn_ref[:, cols])

    half_last = (i + 1) * tq - n_full * kc <= kc // 2
    _attention_pipeline(n_full, half_last, i == 0, kc, MLA_UNROLL, issue_qk, consume,
                        issue_next_tile, (s0_ref, s1_ref))
    _softmax_finish(acc_ref, o_ref)


def _mla_attention(q, k, vt, batch, seq):
    n = q.shape[0]
    tq, hg = MLA_Q_TILE, MLA_HEAD_GROUP
    nq = seq // tq
    return pl.pallas_call(
        _mla_attn_kernel,
        grid=(batch, C_HEADS // hg, nq),
        in_specs=[pl.BlockSpec((tq, hg * LANES), lambda b, g, i: (b * nq + i, g)),
                  pl.BlockSpec((tq, hg * LANES),
                               lambda b, g, i: (b * nq + jnp.minimum(i + 1, nq - 1), g)),
                  pl.BlockSpec((seq, hg * LANES), lambda b, g, i: (b, g)),
                  pl.BlockSpec((hg * C_V_DIM, seq), lambda b, g, i: (g, b))],
        out_specs=pl.BlockSpec((hg * C_V_DIM, tq), lambda b, g, i: (g, b * nq + i)),
        out_shape=jax.ShapeDtypeStruct((C_HEADS * C_V_DIM, n), jnp.bfloat16),
        scratch_shapes=[pltpu.VMEM((hg, LANES, tq), jnp.bfloat16),
                        pltpu.VMEM((hg, 1, tq), jnp.float32),
                        pltpu.VMEM((hg, C_V_DIM + SUM_ROWS, tq), jnp.float32),
                        pltpu.VMEM((hg // 2, MLA_KEY_CHUNK, tq), jnp.float32),
                        pltpu.VMEM((hg // 2, MLA_KEY_CHUNK, tq), jnp.float32)],
        compiler_params=_params("arbitrary", "arbitrary", "arbitrary"),
        name="mla_attention",
    )(q, q, k, vt)


def _rope_rows(positions, rot_dim):
    inv_freq = ROPE_THETA ** (-jnp.arange(0, rot_dim, 2, dtype=jnp.float32) / rot_dim)
    ang = inv_freq[:, None] * positions.astype(jnp.float32).reshape(1, -1)
    return jnp.cos(ang), jnp.sin(ang)


def _even_rope_tables(positions):
    cos, sin = _rope_rows(positions, A_ROT_DIM)
    zero = jnp.zeros_like(sin)
    return (jnp.concatenate([cos, cos], axis=0), jnp.concatenate([-sin, zero], axis=0),
            jnp.concatenate([zero, sin], axis=0))


def _pack_even_w_in(w):
    d = w.shape[0]
    w = w.astype(jnp.bfloat16)
    q, k, v, qi, ki, wi, gb, gc, xi = jnp.split(
        w, [512, 1024, 1536, 2560, 2624, 2640, 3152, 3664], axis=1)
    pad = jnp.zeros((d, LANES - IDX_HEADS), w.dtype)
    return jnp.concatenate([q, k, qi, ki, ki, wi, pad, gb, gc, xi], axis=1), v.T


def _pad_heads(w, heads, width):
    r = w.shape[0]
    w = w.reshape(r, heads, width)
    return jnp.pad(w, ((0, 0), (0, 0), (0, LANES - width))).reshape(r, heads * LANES)


def _rope_swapped(w_rope):
    half = C_ROPE_DIM // 2
    return jnp.concatenate([w_rope, w_rope[..., half:], w_rope[..., :half]], axis=-1)


def _mla_rope_tables(positions):
    cos, sin = _rope_rows(positions, C_ROPE_DIM)
    return jnp.concatenate([cos, cos], axis=0), jnp.concatenate([-sin, sin], axis=0)


def kernel(x, positions, norm_mix_pre, norm_mix_post, norm_ffn_pre, norm_ffn_post, even_w_in, even_conv_w, even_w_out, odd_w_dq, odd_q_norm, odd_w_uq, odd_w_dkv, odd_kv_norm, odd_w_ukv, odd_w_o, mlp_w1, mlp_w2):
    batch, seq, d = x.shape
    depth = norm_mix_pre.shape[0]
    bf = jnp.bfloat16
    tabs_a = _even_rope_tables(positions)
    tabs_c = _mla_rope_tables(positions)
    h = x.reshape(batch * seq, d)
    row = lambda a: a.reshape(1, -1)
    for layer in range(depth):
        j = layer // 2
        g_pre, g_post = row(norm_mix_pre[layer]), row(norm_mix_post[layer])
        if layer % 2 == 0:
            w_packed, w_vt = _pack_even_w_in(even_w_in[j])
            q, k, vt, qi, ki, wi, conv = _even_in_proj(
                h, g_pre, w_packed, w_vt, tabs_a, even_conv_w[j], seq)
            attn = _dsa_attention(q, qi, wi, ki, k, vt, batch, seq)
            w_out = even_w_out[j].astype(bf)
            acts, w_outs = [attn, conv], [w_out[:A_WIDTH], w_out[A_WIDTH:]]
        else:
            w_dkv = odd_w_dkv[j].astype(bf)
            w_down = jnp.concatenate(
                [odd_w_dq[j].astype(bf), jnp.zeros((d, C_NOPE_DIM), bf),
                 _rope_swapped(w_dkv[:, C_KV_RANK:]), w_dkv[:, :C_KV_RANK]], axis=1)
            w_ukv = odd_w_ukv[j].astype(bf).reshape(C_KV_RANK, C_HEADS, C_NOPE_DIM + C_V_DIM)
            w_uk = _pad_heads(w_ukv[:, :, :C_NOPE_DIM].reshape(C_KV_RANK, -1), C_HEADS, C_NOPE_DIM)
            w_uv = w_ukv[:, :, C_NOPE_DIM:].reshape(C_KV_RANK, -1)
            w_uq = odd_w_uq[j].astype(bf).reshape(C_Q_RANK, C_HEADS, C_NOPE_DIM + C_ROPE_DIM)
            w_uq = jnp.concatenate(
                [w_uq[:, :, :C_NOPE_DIM], _rope_swapped(w_uq[:, :, C_NOPE_DIM:])],
                axis=-1).reshape(C_Q_RANK, C_HEADS * LANES)
            q, k, vt = _mla_in_proj(h, g_pre, w_down, row(odd_q_norm[j]), row(odd_kv_norm[j]),
                                    w_uq, w_uk, w_uv.T, tabs_c)
            acts, w_outs = [_mla_attention(q, k, vt, batch, seq)], [odd_w_o[j].astype(bf)]
        h = _mix_mlp(acts, w_outs, h, g_post, row(norm_ffn_pre[layer]),
                     row(norm_ffn_post[layer]), mlp_w1[layer].astype(bf),
                     mlp_w2[layer].astype(bf))
    return h.reshape(batch, seq, d)
```

```python
import functools

import jax
import jax.numpy as jnp
from jax import lax
from jax.experimental import pallas as pl
from jax.experimental.pallas import tpu as pltpu

ROPE_THETA = 500000.0
NORM_EPS = 1e-6
LANES = 128
MXU_COLS = 256
NEG_BIG = -1e30
LOG2E = 1.4426950408889634
SUM_ROWS = 16

A_HEADS = 8
A_HEAD_DIM = 64
A_WIDTH = A_HEADS * A_HEAD_DIM
A_ROT_DIM = A_HEAD_DIM // 4
IDX_HEADS = 16
IDX_DIM = 64
TOPK_MAX = 256
B_WIDTH = 512
CONV_WIDTH = 3
C_HEADS = 16
C_NOPE_DIM = 64
C_ROPE_DIM = 32
C_V_DIM = 64
C_Q_RANK = 384
C_KV_RANK = 256

KEY_LOWEST_FINITE = -2139095040
COARSE_SLACK = 0x8000
KEY_BITS = 32
COARSE_BITS = 16

ROW_TILE = 512
MLA_ROW_TILE = 512
MLP_ROW_TILE = 1024
MLP_FF_TILE = 1024
DSA_Q_TILE = 256
DSA_KEY_CHUNK = 512
DSA_COUNT_CHUNK = 256
DSA_COUNT_ROWS = 64
DSA_UNROLL = 4
DSA_SCORE_UNROLL = 4
MLA_UNROLL = 4
MLA_Q_TILE = 256
MLA_KEY_CHUNK = 512
MLA_HEAD_GROUP = 8
VMEM_LIMIT = 56 * 1024 * 1024


def _dot(a, b):
    return jnp.dot(a, b, preferred_element_type=jnp.float32)


def _dot_nt(a, b):
    return lax.dot_general(a, b, (((1,), (1,)), ((), ())), preferred_element_type=jnp.float32)


def _rms(x, gain):
    ms = jnp.mean(x * x, axis=-1, keepdims=True)
    return x * lax.rsqrt(ms + NORM_EPS) * gain


def _rope_tile(x, c, s_next, s_prev, half):
    return (x * c + pltpu.roll(x, LANES - half, axis=1) * s_next
            + pltpu.roll(x, half, axis=1) * s_prev)


def _params(*semantics):
    return pltpu.CompilerParams(dimension_semantics=semantics, vmem_limit_bytes=VMEM_LIMIT)


_EQ, _EK, _EQI, _EKI, _EWI, _EGB, _EGC, _EXI, _EEND = (
    0, 512, 1024, 2048, 2176, 2304, 2816, 3328, 3840)


def _even_in_kernel(seq_tiles, x_ref, g_ref, w_ref, wvt_ref, c_ref, sn_ref, sp_ref, cw_ref,
                    q_ref, k_ref, vt_ref, qi_ref, ki_ref, wi_ref, conv_ref, carry_ref):
    i = pl.program_id(0)
    tm = x_ref.shape[0]
    hn = _rms(x_ref[...], g_ref[...]).astype(jnp.bfloat16)
    half = A_ROT_DIM // 2

    def lanes_of(rows, fill):
        rest = jnp.full((A_HEAD_DIM - A_ROT_DIM, tm), fill, jnp.float32)
        return jnp.concatenate([rows, rest, rows, rest], axis=0).T

    c, sn, sp = lanes_of(c_ref[...], 1.0), lanes_of(sn_ref[...], 0.0), lanes_of(sp_ref[...], 0.0)

    def rope_section(start, width, out_ref, scale=None):
        for t in range(width // MXU_COLS):
            y2 = _dot(hn, w_ref[:, start + t * MXU_COLS:start + (t + 1) * MXU_COLS])
            for u in range(MXU_COLS // LANES):
                y = _rope_tile(y2[:, u * LANES:(u + 1) * LANES], c, sn, sp, half)
                if scale is not None:
                    y = y * scale
                lo = t * MXU_COLS + u * LANES
                out_ref[:, lo:lo + LANES] = y.astype(out_ref.dtype)

    gate_b = _dot(hn, w_ref[:, _EGB:_EGC])
    u = _dot(hn, w_ref[:, _EGC:_EXI]) * _dot(hn, w_ref[:, _EXI:_EEND])
    seq_start = (i % seq_tiles) == 0
    prev = jnp.where(seq_start, 0.0, carry_ref[...])
    carry_ref[...] = u[tm - 8:, :]
    row = lax.broadcasted_iota(jnp.int32, u.shape, 0)
    u1 = jnp.where(row == 0, prev[7:8, :], pltpu.roll(u, 1, axis=0))
    u2 = pltpu.roll(u, 2, axis=0)
    u2 = jnp.where(row == 0, prev[6:7, :], jnp.where(row == 1, prev[7:8, :], u2))
    y = cw_ref[0:1, :] * u2 + cw_ref[1:2, :] * u1 + cw_ref[2:3, :] * u
    conv_ref[...] = (gate_b * y).astype(conv_ref.dtype)

    rope_section(_EQ, A_WIDTH, q_ref, A_HEAD_DIM ** -0.5 * LOG2E)
    rope_section(_EK, A_WIDTH, k_ref)
    rope_section(_EQI, IDX_HEADS * IDX_DIM, qi_ref)
    kw = _dot(hn, w_ref[:, _EKI:_EGB])
    ki_ref[...] = _rope_tile(kw[:, :LANES], c, sn, sp, half).astype(ki_ref.dtype)
    wi_ref[...] = kw[:, LANES:] * (IDX_HEADS ** -0.5 * IDX_DIM ** -0.5)
    vt_ref[...] = _dot_nt(wvt_ref[...], hn).astype(vt_ref.dtype)


def _even_in_proj(x, gain, w_packed, w_vt, tabs, conv_w, seq):
    n, d = x.shape
    tm = ROW_TILE
    row = lambda width: pl.BlockSpec((tm, width), lambda i: (i, 0))
    full = lambda a: pl.BlockSpec(a.shape, lambda i: (0,) * a.ndim)
    tab = pl.BlockSpec((A_ROT_DIM, tm), lambda i: (0, i))
    bf = jnp.bfloat16
    out_shape = (
        jax.ShapeDtypeStruct((n, A_WIDTH), bf), jax.ShapeDtypeStruct((n, A_WIDTH), bf),
        jax.ShapeDtypeStruct((A_WIDTH, n), bf), jax.ShapeDtypeStruct((n, IDX_HEADS * IDX_DIM), bf),
        jax.ShapeDtypeStruct((n, LANES), bf), jax.ShapeDtypeStruct((n, LANES), jnp.float32),
        jax.ShapeDtypeStruct((n, B_WIDTH), bf))
    return pl.pallas_call(
        functools.partial(_even_in_kernel, seq // tm),
        grid=(n // tm,),
        in_specs=[row(d), full(gain), full(w_packed), full(w_vt), tab, tab, tab,
                  full(conv_w)],
        out_specs=(row(A_WIDTH), row(A_WIDTH), pl.BlockSpec((A_WIDTH, tm), lambda i: (0, i)),
                   row(IDX_HEADS * IDX_DIM), row(LANES), row(LANES), row(B_WIDTH)),
        out_shape=out_shape,
        scratch_shapes=[pltpu.VMEM((8, B_WIDTH), jnp.float32)],
        compiler_params=_params("arbitrary"),
        name="even_in_proj",
    )(x, gain, w_packed, w_vt, *tabs, conv_w)


def _key_to_float(key):
    bits = key ^ ((key >> 31) & jnp.int32(0x7FFFFFFF))
    return lax.bitcast_convert_type(bits, jnp.float32)


def _softmax_step_t(s_t, v_t, m_ref, acc_ref, h):
    kc = s_t.shape[0]
    m_prev = m_ref[h]
    m_new = jnp.maximum(m_prev, jnp.max(s_t, axis=0, keepdims=True))
    alpha = jnp.exp2(m_prev - m_new)
    p_t = jnp.exp2(s_t - m_new).astype(jnp.bfloat16)
    v_aug = jnp.concatenate([v_t, jnp.ones((SUM_ROWS, kc), jnp.bfloat16)], axis=0)
    acc_ref[h] = alpha * acc_ref[h] + _dot(v_aug, p_t)
    m_ref[h] = m_new


def _softmax_init(m_ref, acc_ref):
    m_ref[...] = jnp.full(m_ref.shape, NEG_BIG, jnp.float32)
    acc_ref[...] = jnp.zeros(acc_ref.shape, jnp.float32)


def _softmax_finish(acc_ref, o_ref):
    heads, rows, _ = acc_ref.shape
    dv = rows - SUM_ROWS
    o_t = jnp.concatenate(
        [acc_ref[h, 0:dv, :] * (1.0 / acc_ref[h, dv:dv + 1, :]) for h in range(heads)], axis=0)
    o_ref[...] = o_t.astype(o_ref.dtype)


def _unrolled_loop(n, unroll, step):
    def steps(first, count):
        for t in range(count):
            step(first + t)

    def unrolled(p, carry):
        steps(unroll * p, unroll)
        return carry

    lax.fori_loop(0, n // unroll, unrolled, 0)
    done = (n // unroll) * unroll
    size = unroll // 2
    while size >= 1:
        has = (n & size) != 0
        pl.when(has)(functools.partial(steps, done, size))
        done = done + jnp.where(has, size, 0)
        size //= 2


def _attention_pipeline(n_last, half_last, first_tile, kc, unroll, issue_qk, consume,
                        issue_next_tile, s_refs):
    s0, s1 = s_refs
    pl.when(first_tile)(lambda: issue_qk(0, 0, s0, kc))

    def step(c):
        issue_qk(c, 1, s1, kc)
        consume(c, 0, s0, False, kc)
        issue_qk(c + 1, 0, s0, kc)
        consume(c, 1, s1, False, kc)

    _unrolled_loop(n_last, unroll, step)

    def last_chunk(width):
        issue_qk(n_last, 1, s1, width)
        consume(n_last, 0, s0, True, width)
        issue_next_tile(s0)
        consume(n_last, 1, s1, True, width)

    pl.when(half_last)(lambda: last_chunk(kc // 2))
    pl.when(jnp.logical_not(half_last))(lambda: last_chunk(kc))


def _dsa_kernel(top_k, q_ref, qn_ref, qi_ref, wi_ref, ki_ref, k_ref, vt_ref, o_ref,
                sc_ref, hb_ref, qm_ref, qp_ref, m_ref, acc_ref, s0_ref, s1_ref, bias_ref):
    i = pl.program_id(1)
    tq, kc, ks = DSA_Q_TILE, DSA_KEY_CHUNK, DSA_COUNT_CHUNK
    n_chunks = (i * tq) // kc + 1
    n_count = (i * tq) // ks + 1
    low = lax.broadcasted_iota(jnp.int32, (tq, LANES), 1) < IDX_DIM

    def head_halves(tile):
        t = tile.astype(jnp.float32)
        return (jnp.where(low, t, 0.0).astype(jnp.bfloat16),
                jnp.where(low, 0.0, t).astype(jnp.bfloat16))

    top = lax.broadcasted_iota(jnp.int32, (LANES, tq), 0) < IDX_DIM

    def head_halves_t(tile):
        t = tile.astype(jnp.float32).T
        return (jnp.where(top, t, 0.0).astype(jnp.bfloat16),
                jnp.where(top, 0.0, t).astype(jnp.bfloat16))

    for j in range(IDX_HEADS // 2):
        qm_ref[j, :, 0:tq], qm_ref[j, :, tq:2 * tq] = head_halves_t(
            qi_ref[:, j * LANES:(j + 1) * LANES])
    for j in range(A_HEADS // 2):
        qp_ref[2 * j], qp_ref[2 * j + 1] = head_halves_t(q_ref[:, j * LANES:(j + 1) * LANES])
    w_t = wi_ref[...].T

    half_last = (i + 1) * tq - (n_chunks - 1) * kc <= kc // 2

    def score_chunk(c, width=kc):
        base = pl.multiple_of(c * kc, kc)
        kch = ki_ref[pl.ds(base, width), :]
        acc = jnp.zeros((width, tq), jnp.float32)
        for j in range(IDX_HEADS // 2):
            r = _dot(kch, qm_ref[j])
            acc = acc + jnp.maximum(r[:, :tq], 0.0) * w_t[2 * j:2 * j + 1, :]
            acc = acc + jnp.maximum(r[:, tq:], 0.0) * w_t[2 * j + 1:2 * j + 2, :]
        q_pos = i * tq + lax.broadcasted_iota(jnp.int32, (width, tq), 1)
        k_pos = base + lax.broadcasted_iota(jnp.int32, (width, tq), 0)
        masked = jnp.where(k_pos <= q_pos, acc, -jnp.inf)
        sc_ref[pl.ds(base, width), :] = masked
        hb_ref[pl.ds(base, width), :] = masked.astype(jnp.bfloat16)

    _unrolled_loop(n_chunks - 1, DSA_SCORE_UNROLL, score_chunk)
    pl.when(half_last)(lambda: score_chunk(n_chunks - 1, kc // 2))
    pl.when(jnp.logical_not(half_last))(lambda: score_chunk(n_chunks - 1, kc))

    part = DSA_COUNT_ROWS

    def count(src_ref, pred):
        one = jnp.ones((), src_ref.dtype)

        def count_chunk(c, cnt):
            base = pl.multiple_of(c * ks, ks)
            for r in range(ks // part):
                row0 = base + r * part
                cnt = jnp.where(pred(src_ref[pl.ds(row0, part), :], row0), cnt + one, cnt)
            return cnt

        cnt = lax.fori_loop(0, n_count, count_chunk, jnp.zeros((part, tq), src_ref.dtype))
        return jnp.sum(cnt.astype(jnp.float32), axis=0, keepdims=True)

    def coarse(step, lo):
        cand = lo + jnp.left_shift(jnp.int32(1), KEY_BITS - 1 - step)
        cand_b = jnp.broadcast_to(_key_to_float(cand).astype(jnp.bfloat16), (part, tq))
        total = count(hb_ref, lambda blk, row0: blk >= cand_b)
        return jnp.where(total >= top_k, cand, lo)

    k_coarse = lax.fori_loop(
        0, COARSE_BITS, coarse, jnp.full((1, tq), jnp.iinfo(jnp.int32).min, jnp.int32))

    lo0 = jnp.maximum(k_coarse, KEY_LOWEST_FINITE + COARSE_SLACK) - COARSE_SLACK
    hi0 = jnp.maximum(k_coarse + 2 * COARSE_SLACK, lo0)

    def fine(step, carry):
        lo, hi, n_lo = carry
        mid = lo + jnp.right_shift(hi - lo + 1, 1)
        mid_f = _key_to_float(mid)
        total = count(sc_ref, lambda blk, row0: blk >= mid_f)
        take = total >= top_k
        return jnp.where(take, mid, lo), jnp.where(take, hi, mid - 1), jnp.where(take, total, n_lo)

    unknown = jnp.full((1, tq), -1.0, jnp.float32)
    lo, _, n_ge = lax.fori_loop(0, (3 * COARSE_SLACK).bit_length(), fine, (lo0, hi0, unknown))
    thr = _key_to_float(lo)
    n_ge = lax.cond(jnp.min(n_ge) < 0.0,
                    lambda: count(sc_ref, lambda blk, row0: blk >= thr), lambda: n_ge)

    @pl.when(jnp.max(n_ge) > top_k)
    def _():
        n_gt = count(sc_ref, lambda blk, row0: blk > thr)
        need = top_k - n_gt
        row = lax.broadcasted_iota(jnp.int32, (part, tq), 0)
        index_bits = (sc_ref.shape[0] - 1).bit_length()

        def index_search(step, cut):
            cand = cut + jnp.left_shift(jnp.int32(1), index_bits - 1 - step)
            below = count(sc_ref, lambda blk, row0: (blk == thr) & (row0 + row < cand))
            return jnp.where(below < need, cand, cut)

        cut = lax.fori_loop(0, index_bits, index_search, jnp.zeros((1, tq), jnp.int32))

        def drop_chunk(c, carry):
            base = pl.multiple_of(c * ks, ks)
            blk = sc_ref[pl.ds(base, ks), :]
            pos = base + lax.broadcasted_iota(jnp.int32, (ks, tq), 0)
            sc_ref[pl.ds(base, ks), :] = jnp.where((blk == thr) & (pos > cut), -jnp.inf, blk)
            return carry

        lax.fori_loop(0, n_count, drop_chunk, 0)

    _softmax_init(m_ref, acc_ref)

    group = A_HEADS // 2

    def issue_qk(c, g, s_ref, width):
        base = pl.multiple_of(c * kc, kc)
        for u in range(group):
            h = g * group + u
            cols = slice((h // 2) * LANES, (h // 2 + 1) * LANES)
            s_ref[u, 0:width, :] = _dot(k_ref[pl.ds(base, width), cols], qp_ref[h])

    def consume(c, g, s_ref, last, width):
        base = pl.multiple_of(c * kc, kc)
        if g == 0:
            bias_ref[0:width, :] = jnp.where(
                sc_ref[pl.ds(base, width), :] >= thr, 0.0, NEG_BIG)
        for u in range(group):
            h = g * group + u
            v_t = vt_ref[h * A_HEAD_DIM:(h + 1) * A_HEAD_DIM, pl.ds(base, width)]
            s_t = s_ref[u, 0:width, :] + bias_ref[0:width, :]
            _softmax_step_t(s_t, v_t, m_ref, acc_ref, h)

    def issue_next_tile(s_ref):
        for j in range(group // 2):
            halves = head_halves(qn_ref[:, j * LANES:(j + 1) * LANES])
            for u in range(2):
                s_ref[2 * j + u] = _dot_nt(k_ref[0:kc, j * LANES:(j + 1) * LANES], halves[u])

    _attention_pipeline(n_chunks - 1, half_last, i == 0, kc, DSA_UNROLL, issue_qk, consume,
                        issue_next_tile, (s0_ref, s1_ref))
    _softmax_finish(acc_ref, o_ref)


def _dsa_attention(q, qi, wi, ki, k, vt, batch, seq):
    n = q.shape[0]
    tq = DSA_Q_TILE
    nq = seq // tq
    top_k = min(TOPK_MAX, seq // 4)
    assert A_HEAD_DIM == IDX_DIM == LANES // 2
    qrow = lambda width: pl.BlockSpec((tq, width), lambda b, i: (b * nq + i, 0))
    q_next = pl.BlockSpec((tq, A_WIDTH), lambda b, i: (b * nq + jnp.minimum(i + 1, nq - 1), 0))
    per_batch = lambda width: pl.BlockSpec((seq, width), lambda b, i: (b, 0))
    return pl.pallas_call(
        functools.partial(_dsa_kernel, top_k),
        grid=(batch, nq),
        in_specs=[qrow(A_WIDTH), q_next, qrow(IDX_HEADS * IDX_DIM), qrow(LANES),
                  per_batch(LANES), per_batch(A_WIDTH),
                  pl.BlockSpec((A_WIDTH, seq), lambda b, i: (0, b))],
        out_specs=pl.BlockSpec((A_WIDTH, tq), lambda b, i: (0, b * nq + i)),
        out_shape=jax.ShapeDtypeStruct((A_WIDTH, n), jnp.bfloat16),
        scratch_shapes=[
            pltpu.VMEM((seq, tq), jnp.float32),
            pltpu.VMEM((seq, tq), jnp.bfloat16),
            pltpu.VMEM((IDX_HEADS // 2, LANES, 2 * tq), jnp.bfloat16),
            pltpu.VMEM((A_HEADS, LANES, tq), jnp.bfloat16),
            pltpu.VMEM((A_HEADS, 1, tq), jnp.float32),
            pltpu.VMEM((A_HEADS, A_HEAD_DIM + SUM_ROWS, tq), jnp.float32),
            pltpu.VMEM((A_HEADS // 2, DSA_KEY_CHUNK, tq), jnp.float32),
            pltpu.VMEM((A_HEADS // 2, DSA_KEY_CHUNK, tq), jnp.float32),
            pltpu.VMEM((DSA_KEY_CHUNK, tq), jnp.float32),
        ],
        compiler_params=_params("arbitrary", "arbitrary"),
        name="dsa_attention",
    )(q, q, qi, wi, ki, k, vt)


def _dot_tn(a_t, b):
    return lax.dot_general(a_t, b, (((0,), (0,)), ((), ())), preferred_element_type=jnp.float32)


def _mix_mlp_kernel(n_in, *refs):
    a_refs, wo_refs = refs[:n_in], refs[n_in:2 * n_in]
    (x_ref, gmix_ref, gpre_ref, gpost_ref, w1_ref, w2_ref, o_ref,
     x1_ref, hn_ref, acc_ref) = refs[2 * n_in:]
    f = pl.program_id(1)

    @pl.when(f == 0)
    def _():
        y = _dot_tn(a_refs[0][...], wo_refs[0][...])
        for a_ref, wo_ref in zip(a_refs[1:], wo_refs[1:]):
            y = y + _dot(a_ref[...], wo_ref[...])
        x1 = x_ref[...] + _rms(y, gmix_ref[...])
        x1_ref[...] = x1
        hn_ref[...] = _rms(x1, gpre_ref[...]).astype(jnp.bfloat16)
        acc_ref[...] = jnp.zeros(acc_ref.shape, jnp.float32)

    h = jnp.maximum(_dot(hn_ref[...], w1_ref[...]), 0.0)
    acc_ref[...] += _dot((h * h).astype(jnp.bfloat16), w2_ref[...])

    @pl.when(f == pl.num_programs(1) - 1)
    def _():
        o_ref[...] = x1_ref[...] + _rms(acc_ref[...], gpost_ref[...])


def _mix_mlp(acts, w_outs, x, g_mix, gpre, gpost, w1, w2):
    n, d = x.shape
    ff = w1.shape[1]
    tm, tf = MLP_ROW_TILE, MLP_FF_TILE
    row = lambda a: pl.BlockSpec((tm, a.shape[1]), lambda i, f: (i, 0))
    col = lambda a: pl.BlockSpec((a.shape[0], tm), lambda i, f: (0, i))
    full = lambda a: pl.BlockSpec(a.shape, lambda i, f: (0,) * a.ndim)
    return pl.pallas_call(
        functools.partial(_mix_mlp_kernel, len(acts)),
        grid=(n // tm, ff // tf),
        in_specs=[col(acts[0])] + [row(a) for a in acts[1:]] + [full(w) for w in w_outs]
        + [row(x), full(g_mix), full(gpre), full(gpost),
           pl.BlockSpec((d, tf), lambda i, f: (0, f)),
           pl.BlockSpec((tf, d), lambda i, f: (f, 0))],
        out_specs=row(x),
        out_shape=jax.ShapeDtypeStruct((n, d), jnp.float32),
        scratch_shapes=[pltpu.VMEM((tm, d), jnp.float32), pltpu.VMEM((tm, d), jnp.bfloat16),
                        pltpu.VMEM((tm, d), jnp.float32)],
        compiler_params=_params("arbitrary", "arbitrary"),
        name="mix_mlp",
    )(*acts, *w_outs, x, g_mix, gpre, gpost, w1, w2)


_DKR, _DKV, _DEND = C_Q_RANK, C_Q_RANK + LANES, C_Q_RANK + LANES + C_KV_RANK


def _mla_in_kernel(x_ref, g_ref, wd_ref, qn_ref, kvn_ref, wuq_ref, wuk_ref, wuvt_ref,
                   c_ref, s_ref, q_ref, k_ref, vt_ref):
    hn = _rms(x_ref[...], g_ref[...]).astype(jnp.bfloat16)
    tm = x_ref.shape[0]
    pad = jnp.zeros((LANES - C_NOPE_DIM - C_ROPE_DIM, tm), jnp.float32)
    c = jnp.concatenate([jnp.ones((C_NOPE_DIM, tm), jnp.float32), c_ref[...], pad], axis=0).T
    s = jnp.concatenate([jnp.zeros((C_NOPE_DIM, tm), jnp.float32), s_ref[...], pad], axis=0).T
    scale = (C_NOPE_DIM + C_ROPE_DIM) ** -0.5 * LOG2E
    cq, sq = c * scale, s * scale

    def rope(y, ct, st):
        return y * ct + pltpu.roll(y, LANES - C_ROPE_DIM, axis=1) * st

    kvd = _dot(hn, wd_ref[:, _DKV:_DEND])
    qr = _dot(hn, wd_ref[:, 0:_DKV])
    ckv = _rms(kvd, kvn_ref[...]).astype(jnp.bfloat16)
    vt_ref[...] = _dot_nt(wuvt_ref[...], ckv).astype(vt_ref.dtype)
    kr = rope(qr[:, _DKR:], c, s)
    n_wide = C_HEADS * LANES // MXU_COLS
    for j in range(n_wide):
        k2 = _dot(ckv, wuk_ref[:, j * MXU_COLS:(j + 1) * MXU_COLS])
        for u in range(MXU_COLS // LANES):
            cols = slice(j * MXU_COLS + u * LANES, j * MXU_COLS + (u + 1) * LANES)
            k_ref[:, cols] = (k2[:, u * LANES:(u + 1) * LANES] + kr).astype(k_ref.dtype)

    qa = _rms(qr[:, :_DKR], qn_ref[...]).astype(jnp.bfloat16)
    up = lambda j: _dot(qa, wuq_ref[:, j * MXU_COLS:(j + 1) * MXU_COLS])
    nxt = up(0)
    for j in range(n_wide):
        q2 = nxt
        if j + 1 < n_wide:
            nxt = up(j + 1)
        for u in range(MXU_COLS // LANES):
            cols = slice(j * MXU_COLS + u * LANES, j * MXU_COLS + (u + 1) * LANES)
            q_ref[:, cols] = rope(q2[:, u * LANES:(u + 1) * LANES], cq, sq).astype(q_ref.dtype)


def _mla_in_proj(x, gain, w_down, q_norm, kv_norm, w_uq, w_uk, w_uvt, tabs):
    n, d = x.shape
    tm = MLA_ROW_TILE
    row = lambda width: pl.BlockSpec((tm, width), lambda i: (i, 0))
    full = lambda a: pl.BlockSpec(a.shape, lambda i: (0,) * a.ndim)
    tab = pl.BlockSpec((C_ROPE_DIM, tm), lambda i: (0, i))
    bf = jnp.bfloat16
    return pl.pallas_call(
        _mla_in_kernel,
        grid=(n // tm,),
        in_specs=[row(d), full(gain), full(w_down), full(q_norm), full(kv_norm),
                  full(w_uq), full(w_uk), full(w_uvt), tab, tab],
        out_specs=(row(C_HEADS * LANES), row(C_HEADS * LANES),
                   pl.BlockSpec((C_HEADS * C_V_DIM, tm), lambda i: (0, i))),
        out_shape=(jax.ShapeDtypeStruct((n, C_HEADS * LANES), bf),
                   jax.ShapeDtypeStruct((n, C_HEADS * LANES), bf),
                   jax.ShapeDtypeStruct((C_HEADS * C_V_DIM, n), bf)),
        compiler_params=_params("arbitrary"),
        name="mla_in_proj",
    )(x, gain, w_down, q_norm, kv_norm, w_uq, w_uk, w_uvt, *tabs)


def _mla_attn_kernel(q_ref, qn_ref, k_ref, vt_ref, o_ref,
                     qt_ref, m_ref, acc_ref, s0_ref, s1_ref):
    i = pl.program_id(2)
    tq, kc, group = MLA_Q_TILE, MLA_KEY_CHUNK, MLA_HEAD_GROUP // 2
    n_full = (i * tq) // kc
    _softmax_init(m_ref, acc_ref)
    for h in range(MLA_HEAD_GROUP):
        qt_ref[h] = q_ref[:, h * LANES:(h + 1) * LANES].astype(jnp.float32).T.astype(
            jnp.bfloat16)

    def issue_qk(c, g, s_ref, width):
        base = pl.multiple_of(c * kc, kc)
        for u in range(group):
            h = g * group + u
            s_ref[u, 0:width, :] = _dot(
                k_ref[pl.ds(base, width), h * LANES:(h + 1) * LANES], qt_ref[h])

    def consume(c, g, s_ref, last, width):
        base = pl.multiple_of(c * kc, kc)
        if last:
            q_pos = i * tq + lax.broadcasted_iota(jnp.int32, (width, tq), 1)
            k_pos = base + lax.broadcasted_iota(jnp.int32, (width, tq), 0)
            bias_t = jnp.where(k_pos <= q_pos, 0.0, NEG_BIG)
        for u in range(group):
            h = g * group + u
            s_t = s_ref[u, 0:width, :]
            if last:
                s_t = s_t + bias_t
            v_t = vt_ref[h * C_V_DIM:(h + 1) * C_V_DIM, pl.ds(base, width)]
            _softmax_step_t(s_t, v_t, m_ref, acc_ref, h)

    def issue_next_tile(s_ref):
        for u in range(group):
            cols = slice(u * LANES, (u + 1) * LANES)
            s_ref[u] = _dot_nt(k_ref[0:kc, cols], qn_ref[:, cols])

    half_last = (i + 1) * tq - n_full * kc <= kc // 2
    _attention_pipeline(n_full, half_last, i == 0, kc, MLA_UNROLL, issue_qk, consume,
                        issue_next_tile, (s0_ref, s1_ref))
    _softmax_finish(acc_ref, o_ref)


def _mla_attention(q, k, vt, batch, seq):
    n = q.shape[0]
    tq, hg = MLA_Q_TILE, MLA_HEAD_GROUP
    nq = seq // tq
    return pl.pallas_call(
        _mla_attn_kernel,
        grid=(batch, C_HEADS // hg, nq),
        in_specs=[pl.BlockSpec((tq, hg * LANES), lambda b, g, i: (b * nq + i, g)),
                  pl.BlockSpec((tq, hg * LANES),
                               lambda b, g, i: (b * nq + jnp.minimum(i + 1, nq - 1), g)),
                  pl.BlockSpec((seq, hg * LANES), lambda b, g, i: (b, g)),
                  pl.BlockSpec((hg * C_V_DIM, seq), lambda b, g, i: (g, b))],
        out_specs=pl.BlockSpec((hg * C_V_DIM, tq), lambda b, g, i: (g, b * nq + i)),
        out_shape=jax.ShapeDtypeStruct((C_HEADS * C_V_DIM, n), jnp.bfloat16),
        scratch_shapes=[pltpu.VMEM((hg, LANES, tq), jnp.bfloat16),
                        pltpu.VMEM((hg, 1, tq), jnp.float32),
                        pltpu.VMEM((hg, C_V_DIM + SUM_ROWS, tq), jnp.float32),
                        pltpu.VMEM((hg // 2, MLA_KEY_CHUNK, tq), jnp.float32),
                        pltpu.VMEM((hg // 2, MLA_KEY_CHUNK, tq), jnp.float32)],
        compiler_params=_params("arbitrary", "arbitrary", "arbitrary"),
        name="mla_attention",
    )(q, q, k, vt)


def _rope_rows(positions, rot_dim):
    inv_freq = ROPE_THETA ** (-jnp.arange(0, rot_dim, 2, dtype=jnp.float32) / rot_dim)
    ang = inv_freq[:, None] * positions.astype(jnp.float32).reshape(1, -1)
    return jnp.cos(ang), jnp.sin(ang)


def _even_rope_tables(positions):
    cos, sin = _rope_rows(positions, A_ROT_DIM)
    zero = jnp.zeros_like(sin)
    return (jnp.concatenate([cos, cos], axis=0), jnp.concatenate([-sin, zero], axis=0),
            jnp.concatenate([zero, sin], axis=0))


def _pack_even_w_in(w):
    d = w.shape[0]
    w = w.astype(jnp.bfloat16)
    q, k, v, qi, ki, wi, gb, gc, xi = jnp.split(
        w, [512, 1024, 1536, 2560, 2624, 2640, 3152, 3664], axis=1)
    pad = jnp.zeros((d, LANES - IDX_HEADS), w.dtype)
    return jnp.concatenate([q, k, qi, ki, ki, wi, pad, gb, gc, xi], axis=1), v.T


def _pad_heads(w, heads, width):
    r = w.shape[0]
    w = w.reshape(r, heads, width)
    return jnp.pad(w, ((0, 0), (0, 0), (0, LANES - width))).reshape(r, heads * LANES)


def _rope_swapped(w_rope):
    half = C_ROPE_DIM // 2
    return jnp.concatenate([w_rope, w_rope[..., half:], w_rope[..., :half]], axis=-1)


def _mla_rope_tables(positions):
    cos, sin = _rope_rows(positions, C_ROPE_DIM)
    return jnp.concatenate([cos, cos], axis=0), jnp.concatenate([-sin, sin], axis=0)


def kernel(x, positions, norm_mix_pre, norm_mix_post, norm_ffn_pre, norm_ffn_post, even_w_in, even_conv_w, even_w_out, odd_w_dq, odd_q_norm, odd_w_uq, odd_w_dkv, odd_kv_norm, odd_w_ukv, odd_w_o, mlp_w1, mlp_w2):
    batch, seq, d = x.shape
    depth = norm_mix_pre.shape[0]
    bf = jnp.bfloat16
    tabs_a = _even_rope_tables(positions)
    tabs_c = _mla_rope_tables(positions)
    h = x.reshape(batch * seq, d)
    row = lambda a: a.reshape(1, -1)
    for layer in range(depth):
        j = layer // 2
        g_pre, g_post = row(norm_mix_pre[layer]), row(norm_mix_post[layer])
        if layer % 2 == 0:
            w_packed, w_vt = _pack_even_w_in(even_w_in[j])
            q, k, vt, qi, ki, wi, conv = _even_in_proj(
                h, g_pre, w_packed, w_vt, tabs_a, even_conv_w[j], seq)
            attn = _dsa_attention(q, qi, wi, ki, k, vt, batch, seq)
            w_out = even_w_out[j].astype(bf)
            acts, w_outs = [attn, conv], [w_out[:A_WIDTH], w_out[A_WIDTH:]]
        else:
            w_dkv = odd_w_dkv[j].astype(bf)
            w_down = jnp.concatenate(
                [odd_w_dq[j].astype(bf), jnp.zeros((d, C_NOPE_DIM), bf),
                 _rope_swapped(w_dkv[:, C_KV_RANK:]), w_dkv[:, :C_KV_RANK]], axis=1)
            w_ukv = odd_w_ukv[j].astype(bf).reshape(C_KV_RANK, C_HEADS, C_NOPE_DIM + C_V_DIM)
            w_uk = _pad_heads(w_ukv[:, :, :C_NOPE_DIM].reshape(C_KV_RANK, -1), C_HEADS, C_NOPE_DIM)
            w_uv = w_ukv[:, :, C_NOPE_DIM:].reshape(C_KV_RANK, -1)
            w_uq = odd_w_uq[j].astype(bf).reshape(C_Q_RANK, C_HEADS, C_NOPE_DIM + C_ROPE_DIM)
            w_uq = jnp.concatenate(
                [w_uq[:, :, :C_NOPE_DIM], _rope_swapped(w_uq[:, :, C_NOPE_DIM:])],
                axis=-1).reshape(C_Q_RANK, C_HEADS * LANES)
            q, k, vt = _mla_in_proj(h, g_pre, w_down, row(odd_q_norm[j]), row(odd_kv_norm[j]),
                                    w_uq, w_uk, w_uv.T, tabs_c)
            acts, w_outs = [_mla_attention(q, k, vt, batch, seq)], [odd_w_o[j].astype(bf)]
        h = _mix_mlp(acts, w_outs, h, g_post, row(norm_ffn_pre[layer]),
                     row(norm_ffn_post[layer]), mlp_w1[layer].astype(bf),
                     mlp_w2[layer].astype(bf))
    return h.reshape(batch, seq, d)
```
